```python
import math
import jax, jax.numpy as jnp
from jax import lax
import numpy as np

D_MODEL = 2048
BATCH = 4
SEQ = 8192
DEPTH = 4

N_MIXERS = 3
N_RWKV = (DEPTH + 2) // 3
N_MLSTM = (DEPTH + 1) // 3
N_DSA = DEPTH // 3

RMS_EPS = 1e-6
OUT_SCALE = (2 * DEPTH) ** -0.5

D_FF = -(-8 * D_MODEL // (3 * 256)) * 256

RWKV_HEAD = 64
RWKV_HEADS = D_MODEL // RWKV_HEAD
RWKV_DECAY_LORA = max(32, int(round(1.8 * D_MODEL ** 0.5 / 32)) * 32)
RWKV_AAA_LORA = max(32, int(round(1.8 * D_MODEL ** 0.5 / 32)) * 32)
RWKV_MV_LORA = max(32, int(round(1.3 * D_MODEL ** 0.5 / 32)) * 32)
RWKV_GATE_LORA = max(32, int(round(0.6 * D_MODEL ** 0.8 / 32)) * 32)
RWKV_DECAY_SCALE = math.exp(-0.5)
RWKV_GN_EPS = 64e-5

MLSTM_HEADS = 4
MLSTM_DQK = D_MODEL // 2 // MLSTM_HEADS
MLSTM_DV = D_MODEL // MLSTM_HEADS
MLSTM_CHUNK = 64
MLSTM_GATE_CAP = 15.0

DSA_HEADS = 16
DSA_HEAD_DIM = 128
IDX_HEADS = 16
IDX_DIM = 64
DSA_TOPK_MAX = 256
DSA_Q_BLOCK = 128
T5_BUCKETS = 32
T5_MAX_DISTANCE = 128

kernel_name = 'hybrid_rwkv7_mlstm_dsa_trunk'


def _rmsnorm(x, g):
    xf = x.astype(jnp.float32)
    y = xf * lax.rsqrt(jnp.mean(xf * xf, axis=-1, keepdims=True) + RMS_EPS)
    return (y * g.astype(jnp.float32)).astype(x.dtype)


def _swiglu(x, w_gate, w_up, w_down):
    return (jax.nn.silu(x @ w_gate) * (x @ w_up)) @ w_down


def _split_last(t, sizes):
    cuts = np.cumsum(sizes)[:-1].tolist()
    return jnp.split(t, cuts, axis=-1)


def _rwkv7_scan(r, w, k, v, a, b):
    bsz, _, h, n = r.shape

    def step(state, inp):
        r_t, w_t, k_t, v_t, a_t, b_t = inp
        sa = jnp.einsum('bhvk,bhk->bhv', state, a_t)
        state = (state * w_t[:, :, None, :] + sa[..., None] * b_t[:, :, None, :]
                 + v_t[..., None] * k_t[:, :, None, :])
        return state, jnp.einsum('bhvk,bhk->bhv', state, r_t)

    xs = tuple(jnp.moveaxis(t, 1, 0) for t in (r, w, k, v, a, b))
    state0 = jnp.zeros((bsz, h, n, n), jnp.float32)
    _, ys = lax.scan(step, state0, xs)
    return jnp.moveaxis(ys, 0, 1)


def _rwkv7_mix(x, mu, w_rkv, w0, w1, w2, a0, a1, a2, g1, g2, k_k, k_a, r_k,
               lnx_g, lnx_b, w_o, v_first, vres):
    bsz, seq, d = x.shape
    H, N = RWKV_HEADS, RWKV_HEAD
    f32 = jnp.float32
    x_prev = jnp.pad(x, ((0, 0), (1, 0), (0, 0)))[:, :-1]
    xx = x_prev - x
    xr, xw, xk, xv, xa, xg = (x + xx * mu[i] for i in range(6))
    r = xr @ w_rkv[0]
    k = xk @ w_rkv[1]
    v = xv @ w_rkv[2]
    if vres is None:
        v_first = v
    else:
        v0, v1, v2 = vres
        v = v + (v_first - v) * jax.nn.sigmoid(v0 + (xv @ v1) @ v2)
    decay = jnp.exp(-RWKV_DECAY_SCALE * jax.nn.sigmoid((w0 + jnp.tanh(xw @ w1) @ w2).astype(f32)))
    a = jax.nn.sigmoid(a0 + (xa @ a1) @ a2).astype(f32)
    g = jax.nn.sigmoid(xg @ g1) @ g2
    heads = lambda t: t.astype(f32).reshape(bsz, seq, H, N)
    kk = heads(k * k_k)
    kk = kk / jnp.maximum(jnp.sqrt(jnp.sum(kk * kk, axis=-1, keepdims=True)), 1e-12)
    k = k.astype(f32) * (1.0 + (a - 1.0) * k_a.astype(f32))
    rh, wh, kh, vh, ah = heads(r), heads(decay), heads(k), heads(v), heads(a)
    y = _rwkv7_scan(rh, wh, kh, vh, -kk, kk * ah)
    mean = jnp.mean(y, axis=-1, keepdims=True)
    var = jnp.mean(jnp.square(y - mean), axis=-1, keepdims=True)
    y = ((y - mean) * lax.rsqrt(var + RWKV_GN_EPS) * lnx_g.astype(f32).reshape(H, N)
         + lnx_b.astype(f32).reshape(H, N))
    bonus = jnp.sum(rh * kh * r_k.astype(f32), axis=-1, keepdims=True) * vh
    out = ((y + bonus).reshape(bsz, seq, d) * g.astype(f32)).astype(x.dtype)
    return out @ w_o, v_first


def _soft_cap(z):
    return MLSTM_GATE_CAP * jnp.tanh(z / MLSTM_GATE_CAP)


def _mlstm_chunkwise(q, k, v, li, lf):
    bsz, h, seq, dk = q.shape
    dv = v.shape[-1]
    L = MLSTM_CHUNK
    nc = seq // L
    chunks = lambda t: jnp.moveaxis(t.reshape(bsz, h, nc, L, *t.shape[3:]), 2, 0)
    causal = jnp.tril(jnp.ones((L, L), dtype=bool))

    def step(carry, inp):
        c_st, n_st, m_st = carry
        qc, kc, vc, lic, lfc = inp
        bcum = jnp.cumsum(lfc, axis=-1)
        dmat = jnp.where(causal, bcum[..., :, None] - bcum[..., None, :] + lic[..., None, :], -jnp.inf)
        inter = bcum + m_st[..., None]
        m_t = jnp.maximum(inter, jnp.max(dmat, axis=-1))
        sc = jnp.einsum('bhtd,bhsd->bhts', qc, kc) * jnp.exp(dmat - m_t[..., None])
        w_inter = jnp.exp(inter - m_t)
        num = (jnp.einsum('bhts,bhsv->bhtv', sc, vc)
               + w_inter[..., None] * jnp.einsum('bhtd,bhdv->bhtv', qc, c_st))
        den = jnp.sum(sc, axis=-1) + w_inter * jnp.einsum('bhtd,bhd->bht', qc, n_st)
        h_c = num / jnp.maximum(jnp.abs(den), jnp.exp(-m_t))[..., None]
        b_tot = bcum[..., -1]
        log_wk = b_tot[..., None] - bcum + lic
        m_new = jnp.maximum(b_tot + m_st, jnp.max(log_wk, axis=-1))
        carry_decay = jnp.exp(b_tot + m_st - m_new)
        wk = jnp.exp(log_wk - m_new[..., None])
        c_new = carry_decay[..., None, None] * c_st + jnp.einsum('bhs,bhsd,bhsv->bhdv', wk, kc, vc)
        n_new = carry_decay[..., None] * n_st + jnp.einsum('bhs,bhsd->bhd', wk, kc)
        return (c_new, n_new, m_new), h_c

    carry0 = (jnp.zeros((bsz, h, dk, dv), jnp.float32),
              jnp.zeros((bsz, h, dk), jnp.float32),
              jnp.zeros((bsz, h), jnp.float32))
    _, hs = lax.scan(step, carry0, tuple(chunks(t) for t in (q, k, v, li, lf)))
    return jnp.moveaxis(hs, 0, 2).reshape(bsz, h, seq, dv)


def _mlstm_mix(x, w_in, b_if, norm_g, w_o):
    bsz, seq, _ = x.shape
    H, DK, DV = MLSTM_HEADS, MLSTM_DQK, MLSTM_DV
    f32 = jnp.float32
    q, k, v, o, ig, fg = _split_last(x @ w_in, [H * DK, H * DK, H * DV, H * DV, H, H])
    to_heads = lambda t, dd: jnp.transpose(t.astype(f32).reshape(bsz, seq, H, dd), (0, 2, 1, 3))
    li = jnp.transpose(_soft_cap(ig.astype(f32) + b_if[0].astype(f32)), (0, 2, 1))
    lf = jnp.transpose(jax.nn.log_sigmoid(_soft_cap(fg.astype(f32) + b_if[1].astype(f32))), (0, 2, 1))
    h = _mlstm_chunkwise(to_heads(q, DK) * DK ** -0.5, to_heads(k, DK), to_heads(v, DV), li, lf)
    h = _rmsnorm(jnp.transpose(h, (0, 2, 1, 3)), norm_g.reshape(H, DV))
    out = (h.reshape(bsz, seq, H * DV) * jax.nn.sigmoid(o.astype(f32))).astype(x.dtype)
    return out @ w_o


def _t5_bucket(rel):
    n = jnp.maximum(rel, 0)
    exact = T5_BUCKETS // 2
    nf = jnp.maximum(n, exact).astype(jnp.float32)
    large = exact + (jnp.log(nf / exact) / math.log(T5_MAX_DISTANCE / exact)
                     * (T5_BUCKETS - exact)).astype(jnp.int32)
    return jnp.where(n < exact, n, jnp.minimum(large, T5_BUCKETS - 1))


def _dsa_mix(x, w_in, q_norm_g, k_norm_g, t5_table, w_o):
    bsz, seq, _ = x.shape
    H, DH, HI, DI = DSA_HEADS, DSA_HEAD_DIM, IDX_HEADS, IDX_DIM
    f32 = jnp.float32
    q, k, v, qi, ki, wi = _split_last(x @ w_in, [H * DH, DH, DH, HI * DI, DI, HI])
    q = _rmsnorm(q.reshape(bsz, seq, H, DH), q_norm_g)
    k = _rmsnorm(k, k_norm_g)
    qi = qi.reshape(bsz, seq, HI, DI)
    wi = wi * HI ** -0.5
    n_sel = min(DSA_TOPK_MAX, seq // 4)
    nb = seq // DSA_Q_BLOCK
    pos = jnp.arange(seq, dtype=jnp.int32)
    blocks = lambda t: jnp.moveaxis(t.reshape(bsz, nb, DSA_Q_BLOCK, *t.shape[2:]), 1, 0)

    def attend_block(inp):
        qb, qib, wib, tpos = inp
        idx_logits = jnp.einsum('bthd,bsd->bths', qib, ki) * DI ** -0.5
        score = jnp.einsum('bths,bth->bts', jax.nn.relu(idx_logits), wib).astype(f32)
        score = jnp.where((pos[None, :] <= tpos[:, None])[None], score, -jnp.inf)
        _, sel = lax.top_k(score, n_sel)
        k_sel = jax.vmap(lambda kb, ib: kb[ib])(k, sel)
        v_sel = jax.vmap(lambda vb, ib: vb[ib])(v, sel)
        rel = tpos[None, :, None] - sel
        bias = jnp.moveaxis(t5_table[_t5_bucket(rel)], -1, 2).astype(f32)
        logits = jnp.einsum('bthd,btjd->bthj', qb, k_sel).astype(f32) * DH ** -0.5 + bias
        logits = jnp.where((rel >= 0)[:, :, None, :], logits, -jnp.inf)
        p = jax.nn.softmax(logits, axis=-1).astype(v.dtype)
        return jnp.einsum('bthj,btjd->bthd', p, v_sel).reshape(bsz, DSA_Q_BLOCK, H * DH)

    out = lax.map(attend_block, (blocks(q), blocks(qi), blocks(wi), pos.reshape(nb, DSA_Q_BLOCK)))
    out = jnp.moveaxis(out, 0, 1).reshape(bsz, seq, H * DH)
    return out @ w_o


def setup_inputs(seed: int = 0) -> dict:
    key = jax.random.key(seed)
    keys = jax.random.split(key, 64)
    counter = [0]

    def nxt():
        kk = keys[counter[0]]
        counter[0] += 1
        return kk

    def nrm(shape, scale):
        return jax.random.normal(nxt(), shape, jnp.float32) * scale

    def gain(shape):
        return 1.0 + nrm(shape, 0.05)

    D = D_MODEL
    nA, nB, nC = N_RWKV, N_MLSTM, N_DSA
    nV = max(N_RWKV - 1, 0)
    inp = {}
    inp['x'] = nrm((BATCH, SEQ, D), 1.0)
    inp['rwkv_mu'] = jax.random.uniform(nxt(), (nA, 6, D), jnp.float32)
    inp['rwkv_w_rkv'] = nrm((nA, 3, D, D), D ** -0.5)
    inp['rwkv_w0'] = nrm((nA, D), 1.0)
    inp['rwkv_w1'] = nrm((nA, D, RWKV_DECAY_LORA), D ** -0.5)
    inp['rwkv_w2'] = nrm((nA, RWKV_DECAY_LORA, D), 0.1 * RWKV_DECAY_LORA ** -0.5)
    inp['rwkv_a0'] = nrm((nA, D), 0.1)
    inp['rwkv_a1'] = nrm((nA, D, RWKV_AAA_LORA), D ** -0.5)
    inp['rwkv_a2'] = nrm((nA, RWKV_AAA_LORA, D), 0.1 * RWKV_AAA_LORA ** -0.5)
    inp['rwkv_v0'] = nrm((nV, D), 0.1)
    inp['rwkv_v1'] = nrm((nV, D, RWKV_MV_LORA), D ** -0.5)
    inp['rwkv_v2'] = nrm((nV, RWKV_MV_LORA, D), 0.1 * RWKV_MV_LORA ** -0.5)
    inp['rwkv_g1'] = nrm((nA, D, RWKV_GATE_LORA), D ** -0.5)
    inp['rwkv_g2'] = nrm((nA, RWKV_GATE_LORA, D), RWKV_GATE_LORA ** -0.5)
    inp['rwkv_k_k'] = 0.85 + nrm((nA, D), 0.05)
    inp['rwkv_k_a'] = gain((nA, D))
    inp['rwkv_r_k'] = nrm((nA, RWKV_HEADS, RWKV_HEAD), 0.1)
    inp['rwkv_lnx_g'] = gain((nA, D))
    inp['rwkv_lnx_b'] = nrm((nA, D), 0.02)
    inp['rwkv_w_o'] = nrm((nA, D, D), D ** -0.5 * OUT_SCALE)
    p_m = 2 * MLSTM_HEADS * MLSTM_DQK + 2 * MLSTM_HEADS * MLSTM_DV + 2 * MLSTM_HEADS
    inp['mlstm_w_in'] = nrm((nB, D, p_m), D ** -0.5)
    f_bias = jnp.linspace(3.0, 6.0, MLSTM_HEADS, dtype=jnp.float32)
    inp['mlstm_b_if'] = jnp.stack([nrm((nB, MLSTM_HEADS), 0.1),
                                   f_bias + nrm((nB, MLSTM_HEADS), 0.1)], axis=1)
    inp['mlstm_norm_g'] = gain((nB, MLSTM_HEADS * MLSTM_DV))
    inp['mlstm_w_o'] = nrm((nB, MLSTM_HEADS * MLSTM_DV, D), (MLSTM_HEADS * MLSTM_DV) ** -0.5 * OUT_SCALE)
    p_c = DSA_HEADS * DSA_HEAD_DIM + 2 * DSA_HEAD_DIM + IDX_HEADS * IDX_DIM + IDX_DIM + IDX_HEADS
    inp['dsa_w_in'] = nrm((nC, D, p_c), D ** -0.5)
    inp['dsa_q_norm_g'] = gain((nC, DSA_HEAD_DIM))
    inp['dsa_k_norm_g'] = gain((nC, DSA_HEAD_DIM))
    inp['dsa_w_o'] = nrm((nC, DSA_HEADS * DSA_HEAD_DIM, D), (DSA_HEADS * DSA_HEAD_DIM) ** -0.5 * OUT_SCALE)
    inp['t5_bias'] = nrm((T5_BUCKETS, DSA_HEADS), 0.5)
    inp['mix_norm_g'] = gain((DEPTH, D))
    inp['ffn_norm_g'] = gain((DEPTH, D))
    inp['ffn_w_gate'] = nrm((DEPTH, D, D_FF), D ** -0.5)
    inp['ffn_w_up'] = nrm((DEPTH, D, D_FF), D ** -0.5)
    inp['ffn_w_down'] = nrm((DEPTH, D_FF, D), D_FF ** -0.5 * OUT_SCALE)
    return inp


def reference(x, rwkv_mu, rwkv_w_rkv, rwkv_w0, rwkv_w1, rwkv_w2, rwkv_a0, rwkv_a1, rwkv_a2,
              rwkv_v0, rwkv_v1, rwkv_v2, rwkv_g1, rwkv_g2, rwkv_k_k, rwkv_k_a, rwkv_r_k,
              rwkv_lnx_g, rwkv_lnx_b, rwkv_w_o, mlstm_w_in, mlstm_b_if, mlstm_norm_g, mlstm_w_o,
              dsa_w_in, dsa_q_norm_g, dsa_k_norm_g, dsa_w_o, t5_bias, mix_norm_g, ffn_norm_g,
              ffn_w_gate, ffn_w_up, ffn_w_down):
    v_first = None
    for i in range(DEPTH):
        kind, j = i % N_MIXERS, i // N_MIXERS
        h = _rmsnorm(x, mix_norm_g[i])
        if kind == 0:
            vres = None if j == 0 else (rwkv_v0[j - 1], rwkv_v1[j - 1], rwkv_v2[j - 1])
            y, v_first = _rwkv7_mix(h, rwkv_mu[j], rwkv_w_rkv[j], rwkv_w0[j], rwkv_w1[j], rwkv_w2[j],
                                    rwkv_a0[j], rwkv_a1[j], rwkv_a2[j], rwkv_g1[j], rwkv_g2[j],
                                    rwkv_k_k[j], rwkv_k_a[j], rwkv_r_k[j], rwkv_lnx_g[j],
                                    rwkv_lnx_b[j], rwkv_w_o[j], v_first, vres)
        elif kind == 1:
            y = _mlstm_mix(h, mlstm_w_in[j], mlstm_b_if[j], mlstm_norm_g[j], mlstm_w_o[j])
        else:
            y = _dsa_mix(h, dsa_w_in[j], dsa_q_norm_g[j], dsa_k_norm_g[j], t5_bias, dsa_w_o[j])
        x = x + y.astype(x.dtype)
        h = _rmsnorm(x, ffn_norm_g[i])
        x = x + _swiglu(h, ffn_w_gate[i], ffn_w_up[i], ffn_w_down[i])
    return x
```

```python
import functools
import math

import jax
import jax.numpy as jnp
from jax import lax
from jax.experimental import pallas as pl
from jax.experimental.pallas import tpu as pltpu

F32 = jnp.float32
BF16 = jnp.bfloat16

V7X_VMEM_LIMIT_BYTES = 56 * 1024 * 1024
LANES = 128

RMS_EPS = 1e-6
RWKV_HEAD = 64
RWKV_DECAY_SCALE = math.exp(-0.5)
RWKV_GN_EPS = 64e-5
RWKV_CHUNK = 64
MLSTM_HEADS = 4
MLSTM_GATE_CAP = 15.0
DSA_HEADS = 16
DSA_HEAD_DIM = 128
IDX_HEADS = 16
IDX_DIM = 64
DSA_TOPK_MAX = 256
T5_BUCKETS = 32
T5_MAX_DISTANCE = 128


def _cparams(*sem):
    return pltpu.CompilerParams(dimension_semantics=sem, vmem_limit_bytes=V7X_VMEM_LIMIT_BYTES)


def _dot(a, b):
    return jnp.dot(a.astype(BF16), b.astype(BF16), preferred_element_type=F32)


def _dot_nt(a, b):
    return lax.dot_general(a.astype(BF16), b.astype(BF16), (((1,), (1,)), ((), ())),
                           preferred_element_type=F32)


def _dot_tn(a, b):
    return lax.dot_general(a.astype(BF16), b.astype(BF16), (((0,), (0,)), ((), ())),
                           preferred_element_type=F32)


def _split3(x):
    hi = x.astype(BF16)
    r1 = x - hi.astype(F32)
    mid = r1.astype(BF16)
    lo = (r1 - mid.astype(F32)).astype(BF16)
    return hi, mid, lo


def _dot_exact_lhs(a_bf16, x):
    hi, mid, lo = _split3(x)
    d = lambda p: jnp.dot(a_bf16, p, preferred_element_type=F32)
    return d(hi) + (d(mid) + d(lo))


def _dot_exact_rhs(x, b_bf16):
    hi, mid, lo = _split3(x)
    d = lambda p: jnp.dot(p, b_bf16, preferred_element_type=F32)
    return d(hi) + (d(mid) + d(lo))


def _rms(x, g):
    ms = jnp.mean(x * x, axis=-1, keepdims=True)
    return x * lax.rsqrt(ms + RMS_EPS) * g


def _sigmoid(x):
    return 1.0 / (1.0 + jnp.exp(-x))


def _mm_kernel(a_ref, w_ref, o_ref):
    o_ref[...] = jnp.dot(a_ref[...], w_ref[...], preferred_element_type=F32).astype(o_ref.dtype)


def _mm_res_kernel(a_ref, w_ref, r_ref, o_ref):
    o_ref[...] = (r_ref[...] + jnp.dot(a_ref[...], w_ref[...], preferred_element_type=F32)).astype(o_ref.dtype)


def _mm(a, w, res=None, out_dtype=F32, tm=512, tn=512):
    m, k = a.shape
    n = w.shape[1]
    tm, tn = min(tm, m), min(tn, n)
    assert m % tm == 0 and n % tn == 0
    in_specs = [pl.BlockSpec((tm, k), lambda i, j: (i, 0)),
                pl.BlockSpec((k, tn), lambda i, j: (0, j))]
    args = [a, w]
    kern = _mm_kernel
    if res is not None:
        in_specs.append(pl.BlockSpec((tm, tn), lambda i, j: (i, j)))
        args.append(res)
        kern = _mm_res_kernel
    return pl.pallas_call(
        kern, grid=(m // tm, n // tn), in_specs=in_specs,
        out_specs=pl.BlockSpec((tm, tn), lambda i, j: (i, j)),
        out_shape=jax.ShapeDtypeStruct((m, n), out_dtype),
        compiler_params=_cparams("parallel", "arbitrary"), name="mm")(*args)


def _bmm(a, w, out_dtype=F32, tm=512, tn=512):
    g, m, k = a.shape
    n = w.shape[2]
    tm, tn = min(tm, m), min(tn, n)
    assert m % tm == 0 and n % tn == 0
    return pl.pallas_call(
        _mm_kernel, grid=(g, m // tm, n // tn),
        in_specs=[pl.BlockSpec((None, tm, k), lambda b, i, j: (b, i, 0)),
                  pl.BlockSpec((None, k, tn), lambda b, i, j: (b, 0, j))],
        out_specs=pl.BlockSpec((None, tm, tn), lambda b, i, j: (b, i, j)),
        out_shape=jax.ShapeDtypeStruct((g, m, n), out_dtype),
        compiler_params=_cparams("parallel", "parallel", "arbitrary"), name="bmm")(a, w)


def _norm_mm_kernel(x_ref, g_ref, w_ref, o_ref, h_ref):
    @pl.when(pl.program_id(1) == 0)
    def _():
        h_ref[...] = _rms(x_ref[...], g_ref[...]).astype(BF16)

    o_ref[...] = jnp.dot(h_ref[...], w_ref[...], preferred_element_type=F32).astype(o_ref.dtype)


def _norm_mm(x, g, w, out_dtype=F32, tm=512, tn=512):
    m, k = x.shape
    n = w.shape[1]
    tm, tn = min(tm, m), min(tn, n)
    assert m % tm == 0 and n % tn == 0
    return pl.pallas_call(
        _norm_mm_kernel, grid=(m // tm, n // tn),
        in_specs=[pl.BlockSpec((tm, k), lambda i, j: (i, 0)),
                  pl.BlockSpec((1, k), lambda i, j: (0, 0)),
                  pl.BlockSpec((k, tn), lambda i, j: (0, j))],
        out_specs=pl.BlockSpec((tm, tn), lambda i, j: (i, j)),
        out_shape=jax.ShapeDtypeStruct((m, n), out_dtype),
        scratch_shapes=[pltpu.VMEM((tm, k), BF16)],
        compiler_params=_cparams("parallel", "arbitrary"), name="norm_mm")(x, g.reshape(1, k), w)


def _ffn_kernel(x_ref, g_ref, wg_ref, wu_ref, wd_ref, o_ref, h_ref):
    @pl.when(pl.program_id(1) == 0)
    def _():
        x = x_ref[...]
        h_ref[...] = _rms(x, g_ref[...]).astype(BF16)
        o_ref[...] = x

    h = h_ref[...]
    gate = jnp.dot(h, wg_ref[...], preferred_element_type=F32)
    up = jnp.dot(h, wu_ref[...], preferred_element_type=F32)
    act = (gate * _sigmoid(gate) * up).astype(BF16)
    o_ref[...] += jnp.dot(act, wd_ref[...], preferred_element_type=F32)


def _ffn(x, g, wg, wu, wd, tm=512, tf=512):
    m, d = x.shape
    f = wg.shape[1]
    tm, tf = min(tm, m), min(tf, f)
    assert m % tm == 0 and f % tf == 0
    return pl.pallas_call(
        _ffn_kernel, grid=(m // tm, f // tf),
        in_specs=[pl.BlockSpec((tm, d), lambda i, j: (i, 0)),
                  pl.BlockSpec((1, d), lambda i, j: (0, 0)),
                  pl.BlockSpec((d, tf), lambda i, j: (0, j)),
                  pl.BlockSpec((d, tf), lambda i, j: (0, j)),
                  pl.BlockSpec((tf, d), lambda i, j: (j, 0))],
        out_specs=pl.BlockSpec((tm, d), lambda i, j: (i, 0)),
        out_shape=jax.ShapeDtypeStruct((m, d), F32),
        scratch_shapes=[pltpu.VMEM((tm, d), BF16)],
        compiler_params=_cparams("parallel", "arbitrary"), name="ffn")(x, g.reshape(1, d), wg, wu, wd)


def _rwkv_prep_kernel(*refs, seq, tm, has_v):
    if has_v:
        (x_ref, xp_ref, g_ref, mu_ref, w0_ref, w1_ref, w2_ref, a0_ref, a1_ref, a2_ref,
         g1_ref, g2_ref, v0_ref, v1_ref, v2_ref, xs_ref, lw_ref, a_ref, gate_ref, vg_ref) = refs
    else:
        (x_ref, xp_ref, g_ref, mu_ref, w0_ref, w1_ref, w2_ref, a0_ref, a1_ref, a2_ref,
         g1_ref, g2_ref, xs_ref, lw_ref, a_ref, gate_ref) = refs
    i = pl.program_id(0)
    gn = g_ref[...]
    h = _rms(x_ref[...], gn)
    hp = _rms(xp_ref[...], gn)
    seq_start = (i * tm) % seq == 0
    hp_row = jnp.where(seq_start, 0.0, hp[7:8, :])
    row = lax.broadcasted_iota(jnp.int32, (tm, 1), 0)
    h_prev = jnp.where(row == 0, hp_row, pltpu.roll(h, 1, 0))
    xx = h_prev - h
    mix = lambda n: h + xx * mu_ref[n:n + 1, :]
    xs_ref[0] = mix(0).astype(BF16)
    xs_ref[1] = mix(2).astype(BF16)
    xv = mix(3).astype(BF16)
    xs_ref[2] = xv
    lw_ref[...] = -RWKV_DECAY_SCALE * _sigmoid(
        w0_ref[...] + _dot(jnp.tanh(_dot(mix(1), w1_ref[...])), w2_ref[...]))
    a_ref[...] = _sigmoid(a0_ref[...] + _dot(_dot(mix(4), a1_ref[...]), a2_ref[...]))
    gate_ref[...] = _dot(_sigmoid(_dot(mix(5), g1_ref[...])), g2_ref[...])
    if has_v:
        vg_ref[...] = _sigmoid(v0_ref[...] + _dot(_dot(xv, v1_ref[...]), v2_ref[...]))


def _pad_lora(w_in, w_out):
    r = w_in.shape[1]
    rp = -(-r // LANES) * LANES
    return (jnp.pad(w_in, ((0, 0), (0, rp - r))).astype(BF16),
            jnp.pad(w_out, ((0, rp - r), (0, 0))).astype(BF16))


def _rwkv_prep(x, norm_g, mu, w0, w1, w2, a0, a1, a2, g1, g2, vres, seq, tm=256):
    t, d = x.shape
    tm = min(tm, seq)
    assert t % tm == 0 and seq % tm == 0 and tm % 8 == 0
    has_v = vres is not None
    row = lambda v: v.reshape(1, d)
    full = lambda a: pl.BlockSpec(a.shape, lambda i: (0,) * a.ndim)
    w1p, w2p = _pad_lora(w1, w2)
    a1p, a2p = _pad_lora(a1, a2)
    g1p, g2p = _pad_lora(g1, g2)
    mu8 = jnp.pad(mu, ((0, 2), (0, 0)))
    params = [row(norm_g), mu8, row(w0), w1p, w2p, row(a0), a1p, a2p, g1p, g2p]
    if has_v:
        v1p, v2p = _pad_lora(vres[1], vres[2])
        params += [row(vres[0]), v1p, v2p]
    tile = pl.BlockSpec((tm, d), lambda i: (i, 0))
    in_specs = [tile, pl.BlockSpec((8, d), lambda i: (jnp.maximum(i * (tm // 8) - 1, 0), 0))]
    in_specs += [full(p) for p in params]
    n_f32 = 4 if has_v else 3
    out_shape = [jax.ShapeDtypeStruct((3, t, d), BF16)] + [jax.ShapeDtypeStruct((t, d), F32)] * n_f32
    out_specs = [pl.BlockSpec((3, tm, d), lambda i: (0, i, 0))] + [tile] * n_f32
    return pl.pallas_call(
        functools.partial(_rwkv_prep_kernel, seq=seq, tm=tm, has_v=has_v),
        grid=(t // tm,), in_specs=in_specs, out_specs=out_specs, out_shape=out_shape,
        compiler_params=_cparams("parallel"), name="rwkv_prep")(x, x, *params)


def _seg_sum(x, seg):
    w = x.shape[1]
    return jnp.concatenate([_dot_exact_rhs(x[:, q:q + 256], seg) for q in range(0, w, 256)], axis=1)


def _rwkv_scan_kernel(*refs, L, tc, has_v):
    if has_v:
        (r_ref, k_ref, v_ref, lw_ref, a_ref, gate_ref, vf_ref, vg_ref,
         kk_ref, ka_ref, rk_ref, lng_ref, lnb_ref, o_ref, s_ref) = refs
    else:
        (r_ref, k_ref, v_ref, lw_ref, a_ref, gate_ref,
         kk_ref, ka_ref, rk_ref, lng_ref, lnb_ref, o_ref, s_ref) = refs
    W = r_ref.shape[-1]
    P2 = 2 * L
    npair = W // P2
    N = RWKV_HEAD

    @pl.when(pl.program_id(2) == 0)
    def _():
        s_ref[...] = jnp.zeros_like(s_ref)

    ri = lax.broadcasted_iota(jnp.int32, (P2, P2), 0)
    ci = lax.broadcasted_iota(jnp.int32, (P2, P2), 1)
    strict = ri > ci
    incl = ri >= ci
    eye = (ri == ci).astype(F32)
    levels = []
    s = 1
    while s < L:
        levels.append(((ri // s) % 2 == 1) & ((ci // s) == (ri // s) - 1))
        s *= 2
    head0 = lax.broadcasted_iota(jnp.int32, (L, P2), 1) < N
    sr = lax.broadcasted_iota(jnp.int32, (256, 256), 0) // N
    sc = lax.broadcasted_iota(jnp.int32, (256, 256), 1) // N
    seg = (sr == sc).astype(BF16)
    tri = (lax.broadcasted_iota(jnp.int32, (L, L), 0) >= lax.broadcasted_iota(jnp.int32, (L, L), 1)).astype(BF16)

    def stack(z):
        return jnp.concatenate([jnp.where(head0, z, 0.0), jnp.where(head0, 0.0, z)], axis=0).astype(BF16)

    def chunk(cidx, carry):
        sl = pl.ds(pl.multiple_of(cidx * L, L), L)
        r = r_ref[sl, :]
        k = k_ref[sl, :]
        v = v_ref[sl, :]
        lw = lw_ref[sl, :]
        a = a_ref[sl, :]
        if has_v:
            v = v + (vf_ref[sl, :] - v) * vg_ref[sl, :]
        kk = k * kk_ref[...]
        kk = kk / jnp.maximum(jnp.sqrt(_seg_sum(kk * kk, seg)), 1e-12)
        kmod = k * (1.0 + (a - 1.0) * ka_ref[...])
        c = _dot_exact_lhs(tri, lw)
        enc = jnp.exp(-c)
        ah = -kk * jnp.exp(c - lw)
        bh = kk * a * enc
        kh = kmod * enc
        rh = r * jnp.exp(c)
        p_last = jnp.exp(c[L - 1:L, :])
        ys = []
        for p in range(npair):
            ls = slice(P2 * p, P2 * (p + 1))
            a2, r2, b2, k2, v2 = (stack(z[:, ls]) for z in (ah, rh, bh, kh, v))
            x2 = jnp.concatenate([a2, r2], axis=0)
            y2 = jnp.concatenate([b2, k2], axis=0)
            g = _dot_nt(x2, y2)
            m_ab = jnp.where(strict, g[:P2, :P2], 0.0)
            m_ak = jnp.where(strict, g[:P2, P2:], 0.0)
            m_r = jnp.concatenate([jnp.where(incl, g[P2:, :P2], 0.0),
                                   jnp.where(incl, g[P2:, P2:], 0.0)], axis=1)
            dinv = eye + jnp.where(levels[0], m_ab, 0.0)
            for lv in levels[1:]:
                dinv = dinv + _dot(dinv, _dot(jnp.where(lv, m_ab, 0.0), dinv))
            s0 = s_ref[p]
            xs = _dot_nt(x2, s0)
            sa = _dot(dinv, xs[:P2] + _dot(m_ak, v2))
            sav = jnp.concatenate([sa.astype(BF16), v2], axis=0)
            yo = xs[P2:] + _dot(m_r, sav)
            ys.append(yo[:L] + yo[L:])
            s_ref[p] = (s0 + _dot_tn(sav, y2)) * p_last[:, ls]
        y = jnp.concatenate(ys, axis=1)
        mean = _seg_sum(y, seg) * (1.0 / N)
        yc = y - mean
        var = _seg_sum(yc * yc, seg) * (1.0 / N)
        yn = yc * lax.rsqrt(var + RWKV_GN_EPS) * lng_ref[...] + lnb_ref[...]
        bonus = _seg_sum(r * kmod * rk_ref[...], seg) * v
        o_ref[sl, :] = ((yn + bonus) * gate_ref[sl, :]).astype(o_ref.dtype)
        return carry

    lax.fori_loop(0, tc // L, chunk, 0)


def _rwkv_scan(rkv, lw, a, gate, v_first, vgate, k_k, k_a, r_k, lnx_g, lnx_b, bsz, seq, tc=256, wb=512):
    _, t, d = rkv.shape
    L = RWKV_CHUNK
    tc, wb = min(tc, seq), min(wb, d)
    assert seq % tc == 0 and tc % L == 0 and d % wb == 0 and wb % 256 == 0
    has_v = v_first is not None
    rkv4 = rkv.reshape(3, bsz, seq, d)
    b3 = lambda z: z.reshape(bsz, seq, d)
    blk = pl.BlockSpec((None, tc, wb), lambda b, h, c: (b, c, h))
    rkv_spec = lambda n: pl.BlockSpec((None, None, tc, wb), lambda b, h, c: (n, b, c, h))
    prow = pl.BlockSpec((1, wb), lambda b, h, c: (0, h))
    args = [rkv4, rkv4, rkv4, b3(lw), b3(a), b3(gate)]
    in_specs = [rkv_spec(0), rkv_spec(1), rkv_spec(2), blk, blk, blk]
    if has_v:
        args += [b3(v_first), b3(vgate)]
        in_specs += [blk, blk]
    args += [z.reshape(1, d) for z in (k_k, k_a, r_k, lnx_g, lnx_b)]
    in_specs += [prow] * 5
    out = pl.pallas_call(
        functools.partial(_rwkv_scan_kernel, L=L, tc=tc, has_v=has_v),
        grid=(bsz, d // wb, seq // tc), in_specs=in_specs, out_specs=blk,
        out_shape=jax.ShapeDtypeStruct((bsz, seq, d), BF16),
        scratch_shapes=[pltpu.VMEM((wb // (2 * L), 2 * L, 2 * L), F32)],
        compiler_params=_cparams("parallel", "parallel", "arbitrary"), name="rwkv_scan")(*args)
    return out.reshape(t, d)


def _rwkv_layer(x, norm_g, p, vres, v_first, bsz, seq):
    (mu, w_rkv, w0, w1, w2, a0, a1, a2, g1, g2, k_k, k_a, r_k, lnx_g, lnx_b, w_o) = p
    outs = _rwkv_prep(x, norm_g, mu, w0, w1, w2, a0, a1, a2, g1, g2, vres, seq)
    xs, lw, a, gate = outs[:4]
    vgate = outs[4] if vres is not None else None
    rkv = _bmm(xs, w_rkv.astype(BF16))
    o = _rwkv_scan(rkv, lw, a, gate, v_first if vres is not None else None, vgate,
                   k_k, k_a, r_k, lnx_g, lnx_b, bsz, seq)
    x = _mm(o, w_o.astype(BF16), res=x)
    return x, (rkv[2] if vres is None else v_first)


def _pick_tile(n, cap):
    best = LANES
    for tile in range(LANES, cap + 1, LANES):
        if n % tile == 0:
            best = tile
    return best


def _mlstm_kernel(q_ref, k_ref, v_ref, o_ref, gt_ref, bias_ref, ng_ref, out_ref, c_ref, m_ref, *, nh):
    lc, dk = q_ref.shape
    dv = v_ref.shape[1]
    h = pl.program_id(1)

    @pl.when(pl.program_id(2) == 0)
    def _():
        c_ref[...] = jnp.zeros_like(c_ref)
        m_ref[...] = jnp.zeros_like(m_ref)

    lane = lax.broadcasted_iota(jnp.int32, (lc, LANES), 1)
    z = gt_ref[...] + bias_ref[...]
    zc = MLSTM_GATE_CAP * jnp.tanh(z / MLSTM_GATE_CAP)
    lf_all = jnp.minimum(zc, 0.0) - jnp.log1p(jnp.exp(-jnp.abs(zc)))
    rr = lax.broadcasted_iota(jnp.int32, (lc, lc), 0)
    cc = lax.broadcasted_iota(jnp.int32, (lc, lc), 1)
    causal = rr >= cc
    bcum_all = _dot_exact_lhs(causal.astype(BF16), lf_all)
    comb = jnp.where(lane < nh, zc, bcum_all)
    li_col = jnp.sum(jnp.where(lane == h, comb, 0.0), axis=-1, keepdims=True)
    bc_col = jnp.sum(jnp.where(lane == h + nh, comb, 0.0), axis=-1, keepdims=True)
    er = lax.broadcasted_iota(jnp.int32, (8, LANES), 0)
    ec = lax.broadcasted_iota(jnp.int32, (8, LANES), 1)
    sel = jnp.where(er == 0, jnp.where(ec == h, 1.0, 0.0),
                    jnp.where(er == 1, jnp.where(ec == h + nh, 1.0, 0.0), 0.0)).astype(BF16)
    hi, mid, lo = _split3(comb)
    tr = lambda p: lax.dot_general(sel, p, (((1,), (1,)), ((), ())), preferred_element_type=F32)
    rows = tr(hi) + (tr(mid) + tr(lo))
    li_row, bc_row = rows[0:1], rows[1:2]

    m_st = m_ref[0:1, 0:1]
    dmat = jnp.where(causal, bc_col - bc_row + li_row, -jnp.inf)
    inter = bc_col + m_st
    m_t = jnp.maximum(inter, jnp.max(dmat, axis=-1, keepdims=True))
    q = q_ref[...] * (dk ** -0.5)
    k = k_ref[...]
    sc = _dot_nt(q, k) * jnp.exp(dmat - m_t)
    w_inter = jnp.exp(inter - m_t)
    ones_blk = (lane == 0).astype(BF16)
    v_ext = jnp.concatenate([v_ref[...].astype(BF16), ones_blk], axis=1)
    c_st = c_ref[...]
    nd = _dot(sc, v_ext) + w_inter * _dot(q, c_st)
    den = nd[:, dv:dv + 1]
    hc = nd[:, :dv] / jnp.maximum(jnp.abs(den), jnp.exp(-m_t))

    b_tot = bc_col[lc - 1:lc]
    log_wk = b_tot - bc_col + li_col
    m_new = jnp.maximum(b_tot + m_st, jnp.max(log_wk, axis=0, keepdims=True))
    c_ref[...] = jnp.exp(b_tot + m_st - m_new) * c_st + _dot_tn(k * jnp.exp(log_wk - m_new), v_ext)
    m_ref[...] = jnp.broadcast_to(m_new, m_ref.shape)

    hn = hc * lax.rsqrt(jnp.mean(hc * hc, axis=-1, keepdims=True) + RMS_EPS) * ng_ref[...]
    out_ref[...] = (hn * _sigmoid(o_ref[...])).astype(out_ref.dtype)


def _mlstm_layer(x, mix_g, w_in, b_if, norm_g, w_o, bsz, seq, lc=256):
    t, d = x.shape
    nh = MLSTM_HEADS
    dk, dv = d // 2 // nh, d // nh
    nq = 2 * nh * dk + 2 * nh * dv
    assert w_in.shape[1] == nq + 2 * nh and dk % LANES == 0
    lc = min(lc, seq)
    assert seq % lc == 0
    n = nq + LANES
    w = jnp.pad(w_in, ((0, 0), (0, n - w_in.shape[1]))).astype(BF16)
    proj = _norm_mm(x, mix_g, w, tn=_pick_tile(n, 1024)).reshape(bsz, seq, n)
    bias = jnp.zeros((1, LANES), F32).at[0, :nh].set(b_if[0]).at[0, nh:2 * nh].set(b_if[1])
    out = pl.pallas_call(
        functools.partial(_mlstm_kernel, nh=nh),
        grid=(bsz, nh, seq // lc),
        in_specs=[pl.BlockSpec((None, lc, dk), lambda b, h, c: (b, c, h)),
                  pl.BlockSpec((None, lc, dk), lambda b, h, c: (b, c, nh + h)),
                  pl.BlockSpec((None, lc, dv), lambda b, h, c: (b, c, nh + h)),
                  pl.BlockSpec((None, lc, dv), lambda b, h, c: (b, c, 2 * nh + h)),
                  pl.BlockSpec((None, lc, LANES), lambda b, h, c: (b, c, nq // LANES)),
                  pl.BlockSpec((1, LANES), lambda b, h, c: (0, 0)),
                  pl.BlockSpec((1, dv), lambda b, h, c: (0, h))],
        out_specs=pl.BlockSpec((None, lc, dv), lambda b, h, c: (b, c, h)),
        out_shape=jax.ShapeDtypeStruct((bsz, seq, nh * dv), BF16),
        scratch_shapes=[pltpu.VMEM((dk, dv + LANES), F32), pltpu.VMEM((8, LANES), F32)],
        compiler_params=_cparams("parallel", "parallel", "arbitrary"), name="mlstm")(
            proj, proj, proj, proj, proj, bias, norm_g.reshape(1, nh * dv))
    return _mm(out.reshape(t, nh * dv), w_o.astype(BF16), res=x)


DSA_TQ = 128
DSA_TK = 256
DSA_NEG = -1e30
INT32_MIN = -(2 ** 31)
KEY_NEG_INF = -2139095041
IDX_BIG = 2 ** 30


def _dsa_prep_kernel(p_ref, qg_ref, kg_ref, qn_ref, qi_ref, k_ref, v_ref, ki_ref, wi_ref, *, nh, nih):
    dh = DSA_HEAD_DIM
    for h in range(nh):
        qn_ref[h] = _rms(p_ref[:, h * dh:(h + 1) * dh], qg_ref[...]).astype(BF16)
    base = nh * dh
    k_ref[...] = _rms(p_ref[:, base:base + dh], kg_ref[...]).astype(BF16)
    v_ref[...] = p_ref[:, base + dh:base + 2 * dh].astype(BF16)
    base += 2 * dh
    for h in range(nih):
        qi_ref[h] = p_ref[:, base + h * LANES:base + (h + 1) * LANES].astype(BF16)
    base += nih * LANES
    ki_ref[...] = p_ref[:, base:base + LANES].astype(BF16)
    wi_ref[...] = p_ref[:, base + LANES:base + 2 * LANES]


def _dsa_kernel(fb_ref, qn_ref, qi_ref, wi_ref, k_ref, v_ref, ki_ref, nb_ref, o_ref,
                key_ref, s_ref, p_ref, acc_ref, m_ref, l_ref, *, nh, nih, n_sel, idx_bits):
    tq, tk, dh = DSA_TQ, DSA_TK, DSA_HEAD_DIM
    scale = dh ** -0.5
    q0 = pl.program_id(1) * tq
    jd = (q0 + tq - 1) // tk
    nt = jd + 1
    qs = qn_ref[...].reshape(nh * tq, dh)
    qis = qi_ref[...].reshape(nih * tq, LANES)
    wi = wi_ref[...]
    rowi = lax.broadcasted_iota(jnp.int32, (tq, tk), 0)
    coli = lax.broadcasted_iota(jnp.int32, (tq, tk), 1)
    ktile = lambda jt: pl.ds(pl.multiple_of(jt * tk, tk), tk)

    def score_tile(jt, c):
        lg = _dot_nt(qis, ki_ref[ktile(jt), :])
        score = jnp.zeros((tq, tk), F32)
        for h in range(nih):
            score = score + jnp.maximum(lg[h * tq:(h + 1) * tq], 0.0) * wi[:, h:h + 1]
        score = jnp.where(jt * tk + coli <= q0 + rowi, score, -jnp.inf)
        bits = pltpu.bitcast(score, jnp.int32)
        key_ref[jt] = bits ^ ((bits >> 31) & 0x7FFFFFFF)
        return c

    lax.fori_loop(0, nt, score_tile, 0)

    def count(hit):
        def body(jt, acc):
            keyt = key_ref[jt]
            for c0 in range(0, tk, LANES):
                acc = acc + hit(keyt[:, c0:c0 + LANES], jt * tk + c0)
            return acc
        acc = lax.fori_loop(0, nt, body, jnp.zeros((tq, LANES), F32))
        return jnp.sum(acc, axis=-1, keepdims=True)

    wide = lambda col: jnp.broadcast_to(col, (tq, LANES))
    lane = lax.broadcasted_iota(jnp.int32, (tq, LANES), 1)

    def value_bit(b, prefix):
        cand = prefix + jnp.left_shift(jnp.int32(1), 31 - b)
        cw = wide(cand)
        cnt = count(lambda kv, base: jnp.where(kv >= cw, 1.0, 0.0))
        return jnp.where(cnt >= n_sel, cand, prefix)

    thr = lax.fori_loop(0, 32, value_bit, jnp.full((tq, 1), INT32_MIN, jnp.int32))
    tw = wide(thr)
    c_gt = count(lambda kv, base: jnp.where(kv > tw, 1.0, 0.0))
    c_ge = count(lambda kv, base: jnp.where(kv >= tw, 1.0, 0.0))
    need = n_sel - c_gt
    c_eq = c_ge - c_gt

    def tie_search():
        def index_bit(b, jcur):
            cand = jcur + jnp.left_shift(jnp.int32(1), idx_bits - 1 - b)
            cw = wide(cand)
            f = count(lambda kv, base: jnp.where(kv == tw, jnp.where(base + lane < cw, 1.0, 0.0), 0.0))
            return jnp.where(f <= need, cand, jcur)
        return lax.fori_loop(0, idx_bits, index_bit, jnp.zeros((tq, 1), jnp.int32))

    excess = jnp.max(c_eq - need) > 0.0
    jc = lax.cond(excess, tie_search, lambda: jnp.full((tq, 1), IDX_BIG, jnp.int32))
    jc = jnp.where(c_eq > need, jc, IDX_BIG)
    jc = jnp.where(thr == KEY_NEG_INF, 0, jc)

    m_ref[...] = jnp.full_like(m_ref, DSA_NEG)
    l_ref[...] = jnp.zeros_like(l_ref)
    acc_ref[...] = jnp.zeros_like(acc_ref)

    def attend_tile(jt, near_idx):
        s_ref[...] = _dot_nt(qs, k_ref[ktile(jt), :])
        keyt = key_ref[jt]
        tie_mb = jnp.where(keyt == thr, jnp.where(jt * tk + coli < jc, 0.0, DSA_NEG), DSA_NEG)
        mb = jnp.where(keyt > thr, 0.0, tie_mb)

        def head(h, c):
            rows = pl.ds(pl.multiple_of(h * tq, tq), tq)
            if near_idx is None:
                s = s_ref[rows, :] * scale + (mb + fb_ref[h])
            else:
                s = s_ref[rows, :] * scale + nb_ref[near_idx, h] + mb
            m_old = m_ref[rows, :]
            m_new = jnp.maximum(m_old, jnp.max(s, axis=-1, keepdims=True))
            p = jnp.exp(s - m_new)
            alpha = jnp.exp(m_old - m_new)
            l_ref[rows, :] = alpha * l_ref[rows, :] + jnp.sum(p, axis=-1, keepdims=True)
            m_ref[rows, :] = m_new
            p_ref[rows, :] = p.astype(BF16)
            acc_ref[rows, :] = acc_ref[rows, :] * alpha
            return c

        lax.fori_loop(0, nh, head, 0)
        acc_ref[...] += jnp.dot(p_ref[...], v_ref[ktile(jt), :], preferred_element_type=F32)

    on_tile_edge = q0 == jd * tk
    has_prev_near = jnp.logical_and(on_tile_edge, jd >= 1)
    n_far = jnp.where(has_prev_near, jd - 1, jd)

    def far_tile(jt, c):
        attend_tile(jt, None)
        return c

    lax.fori_loop(0, n_far, far_tile, 0)

    @pl.when(has_prev_near)
    def _():
        attend_tile(jd - 1, 2)

    attend_tile(jd, jnp.where(on_tile_edge, 0, 1))

    for h in range(nh):
        rows = slice(h * tq, (h + 1) * tq)
        o_ref[:, h * dh:(h + 1) * dh] = (acc_ref[rows, :] / l_ref[rows, :]).astype(o_ref.dtype)


def _t5_bucket(rel):
    n = jnp.maximum(rel, 0)
    exact = T5_BUCKETS // 2
    nf = jnp.maximum(n, exact).astype(F32)
    large = exact + (jnp.log(nf / exact) / math.log(T5_MAX_DISTANCE / exact)
                     * (T5_BUCKETS - exact)).astype(jnp.int32)
    return jnp.where(n < exact, n, jnp.minimum(large, T5_BUCKETS - 1))


def _dsa_layer(x, mix_g, w_in, q_norm_g, k_norm_g, t5_table, w_o, bsz, seq):
    t, d = x.shape
    nh, dh, nih, di = DSA_HEADS, DSA_HEAD_DIM, IDX_HEADS, IDX_DIM
    tq, tk = DSA_TQ, DSA_TK
    assert seq % tk == 0 and tk == 2 * tq and dh == LANES and di <= LANES and nih <= LANES
    assert T5_MAX_DISTANCE <= tq
    n_sel = min(DSA_TOPK_MAX, seq // 4)
    o1, o2, o3, o4, o5 = nh * dh, nh * dh + dh, nh * dh + 2 * dh, nh * dh + 2 * dh + nih * di, nh * dh + 2 * dh + nih * di + di
    w_qi = jnp.pad(w_in[:, o3:o4].reshape(d, nih, di), ((0, 0), (0, 0), (0, LANES - di))).reshape(d, nih * LANES)
    w_ki = jnp.pad(w_in[:, o4:o5], ((0, 0), (0, LANES - di)))
    w_wi = jnp.pad(w_in[:, o5:o5 + nih] * (nih ** -0.5 * di ** -0.5), ((0, 0), (0, LANES - nih)))
    w = jnp.concatenate([w_in[:, :o3], w_qi, w_ki, w_wi], axis=1).astype(BF16)
    n = w.shape[1]
    proj = _norm_mm(x, mix_g, w, tn=_pick_tile(n, 1024)).reshape(bsz, seq, n)

    tm = min(256, seq)
    tok = lambda: pl.BlockSpec((None, tm, LANES), lambda b, i: (b, i, 0))
    qn, qi, kn, vb, kib, wis = pl.pallas_call(
        functools.partial(_dsa_prep_kernel, nh=nh, nih=nih),
        grid=(bsz, seq // tm),
        in_specs=[pl.BlockSpec((None, tm, n), lambda b, i: (b, i, 0)),
                  pl.BlockSpec((1, dh), lambda b, i: (0, 0)),
                  pl.BlockSpec((1, dh), lambda b, i: (0, 0))],
        out_specs=[pl.BlockSpec((None, nh, tm, dh), lambda b, i: (b, 0, i, 0)),
                   pl.BlockSpec((None, nih, tm, LANES), lambda b, i: (b, 0, i, 0)),
                   tok(), tok(), tok(), tok()],
        out_shape=[jax.ShapeDtypeStruct((bsz, nh, seq, dh), BF16),
                   jax.ShapeDtypeStruct((bsz, nih, seq, LANES), BF16),
                   jax.ShapeDtypeStruct((bsz, seq, dh), BF16),
                   jax.ShapeDtypeStruct((bsz, seq, dh), BF16),
                   jax.ShapeDtypeStruct((bsz, seq, LANES), BF16),
                   jax.ShapeDtypeStruct((bsz, seq, LANES), F32)],
        compiler_params=_cparams("parallel", "parallel"), name="dsa_prep")(
            proj, q_norm_g.reshape(1, dh), k_norm_g.reshape(1, dh))

    ii = jnp.arange(tq, dtype=jnp.int32)[:, None]
    jj = jnp.arange(tk, dtype=jnp.int32)[None, :]
    near = jnp.stack([t5_table[_t5_bucket(off + ii - jj)] for off in (0, tq, 2 * tq)])
    near = jnp.moveaxis(near, -1, 1).astype(F32)
    far = t5_table[T5_BUCKETS - 1].astype(F32)

    seqblk = lambda: pl.BlockSpec((None, seq, LANES), lambda b, i: (b, 0, 0))
    out = pl.pallas_call(
        functools.partial(_dsa_kernel, nh=nh, nih=nih, n_sel=n_sel, idx_bits=int(seq).bit_length()),
        grid=(bsz, seq // tq),
        in_specs=[pl.BlockSpec(memory_space=pltpu.SMEM),
                  pl.BlockSpec((None, nh, tq, dh), lambda b, i: (b, 0, i, 0)),
                  pl.BlockSpec((None, nih, tq, LANES), lambda b, i: (b, 0, i, 0)),
                  pl.BlockSpec((None, tq, LANES), lambda b, i: (b, i, 0)),
                  seqblk(), seqblk(), seqblk(),
                  pl.BlockSpec((3, nh, tq, tk), lambda b, i: (0, 0, 0, 0))],
        out_specs=pl.BlockSpec((None, tq, nh * dh), lambda b, i: (b, i, 0)),
        out_shape=jax.ShapeDtypeStruct((bsz, seq, nh * dh), BF16),
        scratch_shapes=[pltpu.VMEM((seq // tk, tq, tk), jnp.int32),
                        pltpu.VMEM((nh * tq, tk), F32),
                        pltpu.VMEM((nh * tq, tk), BF16),
                        pltpu.VMEM((nh * tq, dh), F32),
                        pltpu.VMEM((nh * tq, 1), F32),
                        pltpu.VMEM((nh * tq, 1), F32)],
        compiler_params=_cparams("parallel", "arbitrary"), name="dsa_attn")(
            far, qn, qi, wis, kn, vb, kib, near)
    return _mm(out.reshape(t, nh * dh), w_o.astype(BF16), res=x)


def kernel(x, rwkv_mu, rwkv_w_rkv, rwkv_w0, rwkv_w1, rwkv_w2, rwkv_a0, rwkv_a1, rwkv_a2, rwkv_v0, rwkv_v1,
           rwkv_v2, rwkv_g1, rwkv_g2, rwkv_k_k, rwkv_k_a, rwkv_r_k, rwkv_lnx_g, rwkv_lnx_b, rwkv_w_o,
           mlstm_w_in, mlstm_b_if, mlstm_norm_g, mlstm_w_o, dsa_w_in, dsa_q_norm_g, dsa_k_norm_g, dsa_w_o,
           t5_bias, mix_norm_g, ffn_norm_g, ffn_w_gate, ffn_w_up, ffn_w_down):
    bsz, seq, d = x.shape
    depth = mix_norm_g.shape[0]
    h = x.reshape(bsz * seq, d)
    v_first = None
    for i in range(depth):
        kind, j = i % 3, i // 3
        if kind == 0:
            vres = None if j == 0 else (rwkv_v0[j - 1], rwkv_v1[j - 1], rwkv_v2[j - 1])
            p = (rwkv_mu[j], rwkv_w_rkv[j], rwkv_w0[j], rwkv_w1[j], rwkv_w2[j], rwkv_a0[j], rwkv_a1[j],
                 rwkv_a2[j], rwkv_g1[j], rwkv_g2[j], rwkv_k_k[j], rwkv_k_a[j], rwkv_r_k[j],
                 rwkv_lnx_g[j], rwkv_lnx_b[j], rwkv_w_o[j])
            h, v_first = _rwkv_layer(h, mix_norm_g[i], p, vres, v_first, bsz, seq)
        elif kind == 1:
            h = _mlstm_layer(h, mix_norm_g[i], mlstm_w_in[j], mlstm_b_if[j], mlstm_norm_g[j],
                             mlstm_w_o[j], bsz, seq)
        else:
            h = _dsa_layer(h, mix_norm_g[i], dsa_w_in[j], dsa_q_norm_g[j], dsa_k_norm_g[j], t5_bias,
                           dsa_w_o[j], bsz, seq)
        h = _ffn(h, ffn_norm_g[i], ffn_w_gate[i].astype(BF16), ffn_w_up[i].astype(BF16),
                 ffn_w_down[i].astype(BF16))
    return h.reshape(bsz, seq, d)
```

```python
import functools
import math

import jax
import jax.numpy as jnp
from jax import lax
from jax.experimental import pallas as pl
from jax.experimental.pallas import tpu as pltpu

F32 = jnp.float32
BF16 = jnp.bfloat16

V7X_VMEM_LIMIT_BYTES = 56 * 1024 * 1024
LANES = 128

RMS_EPS = 1e-6
RWKV_HEAD = 64
RWKV_DECAY_SCALE = math.exp(-0.5)
RWKV_GN_EPS = 64e-5
RWKV_CHUNK = 64
RWKV_CHAINS = 8
MLSTM_HEADS = 4
MLSTM_GATE_CAP = 15.0
DSA_HEADS = 16
DSA_HEAD_DIM = 128
IDX_HEADS = 16
IDX_DIM = 64
DSA_TOPK_MAX = 256
T5_BUCKETS = 32
T5_MAX_DISTANCE = 128


def _cparams(*sem):
    return pltpu.CompilerParams(dimension_semantics=sem, vmem_limit_bytes=V7X_VMEM_LIMIT_BYTES)


def _dot(a, b):
    return jnp.dot(a.astype(BF16), b.astype(BF16), preferred_element_type=F32)


def _dot_nt(a, b):
    return lax.dot_general(a.astype(BF16), b.astype(BF16), (((1,), (1,)), ((), ())),
                           preferred_element_type=F32)


def _dot_tn(a, b):
    return lax.dot_general(a.astype(BF16), b.astype(BF16), (((0,), (0,)), ((), ())),
                           preferred_element_type=F32)


def _split3(x):
    hi = x.astype(BF16)
    r1 = x - hi.astype(F32)
    mid = r1.astype(BF16)
    lo = (r1 - mid.astype(F32)).astype(BF16)
    return hi, mid, lo


def _dot_exact_lhs(a_bf16, x):
    hi, mid, lo = _split3(x)
    d = lambda p: jnp.dot(a_bf16, p, preferred_element_type=F32)
    return d(hi) + (d(mid) + d(lo))


def _dot_exact_rhs(x, b_bf16):
    hi, mid, lo = _split3(x)
    d = lambda p: jnp.dot(p, b_bf16, preferred_element_type=F32)
    return d(hi) + (d(mid) + d(lo))


def _rms(x, g):
    ms = jnp.mean(x * x, axis=-1, keepdims=True)
    return x * lax.rsqrt(ms + RMS_EPS) * g


def _sigmoid(x):
    return 1.0 / (1.0 + jnp.exp(-x))


def _mm_kernel(a_ref, w_ref, o_ref):
    o_ref[...] = jnp.dot(a_ref[...], w_ref[...], preferred_element_type=F32).astype(o_ref.dtype)


def _mm_res_kernel(a_ref, w_ref, r_ref, o_ref):
    o_ref[...] = (r_ref[...] + jnp.dot(a_ref[...], w_ref[...], preferred_element_type=F32)).astype(o_ref.dtype)


def _mm(a, w, res=None, out_dtype=F32, tm=512, tn=512):
    m, k = a.shape
    n = w.shape[1]
    tm, tn = min(tm, m), min(tn, n)
    assert m % tm == 0 and n % tn == 0
    in_specs = [pl.BlockSpec((tm, k), lambda i, j: (i, 0)),
                pl.BlockSpec((k, tn), lambda i, j: (0, j))]
    args = [a, w]
    kern = _mm_kernel
    if res is not None:
        in_specs.append(pl.BlockSpec((tm, tn), lambda i, j: (i, j)))
        args.append(res)
        kern = _mm_res_kernel
    return pl.pallas_call(
        kern, grid=(m // tm, n // tn), in_specs=in_specs,
        out_specs=pl.BlockSpec((tm, tn), lambda i, j: (i, j)),
        out_shape=jax.ShapeDtypeStruct((m, n), out_dtype),
        compiler_params=_cparams("parallel", "arbitrary"), name="mm")(*args)


def _bmm(a, w, out_dtype=F32, tm=512, tn=512):
    g, m, k = a.shape
    n = w.shape[2]
    tm, tn = min(tm, m), min(tn, n)
    assert m % tm == 0 and n % tn == 0
    return pl.pallas_call(
        _mm_kernel, grid=(g, m // tm, n // tn),
        in_specs=[pl.BlockSpec((None, tm, k), lambda b, i, j: (b, i, 0)),
                  pl.BlockSpec((None, k, tn), lambda b, i, j: (b, 0, j))],
        out_specs=pl.BlockSpec((None, tm, tn), lambda b, i, j: (b, i, j)),
        out_shape=jax.ShapeDtypeStruct((g, m, n), out_dtype),
        compiler_params=_cparams("parallel", "parallel", "arbitrary"), name="bmm")(a, w)


def _norm_mm_kernel(x_ref, g_ref, w_ref, o_ref, h_ref):
    @pl.when(pl.program_id(1) == 0)
    def _():
        h_ref[...] = _rms(x_ref[...], g_ref[...]).astype(BF16)

    o_ref[...] = jnp.dot(h_ref[...], w_ref[...], preferred_element_type=F32).astype(o_ref.dtype)


def _norm_mm(x, g, w, out_dtype=F32, tm=512, tn=512):
    m, k = x.shape
    n = w.shape[1]
    tm, tn = min(tm, m), min(tn, n)
    assert m % tm == 0 and n % tn == 0
    return pl.pallas_call(
        _norm_mm_kernel, grid=(m // tm, n // tn),
        in_specs=[pl.BlockSpec((tm, k), lambda i, j: (i, 0)),
                  pl.BlockSpec((1, k), lambda i, j: (0, 0)),
                  pl.BlockSpec((k, tn), lambda i, j: (0, j))],
        out_specs=pl.BlockSpec((tm, tn), lambda i, j: (i, j)),
        out_shape=jax.ShapeDtypeStruct((m, n), out_dtype),
        scratch_shapes=[pltpu.VMEM((tm, k), BF16)],
        compiler_params=_cparams("parallel", "arbitrary"), name="norm_mm")(x, g.reshape(1, k), w)


def _ffn_kernel(x_ref, g_ref, wg_ref, wu_ref, wd_ref, o_ref, h_ref):
    @pl.when(pl.program_id(1) == 0)
    def _():
        x = x_ref[...]
        h_ref[...] = _rms(x, g_ref[...]).astype(BF16)
        o_ref[...] = x

    h = h_ref[...]
    gate = jnp.dot(h, wg_ref[...], preferred_element_type=F32)
    up = jnp.dot(h, wu_ref[...], preferred_element_type=F32)
    act = (gate * _sigmoid(gate) * up).astype(BF16)
    o_ref[...] += jnp.dot(act, wd_ref[...], preferred_element_type=F32)


def _ffn(x, g, wg, wu, wd, tm=512, tf=512):
    m, d = x.shape
    f = wg.shape[1]
    tm, tf = min(tm, m), min(tf, f)
    assert m % tm == 0 and f % tf == 0
    return pl.pallas_call(
        _ffn_kernel, grid=(m // tm, f // tf),
        in_specs=[pl.BlockSpec((tm, d), lambda i, j: (i, 0)),
                  pl.BlockSpec((1, d), lambda i, j: (0, 0)),
                  pl.BlockSpec((d, tf), lambda i, j: (0, j)),
                  pl.BlockSpec((d, tf), lambda i, j: (0, j)),
                  pl.BlockSpec((tf, d), lambda i, j: (j, 0))],
        out_specs=pl.BlockSpec((tm, d), lambda i, j: (i, 0)),
        out_shape=jax.ShapeDtypeStruct((m, d), F32),
        scratch_shapes=[pltpu.VMEM((tm, d), BF16)],
        compiler_params=_cparams("parallel", "arbitrary"), name="ffn")(x, g.reshape(1, d), wg, wu, wd)


def _rwkv_prep_kernel(*refs, seq, tm, has_v):
    if has_v:
        (x_ref, xp_ref, g_ref, mu_ref, w0_ref, w1_ref, w2_ref, a0_ref, a1_ref, a2_ref,
         g1_ref, g2_ref, v0_ref, v1_ref, v2_ref, xs_ref, lw_ref, a_ref, gate_ref, vg_ref) = refs
    else:
        (x_ref, xp_ref, g_ref, mu_ref, w0_ref, w1_ref, w2_ref, a0_ref, a1_ref, a2_ref,
         g1_ref, g2_ref, xs_ref, lw_ref, a_ref, gate_ref) = refs
    i = pl.program_id(0)
    gn = g_ref[...]
    h = _rms(x_ref[...], gn)
    hp = _rms(xp_ref[...], gn)
    seq_start = (i * tm) % seq == 0
    hp_row = jnp.where(seq_start, 0.0, hp[7:8, :])
    row = lax.broadcasted_iota(jnp.int32, (tm, 1), 0)
    h_prev = jnp.where(row == 0, hp_row, pltpu.roll(h, 1, 0))
    xx = h_prev - h
    mix = lambda n: h + xx * mu_ref[n:n + 1, :]
    xs_ref[0] = mix(0).astype(BF16)
    xs_ref[1] = mix(2).astype(BF16)
    xv = mix(3).astype(BF16)
    xs_ref[2] = xv
    lw_ref[...] = -RWKV_DECAY_SCALE * _sigmoid(
        w0_ref[...] + _dot(jnp.tanh(_dot(mix(1), w1_ref[...])), w2_ref[...]))
    a_ref[...] = _sigmoid(a0_ref[...] + _dot(_dot(mix(4), a1_ref[...]), a2_ref[...]))
    gate_ref[...] = _dot(_sigmoid(_dot(mix(5), g1_ref[...])), g2_ref[...])
    if has_v:
        vg_ref[...] = _sigmoid(v0_ref[...] + _dot(_dot(xv, v1_ref[...]), v2_ref[...]))


def _pad_lora(w_in, w_out):
    r = w_in.shape[1]
    rp = -(-r // LANES) * LANES
    return (jnp.pad(w_in, ((0, 0), (0, rp - r))).astype(BF16),
            jnp.pad(w_out, ((0, rp - r), (0, 0))).astype(BF16))


def _rwkv_prep(x, norm_g, mu, w0, w1, w2, a0, a1, a2, g1, g2, vres, seq, tm=256):
    t, d = x.shape
    tm = min(tm, seq)
    assert t % tm == 0 and seq % tm == 0 and tm % 8 == 0
    has_v = vres is not None
    row = lambda v: v.reshape(1, d)
    full = lambda a: pl.BlockSpec(a.shape, lambda i: (0,) * a.ndim)
    w1p, w2p = _pad_lora(w1, w2)
    a1p, a2p = _pad_lora(a1, a2)
    g1p, g2p = _pad_lora(g1, g2)
    mu8 = jnp.pad(mu, ((0, 2), (0, 0)))
    params = [row(norm_g), mu8, row(w0), w1p, w2p, row(a0), a1p, a2p, g1p, g2p]
    if has_v:
        v1p, v2p = _pad_lora(vres[1], vres[2])
        params += [row(vres[0]), v1p, v2p]
    tile = pl.BlockSpec((tm, d), lambda i: (i, 0))
    in_specs = [tile, pl.BlockSpec((8, d), lambda i: (jnp.maximum(i * (tm // 8) - 1, 0), 0))]
    in_specs += [full(p) for p in params]
    n_f32 = 4 if has_v else 3
    out_shape = [jax.ShapeDtypeStruct((3, t, d), BF16)] + [jax.ShapeDtypeStruct((t, d), F32)] * n_f32
    out_specs = [pl.BlockSpec((3, tm, d), lambda i: (0, i, 0))] + [tile] * n_f32
    return pl.pallas_call(
        functools.partial(_rwkv_prep_kernel, seq=seq, tm=tm, has_v=has_v),
        grid=(t // tm,), in_specs=in_specs, out_specs=out_specs, out_shape=out_shape,
        compiler_params=_cparams("parallel"), name="rwkv_prep")(x, x, *params)


def _seg_sum(x, seg):
    w = x.shape[1]
    return jnp.concatenate([_dot_exact_rhs(x[:, q:q + 256], seg) for q in range(0, w, 256)], axis=1)


def _rwkv_scan_kernel(*refs, L, tc, has_v):
    (s_ref, a2_ref, r2_ref, b2_ref, k2_ref, v2_ref, pl_ref, phi_ref, psi_ref, theta_ref, yloc_ref,
     y_ref, bonus_ref, mab_ref, tinv_ref, mak_ref, mrb_ref, mrk_ref, av_ref) = refs[-19:]
    refs = refs[:-19]
    if has_v:
        (r_ref, k_ref, v_ref, lw_ref, a_ref, gate_ref, vf_ref, vg_ref,
         kk_ref, ka_ref, rk_ref, lng_ref, lnb_ref, o_ref) = refs
    else:
        (r_ref, k_ref, v_ref, lw_ref, a_ref, gate_ref,
         kk_ref, ka_ref, rk_ref, lng_ref, lnb_ref, o_ref) = refs
    W = r_ref.shape[-1]
    P2 = 2 * L
    npair = W // P2
    N = RWKV_HEAD

    @pl.when(pl.program_id(2) == 0)
    def _():
        s_ref[...] = jnp.zeros_like(s_ref)

    ri = lax.broadcasted_iota(jnp.int32, (P2, P2), 0)
    ci = lax.broadcasted_iota(jnp.int32, (P2, P2), 1)
    strict = ri > ci
    incl = ri >= ci
    eye = (ri == ci).astype(F32)
    levels = []
    s = 1
    while s < L:
        levels.append(((ri // s) % 2 == 1) & ((ci // s) == (ri // s) - 1))
        s *= 2
    head0 = lax.broadcasted_iota(jnp.int32, (L, P2), 1) < N
    sr = lax.broadcasted_iota(jnp.int32, (256, 256), 0) // N
    sc = lax.broadcasted_iota(jnp.int32, (256, 256), 1) // N
    seg = (sr == sc).astype(BF16)

    nc = tc // L
    tr = lax.broadcasted_iota(jnp.int32, (tc, tc), 0)
    tcol = lax.broadcasted_iota(jnp.int32, (tc, tc), 1)
    tri = jnp.where(tr >= tcol, jnp.where(tr // L == tcol // L, 1.0, 0.0), 0.0).astype(BF16)
    del tr, tcol

    r = r_ref[...]
    k = k_ref[...]
    v = v_ref[...]
    lw = lw_ref[...]
    a = a_ref[...]
    if has_v:
        v = v + (vf_ref[...] - v) * vg_ref[...]
    kk = k * kk_ref[...]
    kk = kk / jnp.maximum(jnp.sqrt(_seg_sum(kk * kk, seg)), 1e-12)
    kmod = k * (1.0 + (a - 1.0) * ka_ref[...])
    c = _dot_exact_lhs(tri, lw)
    enc = jnp.exp(-c)
    bonus_ref[...] = _seg_sum(r * kmod * rk_ref[...], seg) * v
    operands = (-kk * jnp.exp(c - lw), r * jnp.exp(c), kk * a * enc, kmod * enc, v)
    for z, z_ref in zip(operands, (a2_ref, r2_ref, b2_ref, k2_ref, v2_ref)):
        for ch in range(nc):
            for p in range(npair):
                zz = z[ch * L:(ch + 1) * L, P2 * p:P2 * (p + 1)]
                z_ref[ch, p, :L] = jnp.where(head0, zz, 0.0).astype(BF16)
                z_ref[ch, p, L:] = jnp.where(head0, 0.0, zz).astype(BF16)
    for ch in range(nc):
        pl_ref[ch] = jnp.broadcast_to(jnp.exp(c[(ch + 1) * L - 1:(ch + 1) * L, :]), (8, W))
    del r, k, v, lw, a, kk, kmod, c, enc, operands

    chains = [(ch, p) for ch in range(nc) for p in range(npair)]
    for g0 in range(0, len(chains), RWKV_CHAINS):
        group = chains[g0:g0 + RWKV_CHAINS]
        for c in group:
            g = _dot_nt(jnp.concatenate([a2_ref[c], r2_ref[c]], axis=0),
                        jnp.concatenate([b2_ref[c], k2_ref[c]], axis=0))
            m_ab = jnp.where(strict, g[:P2, :P2], 0.0)
            mab_ref[c] = m_ab
            tinv_ref[c] = eye + jnp.where(levels[0], m_ab, 0.0)
            mak_ref[c] = jnp.where(strict, g[:P2, P2:], 0.0).astype(BF16)
            mrb_ref[c] = jnp.where(incl, g[P2:, :P2], 0.0).astype(BF16)
            mrk_ref[c] = jnp.where(incl, g[P2:, P2:], 0.0).astype(BF16)
        for lv in levels[1:]:
            steps = [_dot(jnp.where(lv, mab_ref[c], 0.0), tinv_ref[c]).astype(BF16) for c in group]
            for c, step in zip(group, steps):
                t_cur = tinv_ref[c]
                tinv_ref[c] = t_cur + _dot(t_cur, step)
        mvs = [_dot(mak_ref[c], v2_ref[c]).astype(BF16) for c in group]
        for c, mv in zip(group, mvs):
            av_ref[c] = _dot(tinv_ref[c], jnp.concatenate([a2_ref[c], mv], axis=1)).astype(BF16)
        ths = [_dot(mrb_ref[c], av_ref[c]) for c in group]
        yls = [_dot(mrk_ref[c], v2_ref[c]) for c in group]
        for c, th, yl in zip(group, ths, yls):
            theta_ref[c] = (r2_ref[c].astype(F32) + th[:, :P2]).astype(BF16)
            yloc_ref[c] = th[:, P2:] + yl
        for c in group:
            av = av_ref[c]
            pp = _dot_tn(jnp.concatenate([av[:, P2:], av[:, :P2]], axis=1), b2_ref[c])
            p_last = pl_ref[c[0]][0:1, P2 * c[1]:P2 * (c[1] + 1)]
            phi_ref[c] = ((eye + pp[P2:]) * p_last).astype(BF16)
            psi_ref[c] = (pp[:P2] + _dot_tn(v2_ref[c], k2_ref[c])) * p_last

    for ch in range(nc):
        for p in range(npair):
            s0 = s_ref[p]
            yo = _dot_nt(theta_ref[ch, p], s0) + yloc_ref[ch, p]
            y_ref[ch * L:(ch + 1) * L, P2 * p:P2 * (p + 1)] = yo[:L] + yo[L:]
            s_ref[p] = _dot(s0, phi_ref[ch, p]) + psi_ref[ch, p]

    y = y_ref[...]
    mean = _seg_sum(y, seg) * (1.0 / N)
    yc = y - mean
    var = _seg_sum(yc * yc, seg) * (1.0 / N)
    yn = yc * lax.rsqrt(var + RWKV_GN_EPS) * lng_ref[...] + lnb_ref[...]
    o_ref[...] = ((yn + bonus_ref[...]) * gate_ref[...]).astype(o_ref.dtype)


def _rwkv_scan(rkv, lw, a, gate, v_first, vgate, k_k, k_a, r_k, lnx_g, lnx_b, bsz, seq, tc=256, wb=512):
    _, t, d = rkv.shape
    L = RWKV_CHUNK
    tc, wb = min(tc, seq), min(wb, d)
    assert seq % tc == 0 and tc % L == 0 and d % wb == 0 and wb % 256 == 0
    has_v = v_first is not None
    nc, p2, npair = tc // L, 2 * L, wb // (2 * L)
    rkv4 = rkv.reshape(3, bsz, seq, d)
    b3 = lambda z: z.reshape(bsz, seq, d)
    blk = pl.BlockSpec((None, tc, wb), lambda b, h, c: (b, c, h))
    rkv_spec = lambda n: pl.BlockSpec((None, None, tc, wb), lambda b, h, c: (n, b, c, h))
    prow = pl.BlockSpec((1, wb), lambda b, h, c: (0, h))
    args = [rkv4, rkv4, rkv4, b3(lw), b3(a), b3(gate)]
    in_specs = [rkv_spec(0), rkv_spec(1), rkv_spec(2), blk, blk, blk]
    if has_v:
        args += [b3(v_first), b3(vgate)]
        in_specs += [blk, blk]
    args += [z.reshape(1, d) for z in (k_k, k_a, r_k, lnx_g, lnx_b)]
    in_specs += [prow] * 5
    out = pl.pallas_call(
        functools.partial(_rwkv_scan_kernel, L=L, tc=tc, has_v=has_v),
        grid=(bsz, d // wb, seq // tc), in_specs=in_specs, out_specs=blk,
        out_shape=jax.ShapeDtypeStruct((bsz, seq, d), BF16),
        scratch_shapes=[pltpu.VMEM((npair, p2, p2), F32)]
        + [pltpu.VMEM((nc, npair, p2, p2), BF16)] * 5
        + [pltpu.VMEM((nc, 8, wb), F32),
           pltpu.VMEM((nc, npair, p2, p2), BF16), pltpu.VMEM((nc, npair, p2, p2), F32),
           pltpu.VMEM((nc, npair, p2, p2), BF16), pltpu.VMEM((nc, npair, p2, p2), F32),
           pltpu.VMEM((tc, wb), F32), pltpu.VMEM((tc, wb), F32),
           pltpu.VMEM((nc, npair, p2, p2), F32), pltpu.VMEM((nc, npair, p2, p2), F32),
           pltpu.VMEM((nc, npair, p2, p2), BF16), pltpu.VMEM((nc, npair, p2, p2), BF16),
           pltpu.VMEM((nc, npair, p2, p2), BF16),
           pltpu.VMEM((nc, npair, p2, 2 * p2), BF16)],
        compiler_params=_cparams("parallel", "parallel", "arbitrary"), name="rwkv_scan")(*args)
    return out.reshape(t, d)


def _rwkv_layer(x, norm_g, p, vres, v_first, bsz, seq):
    (mu, w_rkv, w0, w1, w2, a0, a1, a2, g1, g2, k_k, k_a, r_k, lnx_g, lnx_b, w_o) = p
    outs = _rwkv_prep(x, norm_g, mu, w0, w1, w2, a0, a1, a2, g1, g2, vres, seq)
    xs, lw, a, gate = outs[:4]
    vgate = outs[4] if vres is not None else None
    rkv = _bmm(xs, w_rkv.astype(BF16))
    o = _rwkv_scan(rkv, lw, a, gate, v_first if vres is not None else None, vgate,
                   k_k, k_a, r_k, lnx_g, lnx_b, bsz, seq)
    x = _mm(o, w_o.astype(BF16), res=x)
    return x, (rkv[2] if vres is None else v_first)


def _pick_tile(n, cap):
    best = LANES
    for tile in range(LANES, cap + 1, LANES):
        if n % tile == 0:
            best = tile
    return best


def _mlstm_kernel(q_ref, k_ref, v_ref, o_ref, gt_ref, bias_ref, ng_ref, out_ref, c_ref, m_ref, *, nh):
    lc, dk = q_ref.shape
    dv = v_ref.shape[1]
    h = pl.program_id(1)

    @pl.when(pl.program_id(2) == 0)
    def _():
        c_ref[...] = jnp.zeros_like(c_ref)
        m_ref[...] = jnp.zeros_like(m_ref)

    lane = lax.broadcasted_iota(jnp.int32, (lc, LANES), 1)
    z = gt_ref[...] + bias_ref[...]
    zc = MLSTM_GATE_CAP * jnp.tanh(z / MLSTM_GATE_CAP)
    lf_all = jnp.minimum(zc, 0.0) - jnp.log1p(jnp.exp(-jnp.abs(zc)))
    rr = lax.broadcasted_iota(jnp.int32, (lc, lc), 0)
    cc = lax.broadcasted_iota(jnp.int32, (lc, lc), 1)
    causal = rr >= cc
    bcum_all = _dot_exact_lhs(causal.astype(BF16), lf_all)
    comb = jnp.where(lane < nh, zc, bcum_all)
    li_col = jnp.sum(jnp.where(lane == h, comb, 0.0), axis=-1, keepdims=True)
    bc_col = jnp.sum(jnp.where(lane == h + nh, comb, 0.0), axis=-1, keepdims=True)
    er = lax.broadcasted_iota(jnp.int32, (8, LANES), 0)
    ec = lax.broadcasted_iota(jnp.int32, (8, LANES), 1)
    sel = jnp.where(er == 0, jnp.where(ec == h, 1.0, 0.0),
                    jnp.where(er == 1, jnp.where(ec == h + nh, 1.0, 0.0), 0.0)).astype(BF16)
    hi, mid, lo = _split3(comb)
    tr = lambda p: lax.dot_general(sel, p, (((1,), (1,)), ((), ())), preferred_element_type=F32)
    rows = tr(hi) + (tr(mid) + tr(lo))
    li_row, bc_row = rows[0:1], rows[1:2]

    m_st = m_ref[0:1, 0:1]
    dmat = jnp.where(causal, bc_col - bc_row + li_row, -jnp.inf)
    inter = bc_col + m_st
    m_t = jnp.maximum(inter, jnp.max(dmat, axis=-1, keepdims=True))
    q = q_ref[...] * (dk ** -0.5)
    k = k_ref[...]
    sc = _dot_nt(q, k) * jnp.exp(dmat - m_t)
    w_inter = jnp.exp(inter - m_t)
    ones_blk = (lane == 0).astype(BF16)
    v_ext = jnp.concatenate([v_ref[...].astype(BF16), ones_blk], axis=1)
    c_st = c_ref[...]
    nd = _dot(sc, v_ext) + w_inter * _dot(q, c_st)
    den = nd[:, dv:dv + 1]
    hc = nd[:, :dv] / jnp.maximum(jnp.abs(den), jnp.exp(-m_t))

    b_tot = bc_col[lc - 1:lc]
    log_wk = b_tot - bc_col + li_col
    m_new = jnp.maximum(b_tot + m_st, jnp.max(log_wk, axis=0, keepdims=True))
    c_ref[...] = jnp.exp(b_tot + m_st - m_new) * c_st + _dot_tn(k * jnp.exp(log_wk - m_new), v_ext)
    m_ref[...] = jnp.broadcast_to(m_new, m_ref.shape)

    hn = hc * lax.rsqrt(jnp.mean(hc * hc, axis=-1, keepdims=True) + RMS_EPS) * ng_ref[...]
    out_ref[...] = (hn * _sigmoid(o_ref[...])).astype(out_ref.dtype)


def _mlstm_layer(x, mix_g, w_in, b_if, norm_g, w_o, bsz, seq, lc=256):
    t, d = x.shape
    nh = MLSTM_HEADS
    dk, dv = d // 2 // nh, d // nh
    nq = 2 * nh * dk + 2 * nh * dv
    assert w_in.shape[1] == nq + 2 * nh and dk % LANES == 0
    lc = min(lc, seq)
    assert seq % lc == 0
    n = nq + LANES
    w = jnp.pad(w_in, ((0, 0), (0, n - w_in.shape[1]))).astype(BF16)
    proj = _norm_mm(x, mix_g, w, tn=_pick_tile(n, 1024)).reshape(bsz, seq, n)
    bias = jnp.zeros((1, LANES), F32).at[0, :nh].set(b_if[0]).at[0, nh:2 * nh].set(b_if[1])
    out = pl.pallas_call(
        functools.partial(_mlstm_kernel, nh=nh),
        grid=(bsz, nh, seq // lc),
        in_specs=[pl.BlockSpec((None, lc, dk), lambda b, h, c: (b, c, h)),
                  pl.BlockSpec((None, lc, dk), lambda b, h, c: (b, c, nh + h)),
                  pl.BlockSpec((None, lc, dv), lambda b, h, c: (b, c, nh + h)),
                  pl.BlockSpec((None, lc, dv), lambda b, h, c: (b, c, 2 * nh + h)),
                  pl.BlockSpec((None, lc, LANES), lambda b, h, c: (b, c, nq // LANES)),
                  pl.BlockSpec((1, LANES), lambda b, h, c: (0, 0)),
                  pl.BlockSpec((1, dv), lambda b, h, c: (0, h))],
        out_specs=pl.BlockSpec((None, lc, dv), lambda b, h, c: (b, c, h)),
        out_shape=jax.ShapeDtypeStruct((bsz, seq, nh * dv), BF16),
        scratch_shapes=[pltpu.VMEM((dk, dv + LANES), F32), pltpu.VMEM((8, LANES), F32)],
        compiler_params=_cparams("parallel", "parallel", "arbitrary"), name="mlstm")(
            proj, proj, proj, proj, proj, bias, norm_g.reshape(1, nh * dv))
    return _mm(out.reshape(t, nh * dv), w_o.astype(BF16), res=x)


DSA_TQ = 128
DSA_TK = 256
DSA_NEG = -1e30
INT32_MIN = -(2 ** 31)
KEY_NEG_INF = -2139095041
IDX_BIG = 2 ** 30
LOG2E = math.log2(math.e)


def _dsa_prep_kernel(p_ref, qg_ref, kg_ref, qn_ref, qi_ref, k_ref, v_ref, ki_ref, wi_ref, *, nh, nih):
    dh = DSA_HEAD_DIM
    qscale = dh ** -0.5 * LOG2E
    for h in range(nh):
        qn_ref[h] = (_rms(p_ref[:, h * dh:(h + 1) * dh], qg_ref[...]) * qscale).astype(BF16)
    base = nh * dh
    k_ref[...] = _rms(p_ref[:, base:base + dh], kg_ref[...]).astype(BF16)
    v_ref[:, :dh] = p_ref[:, base + dh:base + 2 * dh].astype(BF16)
    v_ref[:, dh:] = jnp.ones((v_ref.shape[0], LANES), BF16)
    base += 2 * dh
    for h in range(nih):
        qi_ref[h] = p_ref[:, base + h * LANES:base + (h + 1) * LANES].astype(BF16)
    base += nih * LANES
    ki_ref[...] = p_ref[:, base:base + LANES].astype(BF16)
    wi_ref[...] = p_ref[:, base + LANES:base + 2 * LANES]


def _dsa_kernel(fb_ref, qn_ref, qi_ref, wi_ref, k_ref, v_ref, ki_ref, nb_ref, o_ref,
                key_ref, s_ref, p_ref, acc_ref, m_ref, *, nh, nih, n_sel, idx_bits):
    tq, tk, dh = DSA_TQ, DSA_TK, DSA_HEAD_DIM
    q0 = pl.program_id(1) * tq
    jd = (q0 + tq - 1) // tk
    nt = jd + 1
    qs = qn_ref[...].reshape(nh * tq, dh)
    qis = qi_ref[...].reshape(nih * tq, LANES)
    wi = wi_ref[...]
    rowi = lax.broadcasted_iota(jnp.int32, (tq, tk), 0)
    coli = lax.broadcasted_iota(jnp.int32, (tq, tk), 1)
    ktile = lambda jt: pl.ds(pl.multiple_of(jt * tk, tk), tk)

    def score_tile(jt, c):
        lg = _dot_nt(qis, ki_ref[ktile(jt), :])
        score = jnp.zeros((tq, tk), F32)
        for h in range(nih):
            score = score + jnp.maximum(lg[h * tq:(h + 1) * tq], 0.0) * wi[:, h:h + 1]
        score = jnp.where(jt * tk + coli <= q0 + rowi, score, -jnp.inf)
        bits = pltpu.bitcast(score, jnp.int32)
        key_ref[jt] = bits ^ ((bits >> 31) & 0x7FFFFFFF)
        return c

    lax.fori_loop(0, nt, score_tile, 0)

    def count(hit):
        def body(jt, acc):
            keyt = key_ref[jt]
            for c0 in range(0, tk, LANES):
                acc = acc + hit(keyt[:, c0:c0 + LANES], jt * tk + c0)
            return acc
        acc = lax.fori_loop(0, nt, body, jnp.zeros((tq, LANES), F32))
        return jnp.sum(acc, axis=-1, keepdims=True)

    wide = lambda col: jnp.broadcast_to(col, (tq, LANES))
    lane = lax.broadcasted_iota(jnp.int32, (tq, LANES), 1)

    def value_bit(b, prefix):
        cand = prefix + jnp.left_shift(jnp.int32(1), 31 - b)
        cw = wide(cand)
        cnt = count(lambda kv, base: jnp.where(kv >= cw, 1.0, 0.0))
        return jnp.where(cnt >= n_sel, cand, prefix)

    thr = lax.fori_loop(0, 32, value_bit, jnp.full((tq, 1), INT32_MIN, jnp.int32))
    tw = wide(thr)
    c_gt = count(lambda kv, base: jnp.where(kv > tw, 1.0, 0.0))
    c_ge = count(lambda kv, base: jnp.where(kv >= tw, 1.0, 0.0))
    need = n_sel - c_gt
    c_eq = c_ge - c_gt

    def tie_search():
        def index_bit(b, jcur):
            cand = jcur + jnp.left_shift(jnp.int32(1), idx_bits - 1 - b)
            cw = wide(cand)
            f = count(lambda kv, base: jnp.where(kv == tw, jnp.where(base + lane < cw, 1.0, 0.0), 0.0))
            return jnp.where(f <= need, cand, jcur)
        return lax.fori_loop(0, idx_bits, index_bit, jnp.zeros((tq, 1), jnp.int32))

    excess = jnp.max(c_eq - need) > 0.0
    jc = lax.cond(excess, tie_search, lambda: jnp.full((tq, 1), IDX_BIG, jnp.int32))
    jc = jnp.where(c_eq > need, jc, IDX_BIG)
    jc = jnp.where(thr == KEY_NEG_INF, 0, jc)

    m_ref[...] = jnp.full_like(m_ref, DSA_NEG)
    acc_ref[...] = jnp.zeros_like(acc_ref)
    twice = lambda z: jnp.concatenate([z] * (tk // LANES), axis=1)

    def attend_tile(jt, near_idx):
        s_ref[...] = _dot_nt(qs, k_ref[ktile(jt), :])
        keyt = key_ref[jt]
        tie_mb = jnp.where(keyt == thr, jnp.where(jt * tk + coli < jc, 0.0, DSA_NEG), DSA_NEG)
        mb = jnp.where(keyt > thr, 0.0, tie_mb)
        for h in range(nh):
            rows = slice(h * tq, (h + 1) * tq)
            if near_idx is None:
                s = s_ref[rows, :] + (mb + fb_ref[h])
            else:
                s = s_ref[rows, :] + (nb_ref[near_idx, h] + mb)
            m_old = m_ref[rows, :]
            m_new = jnp.maximum(m_old, jnp.max(s, axis=-1, keepdims=True))
            m_ref[rows, :] = m_new
            p_ref[rows, :] = jnp.exp2(s - twice(m_new)).astype(BF16)
            acc_ref[rows, :] = acc_ref[rows, :] * twice(jnp.exp2(m_old - m_new))
        acc_ref[...] += jnp.dot(p_ref[...], v_ref[ktile(jt), :], preferred_element_type=F32)

    on_tile_edge = q0 == jd * tk
    has_prev_near = jnp.logical_and(on_tile_edge, jd >= 1)
    n_far = jnp.where(has_prev_near, jd - 1, jd)

    def far_tile(jt, c):
        attend_tile(jt, None)
        return c

    lax.fori_loop(0, n_far, far_tile, 0)

    @pl.when(has_prev_near)
    def _():
        attend_tile(jd - 1, 2)

    attend_tile(jd, jnp.where(on_tile_edge, 0, 1))

    for h in range(nh):
        rows = slice(h * tq, (h + 1) * tq)
        o_ref[:, h * dh:(h + 1) * dh] = (acc_ref[rows, :dh] / acc_ref[rows, dh:]).astype(o_ref.dtype)


def _t5_bucket(rel):
    n = jnp.maximum(rel, 0)
    exact = T5_BUCKETS // 2
    nf = jnp.maximum(n, exact).astype(F32)
    large = exact + (jnp.log(nf / exact) / math.log(T5_MAX_DISTANCE / exact)
                     * (T5_BUCKETS - exact)).astype(jnp.int32)
    return jnp.where(n < exact, n, jnp.minimum(large, T5_BUCKETS - 1))


def _dsa_layer(x, mix_g, w_in, q_norm_g, k_norm_g, t5_table, w_o, bsz, seq):
    t, d = x.shape
    nh, dh, nih, di = DSA_HEADS, DSA_HEAD_DIM, IDX_HEADS, IDX_DIM
    tq, tk = DSA_TQ, DSA_TK
    assert seq % tk == 0 and tk == 2 * tq and dh == LANES and di <= LANES and nih <= LANES
    assert T5_MAX_DISTANCE <= tq
    n_sel = min(DSA_TOPK_MAX, seq // 4)
    o1, o2, o3, o4, o5 = nh * dh, nh * dh + dh, nh * dh + 2 * dh, nh * dh + 2 * dh + nih * di, nh * dh + 2 * dh + nih * di + di
    w_qi = jnp.pad(w_in[:, o3:o4].reshape(d, nih, di), ((0, 0), (0, 0), (0, LANES - di))).reshape(d, nih * LANES)
    w_ki = jnp.pad(w_in[:, o4:o5], ((0, 0), (0, LANES - di)))
    w_wi = jnp.pad(w_in[:, o5:o5 + nih] * (nih ** -0.5 * di ** -0.5), ((0, 0), (0, LANES - nih)))
    w = jnp.concatenate([w_in[:, :o3], w_qi, w_ki, w_wi], axis=1).astype(BF16)
    n = w.shape[1]
    proj = _norm_mm(x, mix_g, w, tn=_pick_tile(n, 1024)).reshape(bsz, seq, n)

    tm = min(256, seq)
    tok = lambda: pl.BlockSpec((None, tm, LANES), lambda b, i: (b, i, 0))
    qn, qi, kn, vb, kib, wis = pl.pallas_call(
        functools.partial(_dsa_prep_kernel, nh=nh, nih=nih),
        grid=(bsz, seq // tm),
        in_specs=[pl.BlockSpec((None, tm, n), lambda b, i: (b, i, 0)),
                  pl.BlockSpec((1, dh), lambda b, i: (0, 0)),
                  pl.BlockSpec((1, dh), lambda b, i: (0, 0))],
        out_specs=[pl.BlockSpec((None, nh, tm, dh), lambda b, i: (b, 0, i, 0)),
                   pl.BlockSpec((None, nih, tm, LANES), lambda b, i: (b, 0, i, 0)),
                   tok(), pl.BlockSpec((None, tm, dh + LANES), lambda b, i: (b, i, 0)), tok(), tok()],
        out_shape=[jax.ShapeDtypeStruct((bsz, nh, seq, dh), BF16),
                   jax.ShapeDtypeStruct((bsz, nih, seq, LANES), BF16),
                   jax.ShapeDtypeStruct((bsz, seq, dh), BF16),
                   jax.ShapeDtypeStruct((bsz, seq, dh + LANES), BF16),
                   jax.ShapeDtypeStruct((bsz, seq, LANES), BF16),
                   jax.ShapeDtypeStruct((bsz, seq, LANES), F32)],
        compiler_params=_cparams("parallel", "parallel"), name="dsa_prep")(
            proj, q_norm_g.reshape(1, dh), k_norm_g.reshape(1, dh))

    ii = jnp.arange(tq, dtype=jnp.int32)[:, None]
    jj = jnp.arange(tk, dtype=jnp.int32)[None, :]
    near = jnp.stack([t5_table[_t5_bucket(off + ii - jj)] for off in (0, tq, 2 * tq)])
    near = jnp.moveaxis(near, -1, 1).astype(F32) * LOG2E
    far = t5_table[T5_BUCKETS - 1].astype(F32) * LOG2E

    seqblk = lambda: pl.BlockSpec((None, seq, LANES), lambda b, i: (b, 0, 0))
    out = pl.pallas_call(
        functools.partial(_dsa_kernel, nh=nh, nih=nih, n_sel=n_sel, idx_bits=int(seq).bit_length()),
        grid=(bsz, seq // tq),
        in_specs=[pl.BlockSpec(memory_space=pltpu.SMEM),
                  pl.BlockSpec((None, nh, tq, dh), lambda b, i: (b, 0, i, 0)),
                  pl.BlockSpec((None, nih, tq, LANES), lambda b, i: (b, 0, i, 0)),
                  pl.BlockSpec((None, tq, LANES), lambda b, i: (b, i, 0)),
                  seqblk(), pl.BlockSpec((None, seq, dh + LANES), lambda b, i: (b, 0, 0)), seqblk(),
                  pl.BlockSpec((3, nh, tq, tk), lambda b, i: (0, 0, 0, 0))],
        out_specs=pl.BlockSpec((None, tq, nh * dh), lambda b, i: (b, i, 0)),
        out_shape=jax.ShapeDtypeStruct((bsz, seq, nh * dh), BF16),
        scratch_shapes=[pltpu.VMEM((seq // tk, tq, tk), jnp.int32),
                        pltpu.VMEM((nh * tq, tk), F32),
                        pltpu.VMEM((nh * tq, tk), BF16),
                        pltpu.VMEM((nh * tq, dh + LANES), F32),
                        pltpu.VMEM((nh * tq, LANES), F32)],
        compiler_params=_cparams("parallel", "arbitrary"), name="dsa_attn")(
            far, qn, qi, wis, kn, vb, kib, near)
    return _mm(out.reshape(t, nh * dh), w_o.astype(BF16), res=x)


def kernel(x, rwkv_mu, rwkv_w_rkv, rwkv_w0, rwkv_w1, rwkv_w2, rwkv_a0, rwkv_a1, rwkv_a2, rwkv_v0, rwkv_v1,
           rwkv_v2, rwkv_g1, rwkv_g2, rwkv_k_k, rwkv_k_a, rwkv_r_k, rwkv_lnx_g, rwkv_lnx_b, rwkv_w_o,
           mlstm_w_in, mlstm_b_if, mlstm_norm_g, mlstm_w_o, dsa_w_in, dsa_q_norm_g, dsa_k_norm_g, dsa_w_o,
           t5_bias, mix_norm_g, ffn_norm_g, ffn_w_gate, ffn_w_up, ffn_w_down):
    bsz, seq, d = x.shape
    depth = mix_norm_g.shape[0]
    h = x.reshape(bsz * seq, d)
    v_first = None
    for i in range(depth):
        kind, j = i % 3, i // 3
        if kind == 0:
            vres = None if j == 0 else (rwkv_v0[j - 1], rwkv_v1[j - 1], rwkv_v2[j - 1])
            p = (rwkv_mu[j], rwkv_w_rkv[j], rwkv_w0[j], rwkv_w1[j], rwkv_w2[j], rwkv_a0[j], rwkv_a1[j],
                 rwkv_a2[j], rwkv_g1[j], rwkv_g2[j], rwkv_k_k[j], rwkv_k_a[j], rwkv_r_k[j],
                 rwkv_lnx_g[j], rwkv_lnx_b[j], rwkv_w_o[j])
            h, v_first = _rwkv_layer(h, mix_norm_g[i], p, vres, v_first, bsz, seq)
        elif kind == 1:
            h = _mlstm_layer(h, mix_norm_g[i], mlstm_w_in[j], mlstm_b_if[j], mlstm_norm_g[j],
                             mlstm_w_o[j], bsz, seq)
        else:
            h = _dsa_layer(h, mix_norm_g[i], dsa_w_in[j], dsa_q_norm_g[j], dsa_k_norm_g[j], t5_bias,
                           dsa_w_o[j], bsz, seq)
        h = _ffn(h, ffn_norm_g[i], ffn_w_gate[i].astype(BF16), ffn_w_up[i].astype(BF16),
                 ffn_w_down[i].astype(BF16))
    return h.reshape(bsz, seq, d)
```

```python
import functools
import math

import jax
import jax.numpy as jnp
from jax import lax
from jax.experimental import pallas as pl
from jax.experimental.pallas import tpu as pltpu

F32 = jnp.float32
BF16 = jnp.bfloat16

V7X_VMEM_LIMIT_BYTES = 56 * 1024 * 1024
LANES = 128

RMS_EPS = 1e-6
RWKV_HEAD = 64
RWKV_DECAY_SCALE = math.exp(-0.5)
RWKV_GN_EPS = 64e-5
RWKV_CHUNK = 64
RWKV_CHAINS = 8
MLSTM_HEADS = 4
MLSTM_GATE_CAP = 15.0
DSA_HEADS = 16
DSA_HEAD_DIM = 128
IDX_HEADS = 16
IDX_DIM = 64
DSA_TOPK_MAX = 256
T5_BUCKETS = 32
T5_MAX_DISTANCE = 128


def _cparams(*sem):
    return pltpu.CompilerParams(dimension_semantics=sem, vmem_limit_bytes=V7X_VMEM_LIMIT_BYTES)


def _dot(a, b):
    return jnp.dot(a.astype(BF16), b.astype(BF16), preferred_element_type=F32)


def _dot_nt(a, b):
    return lax.dot_general(a.astype(BF16), b.astype(BF16), (((1,), (1,)), ((), ())),
                           preferred_element_type=F32)


def _dot_tn(a, b):
    return lax.dot_general(a.astype(BF16), b.astype(BF16), (((0,), (0,)), ((), ())),
                           preferred_element_type=F32)


def _split3(x):
    hi = x.astype(BF16)
    r1 = x - hi.astype(F32)
    mid = r1.astype(BF16)
    lo = (r1 - mid.astype(F32)).astype(BF16)
    return hi, mid, lo


def _dot_exact_lhs(a_bf16, x):
    hi, mid, lo = _split3(x)
    d = lambda p: jnp.dot(a_bf16, p, preferred_element_type=F32)
    return d(hi) + (d(mid) + d(lo))


def _rms(x, g):
    ms = jnp.mean(x * x, axis=-1, keepdims=True)
    return x * lax.rsqrt(ms + RMS_EPS) * g


def _sigmoid(x):
    return 1.0 / (1.0 + jnp.exp(-x))


def _mm_kernel(a_ref, w_ref, o_ref):
    o_ref[...] = jnp.dot(a_ref[...], w_ref[...], preferred_element_type=F32).astype(o_ref.dtype)


def _mm_res_kernel(a_ref, w_ref, r_ref, o_ref):
    o_ref[...] = (r_ref[...] + jnp.dot(a_ref[...], w_ref[...], preferred_element_type=F32)).astype(o_ref.dtype)


def _mm(a, w, res=None, out_dtype=F32, tm=1024, tn=512):
    m, k = a.shape
    n = w.shape[1]
    tm, tn = min(tm, m), min(tn, n)
    assert m % tm == 0 and n % tn == 0
    in_specs = [pl.BlockSpec((tm, k), lambda i, j: (i, 0)),
                pl.BlockSpec((k, tn), lambda i, j: (0, j))]
    args = [a, w]
    kern = _mm_kernel
    if res is not None:
        in_specs.append(pl.BlockSpec((tm, tn), lambda i, j: (i, j)))
        args.append(res)
        kern = _mm_res_kernel
    return pl.pallas_call(
        kern, grid=(m // tm, n // tn), in_specs=in_specs,
        out_specs=pl.BlockSpec((tm, tn), lambda i, j: (i, j)),
        out_shape=jax.ShapeDtypeStruct((m, n), out_dtype),
        compiler_params=_cparams("parallel", "arbitrary"), name="mm")(*args)


def _bmm(a, w, out_dtype=F32, tm=1024, tn=512):
    g, m, k = a.shape
    n = w.shape[2]
    tm, tn = min(tm, m), min(tn, n)
    assert m % tm == 0 and n % tn == 0
    return pl.pallas_call(
        _mm_kernel, grid=(g, m // tm, n // tn),
        in_specs=[pl.BlockSpec((None, tm, k), lambda b, i, j: (b, i, 0)),
                  pl.BlockSpec((None, k, tn), lambda b, i, j: (b, 0, j))],
        out_specs=pl.BlockSpec((None, tm, tn), lambda b, i, j: (b, i, j)),
        out_shape=jax.ShapeDtypeStruct((g, m, n), out_dtype),
        compiler_params=_cparams("parallel", "parallel", "arbitrary"), name="bmm")(a, w)


def _norm_mm_kernel(x_ref, g_ref, w_ref, o_ref, h_ref):
    @pl.when(pl.program_id(1) == 0)
    def _():
        h_ref[...] = _rms(x_ref[...], g_ref[...]).astype(BF16)

    o_ref[...] = jnp.dot(h_ref[...], w_ref[...], preferred_element_type=F32).astype(o_ref.dtype)


def _norm_mm(x, g, w, out_dtype=F32, tm=1024, tn=512):
    m, k = x.shape
    n = w.shape[1]
    tm, tn = min(tm, m), min(tn, n)
    assert m % tm == 0 and n % tn == 0
    return pl.pallas_call(
        _norm_mm_kernel, grid=(m // tm, n // tn),
        in_specs=[pl.BlockSpec((tm, k), lambda i, j: (i, 0)),
                  pl.BlockSpec((1, k), lambda i, j: (0, 0)),
                  pl.BlockSpec((k, tn), lambda i, j: (0, j))],
        out_specs=pl.BlockSpec((tm, tn), lambda i, j: (i, j)),
        out_shape=jax.ShapeDtypeStruct((m, n), out_dtype),
        scratch_shapes=[pltpu.VMEM((tm, k), BF16)],
        compiler_params=_cparams("parallel", "arbitrary"), name="norm_mm")(x, g.reshape(1, k), w)


def _ffn_kernel(x_ref, g_ref, wg_ref, wu_ref, wd_ref, o_ref, h_ref):
    @pl.when(pl.program_id(1) == 0)
    def _():
        x = x_ref[...]
        h_ref[...] = _rms(x, g_ref[...]).astype(BF16)
        o_ref[...] = x

    h = h_ref[...]
    gate = jnp.dot(h, wg_ref[...], preferred_element_type=F32)
    up = jnp.dot(h, wu_ref[...], preferred_element_type=F32)
    act = (gate * _sigmoid(gate) * up).astype(BF16)
    o_ref[...] += jnp.dot(act, wd_ref[...], preferred_element_type=F32)


def _ffn(x, g, wg, wu, wd, tm=512, tf=512):
    m, d = x.shape
    f = wg.shape[1]
    tm, tf = min(tm, m), min(tf, f)
    assert m % tm == 0 and f % tf == 0
    return pl.pallas_call(
        _ffn_kernel, grid=(m // tm, f // tf),
        in_specs=[pl.BlockSpec((tm, d), lambda i, j: (i, 0)),
                  pl.BlockSpec((1, d), lambda i, j: (0, 0)),
                  pl.BlockSpec((d, tf), lambda i, j: (0, j)),
                  pl.BlockSpec((d, tf), lambda i, j: (0, j)),
                  pl.BlockSpec((tf, d), lambda i, j: (j, 0))],
        out_specs=pl.BlockSpec((tm, d), lambda i, j: (i, 0)),
        out_shape=jax.ShapeDtypeStruct((m, d), F32),
        scratch_shapes=[pltpu.VMEM((tm, d), BF16)],
        compiler_params=_cparams("parallel", "arbitrary"), name="ffn")(x, g.reshape(1, d), wg, wu, wd)


def _rwkv_prep_kernel(*refs, seq, tm, has_v):
    if has_v:
        (x_ref, xp_ref, g_ref, mu_ref, w0_ref, w1_ref, w2_ref, a0_ref, a1_ref, a2_ref,
         g1_ref, g2_ref, v0_ref, v1_ref, v2_ref, xs_ref, lw_ref, a_ref, gate_ref, vg_ref) = refs
    else:
        (x_ref, xp_ref, g_ref, mu_ref, w0_ref, w1_ref, w2_ref, a0_ref, a1_ref, a2_ref,
         g1_ref, g2_ref, xs_ref, lw_ref, a_ref, gate_ref) = refs
    i = pl.program_id(0)
    gn = g_ref[...]
    h = _rms(x_ref[...], gn)
    hp = _rms(xp_ref[...], gn)
    seq_start = (i * tm) % seq == 0
    hp_row = jnp.where(seq_start, 0.0, hp[7:8, :])
    row = lax.broadcasted_iota(jnp.int32, (tm, 1), 0)
    h_prev = jnp.where(row == 0, hp_row, pltpu.roll(h, 1, 0))
    xx = h_prev - h
    mix = lambda n: h + xx * mu_ref[n:n + 1, :]
    xs_ref[0] = mix(0).astype(BF16)
    xs_ref[1] = mix(2).astype(BF16)
    xv = mix(3).astype(BF16)
    xs_ref[2] = xv
    lw_ref[...] = -RWKV_DECAY_SCALE * _sigmoid(
        w0_ref[...] + _dot(jnp.tanh(_dot(mix(1), w1_ref[...])), w2_ref[...]))
    a_ref[...] = _sigmoid(a0_ref[...] + _dot(_dot(mix(4), a1_ref[...]), a2_ref[...]))
    gate_ref[...] = _dot(_sigmoid(_dot(mix(5), g1_ref[...])), g2_ref[...])
    if has_v:
        vg_ref[...] = _sigmoid(v0_ref[...] + _dot(_dot(xv, v1_ref[...]), v2_ref[...]))


def _pad_lora(w_in, w_out):
    r = w_in.shape[1]
    rp = -(-r // LANES) * LANES
    return (jnp.pad(w_in, ((0, 0), (0, rp - r))).astype(BF16),
            jnp.pad(w_out, ((0, rp - r), (0, 0))).astype(BF16))


def _rwkv_prep(x, norm_g, mu, w0, w1, w2, a0, a1, a2, g1, g2, vres, seq, tm=256):
    t, d = x.shape
    tm = min(tm, seq)
    assert t % tm == 0 and seq % tm == 0 and tm % 8 == 0
    has_v = vres is not None
    row = lambda v: v.reshape(1, d)
    full = lambda a: pl.BlockSpec(a.shape, lambda i: (0,) * a.ndim)
    w1p, w2p = _pad_lora(w1, w2)
    a1p, a2p = _pad_lora(a1, a2)
    g1p, g2p = _pad_lora(g1, g2)
    mu8 = jnp.pad(mu, ((0, 2), (0, 0)))
    params = [row(norm_g), mu8, row(w0), w1p, w2p, row(a0), a1p, a2p, g1p, g2p]
    if has_v:
        v1p, v2p = _pad_lora(vres[1], vres[2])
        params += [row(vres[0]), v1p, v2p]
    tile = pl.BlockSpec((tm, d), lambda i: (i, 0))
    in_specs = [tile, pl.BlockSpec((8, d), lambda i: (jnp.maximum(i * (tm // 8) - 1, 0), 0))]
    in_specs += [full(p) for p in params]
    n_f32 = 4 if has_v else 3
    out_shape = [jax.ShapeDtypeStruct((3, t, d), BF16)] + [jax.ShapeDtypeStruct((t, d), F32)] * n_f32
    out_specs = [pl.BlockSpec((3, tm, d), lambda i: (0, i, 0))] + [tile] * n_f32
    return pl.pallas_call(
        functools.partial(_rwkv_prep_kernel, seq=seq, tm=tm, has_v=has_v),
        grid=(t // tm,), in_specs=in_specs, out_specs=out_specs, out_shape=out_shape,
        compiler_params=_cparams("parallel"), name="rwkv_prep")(x, x, *params)


def _seg_sum(x, seg):
    w = x.shape[1]
    hi = x.astype(BF16)
    lo = (x - hi.astype(F32)).astype(BF16)
    d = lambda p, q: jnp.dot(p[:, q:q + 256], seg, preferred_element_type=F32)
    return jnp.concatenate([d(hi, q) + d(lo, q) for q in range(0, w, 256)], axis=1)


def _rwkv_scan_kernel(*refs, L, tc, has_v):
    (s_ref, a2_ref, r2_ref, b2_ref, k2_ref, v2_ref, pl_ref, phi_ref, psi_ref, theta_ref, yloc_ref,
     y_ref, bonus_ref, mab_ref, tinv_ref, mak_ref, mrb_ref, mrk_ref, av_ref) = refs[-19:]
    refs = refs[:-19]
    if has_v:
        (r_ref, k_ref, v_ref, lw_ref, a_ref, gate_ref, vf_ref, vg_ref,
         kk_ref, ka_ref, rk_ref, lng_ref, lnb_ref, o_ref) = refs
    else:
        (r_ref, k_ref, v_ref, lw_ref, a_ref, gate_ref,
         kk_ref, ka_ref, rk_ref, lng_ref, lnb_ref, o_ref) = refs
    W = r_ref.shape[-1]
    P2 = 2 * L
    npair = W // P2
    N = RWKV_HEAD

    @pl.when(pl.program_id(2) == 0)
    def _():
        s_ref[...] = jnp.zeros_like(s_ref)

    ri = lax.broadcasted_iota(jnp.int32, (P2, P2), 0)
    ci = lax.broadcasted_iota(jnp.int32, (P2, P2), 1)
    strict = ri > ci
    incl = ri >= ci
    eye = (ri == ci).astype(F32)
    levels = []
    s = 1
    while s < L:
        levels.append(((ri // s) % 2 == 1) & ((ci // s) == (ri // s) - 1))
        s *= 2
    head0 = lax.broadcasted_iota(jnp.int32, (L, P2), 1) < N
    sr = lax.broadcasted_iota(jnp.int32, (256, 256), 0) // N
    sc = lax.broadcasted_iota(jnp.int32, (256, 256), 1) // N
    seg = (sr == sc).astype(BF16)

    nc = tc // L
    tr = lax.broadcasted_iota(jnp.int32, (tc, tc), 0)
    tcol = lax.broadcasted_iota(jnp.int32, (tc, tc), 1)
    tri = jnp.where(tr >= tcol, jnp.where(tr // L == tcol // L, 1.0, 0.0), 0.0).astype(BF16)
    del tr, tcol

    r = r_ref[...]
    k = k_ref[...]
    v = v_ref[...]
    lw = lw_ref[...]
    a = a_ref[...]
    if has_v:
        v = v + (vf_ref[...] - v) * vg_ref[...]
    kk = k * kk_ref[...]
    kk = kk / jnp.maximum(jnp.sqrt(_seg_sum(kk * kk, seg)), 1e-12)
    kmod = k * (1.0 + (a - 1.0) * ka_ref[...])
    c = _dot_exact_lhs(tri, lw)
    enc = jnp.exp(-c)
    bonus_ref[...] = _seg_sum(r * kmod * rk_ref[...], seg) * v
    operands = (-kk * jnp.exp(c - lw), r * jnp.exp(c), kk * a * enc, kmod * enc, v)
    for z, z_ref in zip(operands, (a2_ref, r2_ref, b2_ref, k2_ref, v2_ref)):
        for ch in range(nc):
            for p in range(npair):
                zz = z[ch * L:(ch + 1) * L, P2 * p:P2 * (p + 1)]
                z_ref[ch, p, :L] = jnp.where(head0, zz, 0.0).astype(BF16)
                z_ref[ch, p, L:] = jnp.where(head0, 0.0, zz).astype(BF16)
    for ch in range(nc):
        pl_ref[ch] = jnp.broadcast_to(jnp.exp(c[(ch + 1) * L - 1:(ch + 1) * L, :]), (8, W))
    del r, k, v, lw, a, kk, kmod, c, enc, operands

    chains = [(ch, p) for ch in range(nc) for p in range(npair)]
    for g0 in range(0, len(chains), RWKV_CHAINS):
        group = chains[g0:g0 + RWKV_CHAINS]
        for c in group:
            g = _dot_nt(jnp.concatenate([a2_ref[c], r2_ref[c]], axis=0),
                        jnp.concatenate([b2_ref[c], k2_ref[c]], axis=0))
            m_ab = jnp.where(strict, g[:P2, :P2], 0.0)
            mab_ref[c] = m_ab
            tinv_ref[c] = eye + jnp.where(levels[0], m_ab, 0.0)
            mak_ref[c] = jnp.where(strict, g[:P2, P2:], 0.0).astype(BF16)
            mrb_ref[c] = jnp.where(incl, g[P2:, :P2], 0.0).astype(BF16)
            mrk_ref[c] = jnp.where(incl, g[P2:, P2:], 0.0).astype(BF16)
        for lv in levels[1:]:
            steps = [_dot(jnp.where(lv, mab_ref[c], 0.0), tinv_ref[c]).astype(BF16) for c in group]
            for c, step in zip(group, steps):
                t_cur = tinv_ref[c]
                tinv_ref[c] = t_cur + _dot(t_cur, step)
        mvs = [_dot(mak_ref[c], v2_ref[c]).astype(BF16) for c in group]
        for c, mv in zip(group, mvs):
            av_ref[c] = _dot(tinv_ref[c], jnp.concatenate([a2_ref[c], mv], axis=1)).astype(BF16)
        ths = [_dot(mrb_ref[c], av_ref[c]) for c in group]
        yls = [_dot(mrk_ref[c], v2_ref[c]) for c in group]
        for c, th, yl in zip(group, ths, yls):
            theta_ref[c] = (r2_ref[c].astype(F32) + th[:, :P2]).astype(BF16)
            yloc_ref[c] = th[:, P2:] + yl
        for c in group:
            av = av_ref[c]
            pp = _dot_tn(jnp.concatenate([av[:, P2:], av[:, :P2]], axis=1), b2_ref[c])
            p_last = pl_ref[c[0]][0:1, P2 * c[1]:P2 * (c[1] + 1)]
            phi_ref[c] = ((eye + pp[P2:]) * p_last).astype(BF16)
            psi_ref[c] = (pp[:P2] + _dot_tn(v2_ref[c], k2_ref[c])) * p_last

    for ch in range(nc):
        for p in range(npair):
            s0 = s_ref[p]
            yo = _dot_nt(theta_ref[ch, p], s0) + yloc_ref[ch, p]
            y_ref[ch * L:(ch + 1) * L, P2 * p:P2 * (p + 1)] = yo[:L] + yo[L:]
            s_ref[p] = _dot(s0, phi_ref[ch, p]) + psi_ref[ch, p]

    y = y_ref[...]
    mean = _seg_sum(y, seg) * (1.0 / N)
    yc = y - mean
    var = _seg_sum(yc * yc, seg) * (1.0 / N)
    yn = yc * lax.rsqrt(var + RWKV_GN_EPS) * lng_ref[...] + lnb_ref[...]
    o_ref[...] = ((yn + bonus_ref[...]) * gate_ref[...]).astype(o_ref.dtype)


def _rwkv_scan(rkv, lw, a, gate, v_first, vgate, k_k, k_a, r_k, lnx_g, lnx_b, bsz, seq, tc=256, wb=512):
    _, t, d = rkv.shape
    L = RWKV_CHUNK
    tc, wb = min(tc, seq), min(wb, d)
    assert seq % tc == 0 and tc % L == 0 and d % wb == 0 and wb % 256 == 0
    has_v = v_first is not None
    nc, p2, npair = tc // L, 2 * L, wb // (2 * L)
    rkv4 = rkv.reshape(3, bsz, seq, d)
    b3 = lambda z: z.reshape(bsz, seq, d)
    blk = pl.BlockSpec((None, tc, wb), lambda b, h, c: (b, c, h))
    rkv_spec = lambda n: pl.BlockSpec((None, None, tc, wb), lambda b, h, c: (n, b, c, h))
    prow = pl.BlockSpec((1, wb), lambda b, h, c: (0, h))
    args = [rkv4, rkv4, rkv4, b3(lw), b3(a), b3(gate)]
    in_specs = [rkv_spec(0), rkv_spec(1), rkv_spec(2), blk, blk, blk]
    if has_v:
        args += [v_first.reshape(3, bsz, seq, d), b3(vgate)]
        in_specs += [rkv_spec(2), blk]
    args += [z.reshape(1, d) for z in (k_k, k_a, r_k, lnx_g, lnx_b)]
    in_specs += [prow] * 5
    out = pl.pallas_call(
        functools.partial(_rwkv_scan_kernel, L=L, tc=tc, has_v=has_v),
        grid=(bsz, d // wb, seq // tc), in_specs=in_specs, out_specs=blk,
        out_shape=jax.ShapeDtypeStruct((bsz, seq, d), BF16),
        scratch_shapes=[pltpu.VMEM((npair, p2, p2), F32)]
        + [pltpu.VMEM((nc, npair, p2, p2), BF16)] * 5
        + [pltpu.VMEM((nc, 8, wb), F32),
           pltpu.VMEM((nc, npair, p2, p2), BF16), pltpu.VMEM((nc, npair, p2, p2), F32),
           pltpu.VMEM((nc, npair, p2, p2), BF16), pltpu.VMEM((nc, npair, p2, p2), F32),
           pltpu.VMEM((tc, wb), F32), pltpu.VMEM((tc, wb), F32),
           pltpu.VMEM((nc, npair, p2, p2), F32), pltpu.VMEM((nc, npair, p2, p2), F32),
           pltpu.VMEM((nc, npair, p2, p2), BF16), pltpu.VMEM((nc, npair, p2, p2), BF16),
           pltpu.VMEM((nc, npair, p2, p2), BF16),
           pltpu.VMEM((nc, npair, p2, 2 * p2), BF16)],
        compiler_params=_cparams("parallel", "parallel", "arbitrary"), name="rwkv_scan")(*args)
    return out.reshape(t, d)


def _rwkv_layer(x, norm_g, p, vres, v_first, bsz, seq):
    (mu, w_rkv, w0, w1, w2, a0, a1, a2, g1, g2, k_k, k_a, r_k, lnx_g, lnx_b, w_o) = p
    outs = _rwkv_prep(x, norm_g, mu, w0, w1, w2, a0, a1, a2, g1, g2, vres, seq)
    xs, lw, a, gate = outs[:4]
    vgate = outs[4] if vres is not None else None
    rkv = _bmm(xs, w_rkv.astype(BF16))
    o = _rwkv_scan(rkv, lw, a, gate, v_first if vres is not None else None, vgate,
                   k_k, k_a, r_k, lnx_g, lnx_b, bsz, seq)
    x = _mm(o, w_o.astype(BF16), res=x)
    return x, (rkv if vres is None else v_first)


def _pick_tile(n, cap):
    best = LANES
    for tile in range(LANES, cap + 1, LANES):
        if n % tile == 0:
            best = tile
    return best


def _mlstm_kernel(q_ref, k_ref, v_ref, o_ref, gt_ref, bias_ref, ng_ref, out_ref, c_ref, m_ref, *, nh):
    lc, dk = q_ref.shape
    dv = v_ref.shape[1]
    h = pl.program_id(1)

    @pl.when(pl.program_id(2) == 0)
    def _():
        c_ref[...] = jnp.zeros_like(c_ref)
        m_ref[...] = jnp.zeros_like(m_ref)

    lane = lax.broadcasted_iota(jnp.int32, (lc, LANES), 1)
    z = gt_ref[...] + bias_ref[...]
    zc = MLSTM_GATE_CAP * jnp.tanh(z / MLSTM_GATE_CAP)
    lf_all = jnp.minimum(zc, 0.0) - jnp.log1p(jnp.exp(-jnp.abs(zc)))
    rr = lax.broadcasted_iota(jnp.int32, (lc, lc), 0)
    cc = lax.broadcasted_iota(jnp.int32, (lc, lc), 1)
    causal = rr >= cc
    bcum_all = _dot_exact_lhs(causal.astype(BF16), lf_all)
    comb = jnp.where(lane < nh, zc, bcum_all)
    li_col = jnp.sum(jnp.where(lane == h, comb, 0.0), axis=-1, keepdims=True)
    bc_col = jnp.sum(jnp.where(lane == h + nh, comb, 0.0), axis=-1, keepdims=True)
    er = lax.broadcasted_iota(jnp.int32, (8, LANES), 0)
    ec = lax.broadcasted_iota(jnp.int32, (8, LANES), 1)
    sel = jnp.where(er == 0, jnp.where(ec == h, 1.0, 0.0),
                    jnp.where(er == 1, jnp.where(ec == h + nh, 1.0, 0.0), 0.0)).astype(BF16)
    hi, mid, lo = _split3(comb)
    tr = lambda p: lax.dot_general(sel, p, (((1,), (1,)), ((), ())), preferred_element_type=F32)
    rows = tr(hi) + (tr(mid) + tr(lo))
    li_row, bc_row = rows[0:1], rows[1:2]

    m_st = m_ref[0:1, 0:1]
    dmat = jnp.where(causal, bc_col - bc_row + li_row, -jnp.inf)
    inter = bc_col + m_st
    m_t = jnp.maximum(inter, jnp.max(dmat, axis=-1, keepdims=True))
    q = q_ref[...] * (dk ** -0.5)
    k = k_ref[...]
    sc = _dot_nt(q, k) * jnp.exp(dmat - m_t)
    w_inter = jnp.exp(inter - m_t)
    ones_blk = (lane == 0).astype(BF16)
    v_ext = jnp.concatenate([v_ref[...].astype(BF16), ones_blk], axis=1)
    c_st = c_ref[...]
    nd = _dot(sc, v_ext) + w_inter * _dot(q, c_st)
    den = nd[:, dv:dv + 1]
    hc = nd[:, :dv] / jnp.maximum(jnp.abs(den), jnp.exp(-m_t))

    b_tot = bc_col[lc - 1:lc]
    log_wk = b_tot - bc_col + li_col
    m_new = jnp.maximum(b_tot + m_st, jnp.max(log_wk, axis=0, keepdims=True))
    c_ref[...] = jnp.exp(b_tot + m_st - m_new) * c_st + _dot_tn(k * jnp.exp(log_wk - m_new), v_ext)
    m_ref[...] = jnp.broadcast_to(m_new, m_ref.shape)

    hn = hc * lax.rsqrt(jnp.mean(hc * hc, axis=-1, keepdims=True) + RMS_EPS) * ng_ref[...]
    out_ref[...] = (hn * _sigmoid(o_ref[...])).astype(out_ref.dtype)


def _mlstm_layer(x, mix_g, w_in, b_if, norm_g, w_o, bsz, seq, lc=256):
    t, d = x.shape
    nh = MLSTM_HEADS
    dk, dv = d // 2 // nh, d // nh
    nq = 2 * nh * dk + 2 * nh * dv
    assert w_in.shape[1] == nq + 2 * nh and dk % LANES == 0
    lc = min(lc, seq)
    assert seq % lc == 0
    n = nq + LANES
    w = jnp.pad(w_in, ((0, 0), (0, n - w_in.shape[1]))).astype(BF16)
    proj = _norm_mm(x, mix_g, w, tn=_pick_tile(n, 1024)).reshape(bsz, seq, n)
    bias = jnp.zeros((1, LANES), F32).at[0, :nh].set(b_if[0]).at[0, nh:2 * nh].set(b_if[1])
    out = pl.pallas_call(
        functools.partial(_mlstm_kernel, nh=nh),
        grid=(bsz, nh, seq // lc),
        in_specs=[pl.BlockSpec((None, lc, dk), lambda b, h, c: (b, c, h)),
                  pl.BlockSpec((None, lc, dk), lambda b, h, c: (b, c, nh + h)),
                  pl.BlockSpec((None, lc, dv), lambda b, h, c: (b, c, nh + h)),
                  pl.BlockSpec((None, lc, dv), lambda b, h, c: (b, c, 2 * nh + h)),
                  pl.BlockSpec((None, lc, LANES), lambda b, h, c: (b, c, nq // LANES)),
                  pl.BlockSpec((1, LANES), lambda b, h, c: (0, 0)),
                  pl.BlockSpec((1, dv), lambda b, h, c: (0, h))],
        out_specs=pl.BlockSpec((None, lc, dv), lambda b, h, c: (b, c, h)),
        out_shape=jax.ShapeDtypeStruct((bsz, seq, nh * dv), BF16),
        scratch_shapes=[pltpu.VMEM((dk, dv + LANES), F32), pltpu.VMEM((8, LANES), F32)],
        compiler_params=_cparams("parallel", "parallel", "arbitrary"), name="mlstm")(
            proj, proj, proj, proj, proj, bias, norm_g.reshape(1, nh * dv))
    return _mm(out.reshape(t, nh * dv), w_o.astype(BF16), res=x)


DSA_TQ = 128
DSA_TK = 256
DSA_HEAD_GROUP = 4
DSA_NEG = -1e30
INT32_MIN = -(2 ** 31)
KEY_NEG_INF = -2139095041
IDX_BIG = 2 ** 30
LOG2E = math.log2(math.e)


def _dsa_prep_kernel(p_ref, qg_ref, kg_ref, qn_ref, qi_ref, k_ref, v_ref, ki_ref, wi_ref, *, nh, nih):
    dh = DSA_HEAD_DIM
    qscale = dh ** -0.5 * LOG2E
    for h in range(nh):
        qn_ref[h] = (_rms(p_ref[:, h * dh:(h + 1) * dh], qg_ref[...]) * qscale).astype(BF16)
    base = nh * dh
    k_ref[...] = _rms(p_ref[:, base:base + dh], kg_ref[...]).astype(BF16)
    v_ref[:, :dh] = p_ref[:, base + dh:base + 2 * dh].astype(BF16)
    v_ref[:, dh:] = jnp.ones((v_ref.shape[0], LANES), BF16)
    base += 2 * dh
    for h in range(nih):
        qi_ref[h] = p_ref[:, base + h * LANES:base + (h + 1) * LANES].astype(BF16)
    base += nih * LANES
    ki_ref[...] = p_ref[:, base:base + LANES].astype(BF16)
    wi_ref[...] = p_ref[:, base + LANES:base + 2 * LANES]


def _dsa_kernel(qn_ref, qi_ref, wi_ref, k_ref, v_ref, ki_ref, nb_ref, o_ref,
                key_ref, w_ref, acc_ref, m_ref, *, nh, nih, n_sel, idx_bits):
    tq, tk, dh = DSA_TQ, DSA_TK, DSA_HEAD_DIM
    hg = DSA_HEAD_GROUP
    q0 = pl.program_id(1) * tq
    jd = (q0 + tq - 1) // tk
    nt = jd + 1
    rowi = lax.broadcasted_iota(jnp.int32, (tq, tk), 0)
    coli = lax.broadcasted_iota(jnp.int32, (tq, tk), 1)
    ktile = lambda jt: pl.ds(pl.multiple_of(jt * tk, tk), tk)
    twice = lambda z: jnp.concatenate([z] * (tk // LANES), axis=1)

    wi = wi_ref[...]
    for h in range(nih):
        w_ref[h] = jnp.broadcast_to(wi[:, h:h + 1], (tq, LANES))

    def by_two(n, body):
        def pair(i, c):
            body(2 * i)
            body(2 * i + 1)
            return c
        lax.fori_loop(0, n // 2, pair, 0)

        @pl.when(n % 2 == 1)
        def _():
            body(n - 1)

    def score_tile(jt):
        ki_t = ki_ref[ktile(jt), :]
        score = jnp.zeros((tq, tk), F32)
        for g0 in range(0, nih, hg):
            lg = _dot_nt(qi_ref[g0:g0 + hg].reshape(hg * tq, LANES), ki_t)
            for h in range(hg):
                score = score + jnp.maximum(lg[h * tq:(h + 1) * tq], 0.0) * twice(w_ref[g0 + h])
        score = jnp.where(jt * tk + coli <= q0 + rowi, score, -jnp.inf)
        bits = pltpu.bitcast(score, jnp.int32)
        key_ref[jt] = bits ^ ((bits >> 31) & 0x7FFFFFFF)

    by_two(nt, score_tile)

    def count(hit):
        def body(jt, acc):
            keyt = key_ref[jt]
            for c0 in range(0, tk, LANES):
                acc = acc + hit(keyt[:, c0:c0 + LANES], jt * tk + c0)
            return acc
        acc = lax.fori_loop(0, nt, body, jnp.zeros((tq, LANES), F32))
        return jnp.sum(acc, axis=-1, keepdims=True)

    wide = lambda col: jnp.broadcast_to(col, (tq, LANES))
    lane = lax.broadcasted_iota(jnp.int32, (tq, LANES), 1)

    def value_bit(b, prefix):
        cand = prefix + jnp.left_shift(jnp.int32(1), 31 - b)
        cw = wide(cand)
        cnt = count(lambda kv, base: jnp.where(kv >= cw, 1.0, 0.0))
        return jnp.where(cnt >= n_sel, cand, prefix)

    thr = lax.fori_loop(0, 32, value_bit, jnp.full((tq, 1), INT32_MIN, jnp.int32))
    tw = wide(thr)
    c_gt = count(lambda kv, base: jnp.where(kv > tw, 1.0, 0.0))
    c_ge = count(lambda kv, base: jnp.where(kv >= tw, 1.0, 0.0))
    need = n_sel - c_gt
    c_eq = c_ge - c_gt

    def tie_search():
        def index_bit(b, jcur):
            cand = jcur + jnp.left_shift(jnp.int32(1), idx_bits - 1 - b)
            cw = wide(cand)
            f = count(lambda kv, base: jnp.where(kv == tw, jnp.where(base + lane < cw, 1.0, 0.0), 0.0))
            return jnp.where(f <= need, cand, jcur)
        return lax.fori_loop(0, idx_bits, index_bit, jnp.zeros((tq, 1), jnp.int32))

    excess = jnp.max(c_eq - need) > 0.0
    jc = lax.cond(excess, tie_search, lambda: jnp.full((tq, 1), IDX_BIG, jnp.int32))
    jc = jnp.where(c_eq > need, jc, IDX_BIG)
    jc = jnp.where(thr == KEY_NEG_INF, 0, jc)

    m_ref[...] = jnp.full_like(m_ref, DSA_NEG)
    acc_ref[...] = jnp.zeros_like(acc_ref)

    def attend_tile(jt, near_idx):
        k_t = k_ref[ktile(jt), :]
        v_t = v_ref[ktile(jt), :]
        keyt = key_ref[jt]
        tie_mb = jnp.where(keyt == thr, jnp.where(jt * tk + coli < jc, 0.0, DSA_NEG), DSA_NEG)
        mb = jnp.where(keyt > thr, 0.0, tie_mb)
        for g0 in range(0, nh, hg):
            s_g = _dot_nt(qn_ref[g0:g0 + hg].reshape(hg * tq, dh), k_t)
            ps, alphas = [], []
            for h in range(hg):
                rows = slice((g0 + h) * tq, (g0 + h + 1) * tq)
                s = s_g[h * tq:(h + 1) * tq] + (mb if near_idx is None else nb_ref[near_idx, g0 + h] + mb)
                m_old = m_ref[rows, :]
                m_new = jnp.maximum(m_old, jnp.max(s, axis=-1, keepdims=True))
                m_ref[rows, :] = m_new
                ps.append(jnp.exp2(s - twice(m_new)).astype(BF16))
                alphas.append(jnp.exp2(m_old - m_new))
            pv = jnp.dot(jnp.concatenate(ps, axis=0), v_t, preferred_element_type=F32)
            rows_g = slice(g0 * tq, (g0 + hg) * tq)
            acc_ref[rows_g, :] = acc_ref[rows_g, :] * twice(jnp.concatenate(alphas, axis=0)) + pv

    on_tile_edge = q0 == jd * tk
    has_prev_near = jnp.logical_and(on_tile_edge, jd >= 1)
    n_far = jnp.where(has_prev_near, jd - 1, jd)

    by_two(n_far, lambda jt: attend_tile(jt, None))

    @pl.when(has_prev_near)
    def _():
        attend_tile(jd - 1, 2)

    attend_tile(jd, jnp.where(on_tile_edge, 0, 1))

    for h in range(nh):
        rows = slice(h * tq, (h + 1) * tq)
        o_ref[:, h * dh:(h + 1) * dh] = (acc_ref[rows, :dh] / acc_ref[rows, dh:]).astype(o_ref.dtype)


def _t5_bucket(rel):
    n = jnp.maximum(rel, 0)
    exact = T5_BUCKETS // 2
    nf = jnp.maximum(n, exact).astype(F32)
    large = exact + (jnp.log(nf / exact) / math.log(T5_MAX_DISTANCE / exact)
                     * (T5_BUCKETS - exact)).astype(jnp.int32)
    return jnp.where(n < exact, n, jnp.minimum(large, T5_BUCKETS - 1))


def _dsa_layer(x, mix_g, w_in, q_norm_g, k_norm_g, t5_table, w_o, bsz, seq):
    t, d = x.shape
    nh, dh, nih, di = DSA_HEADS, DSA_HEAD_DIM, IDX_HEADS, IDX_DIM
    tq, tk = DSA_TQ, DSA_TK
    assert seq % tk == 0 and tk == 2 * tq and dh == LANES and di <= LANES and nih <= LANES
    assert T5_MAX_DISTANCE <= tq
    n_sel = min(DSA_TOPK_MAX, seq // 4)
    o1, o2, o3, o4, o5 = nh * dh, nh * dh + dh, nh * dh + 2 * dh, nh * dh + 2 * dh + nih * di, nh * dh + 2 * dh + nih * di + di
    w_qi = jnp.pad(w_in[:, o3:o4].reshape(d, nih, di), ((0, 0), (0, 0), (0, LANES - di))).reshape(d, nih * LANES)
    w_ki = jnp.pad(w_in[:, o4:o5], ((0, 0), (0, LANES - di)))
    w_wi = jnp.pad(w_in[:, o5:o5 + nih] * (nih ** -0.5 * di ** -0.5), ((0, 0), (0, LANES - nih)))
    w = jnp.concatenate([w_in[:, :o3], w_qi, w_ki, w_wi], axis=1).astype(BF16)
    n = w.shape[1]
    proj = _norm_mm(x, mix_g, w, tn=_pick_tile(n, 1024)).reshape(bsz, seq, n)

    tm = min(256, seq)
    tok = lambda: pl.BlockSpec((None, tm, LANES), lambda b, i: (b, i, 0))
    qn, qi, kn, vb, kib, wis = pl.pallas_call(
        functools.partial(_dsa_prep_kernel, nh=nh, nih=nih),
        grid=(bsz, seq // tm),
        in_specs=[pl.BlockSpec((None, tm, n), lambda b, i: (b, i, 0)),
                  pl.BlockSpec((1, dh), lambda b, i: (0, 0)),
                  pl.BlockSpec((1, dh), lambda b, i: (0, 0))],
        out_specs=[pl.BlockSpec((None, nh, tm, dh), lambda b, i: (b, 0, i, 0)),
                   pl.BlockSpec((None, nih, tm, LANES), lambda b, i: (b, 0, i, 0)),
                   tok(), pl.BlockSpec((None, tm, dh + LANES), lambda b, i: (b, i, 0)), tok(), tok()],
        out_shape=[jax.ShapeDtypeStruct((bsz, nh, seq, dh), BF16),
                   jax.ShapeDtypeStruct((bsz, nih, seq, LANES), BF16),
                   jax.ShapeDtypeStruct((bsz, seq, dh), BF16),
                   jax.ShapeDtypeStruct((bsz, seq, dh + LANES), BF16),
                   jax.ShapeDtypeStruct((bsz, seq, LANES), BF16),
                   jax.ShapeDtypeStruct((bsz, seq, LANES), F32)],
        compiler_params=_cparams("parallel", "parallel"), name="dsa_prep")(
            proj, q_norm_g.reshape(1, dh), k_norm_g.reshape(1, dh))

    ii = jnp.arange(tq, dtype=jnp.int32)[:, None]
    jj = jnp.arange(tk, dtype=jnp.int32)[None, :]
    buckets = jnp.stack([_t5_bucket(off + ii - jj) for off in (0, tq, 2 * tq)])
    rel_table = (t5_table - t5_table[T5_BUCKETS - 1]).astype(F32)
    near = jnp.einsum("otkb,bh->ohtk", jax.nn.one_hot(buckets, T5_BUCKETS, dtype=F32), rel_table,
                      precision=lax.Precision.HIGHEST) * LOG2E

    seqblk = lambda: pl.BlockSpec((None, seq, LANES), lambda b, i: (b, 0, 0))
    out = pl.pallas_call(
        functools.partial(_dsa_kernel, nh=nh, nih=nih, n_sel=n_sel, idx_bits=int(seq).bit_length()),
        grid=(bsz, seq // tq),
        in_specs=[pl.BlockSpec((None, nh, tq, dh), lambda b, i: (b, 0, i, 0)),
                  pl.BlockSpec((None, nih, tq, LANES), lambda b, i: (b, 0, i, 0)),
                  pl.BlockSpec((None, tq, LANES), lambda b, i: (b, i, 0)),
                  seqblk(), pl.BlockSpec((None, seq, dh + LANES), lambda b, i: (b, 0, 0)), seqblk(),
                  pl.BlockSpec((3, nh, tq, tk), lambda b, i: (0, 0, 0, 0))],
        out_specs=pl.BlockSpec((None, tq, nh * dh), lambda b, i: (b, i, 0)),
        out_shape=jax.ShapeDtypeStruct((bsz, seq, nh * dh), BF16),
        scratch_shapes=[pltpu.VMEM((seq // tk, tq, tk), jnp.int32),
                        pltpu.VMEM((nih, tq, LANES), F32),
                        pltpu.VMEM((nh * tq, dh + LANES), F32),
                        pltpu.VMEM((nh * tq, LANES), F32)],
        compiler_params=_cparams("parallel", "arbitrary"), name="dsa_attn")(
            qn, qi, wis, kn, vb, kib, near)
    return _mm(out.reshape(t, nh * dh), w_o.astype(BF16), res=x)


def kernel(x, rwkv_mu, rwkv_w_rkv, rwkv_w0, rwkv_w1, rwkv_w2, rwkv_a0, rwkv_a1, rwkv_a2, rwkv_v0, rwkv_v1,
           rwkv_v2, rwkv_g1, rwkv_g2, rwkv_k_k, rwkv_k_a, rwkv_r_k, rwkv_lnx_g, rwkv_lnx_b, rwkv_w_o,
           mlstm_w_in, mlstm_b_if, mlstm_norm_g, mlstm_w_o, dsa_w_in, dsa_q_norm_g, dsa_k_norm_g, dsa_w_o,
           t5_bias, mix_norm_g, ffn_norm_g, ffn_w_gate, ffn_w_up, ffn_w_down):
    bsz, seq, d = x.shape
    depth = mix_norm_g.shape[0]
    h = x.reshape(bsz * seq, d)
    v_first = None
    for i in range(depth):
        kind, j = i % 3, i // 3
        if kind == 0:
            vres = None if j == 0 else (rwkv_v0[j - 1], rwkv_v1[j - 1], rwkv_v2[j - 1])
            p = (rwkv_mu[j], rwkv_w_rkv[j], rwkv_w0[j], rwkv_w1[j], rwkv_w2[j], rwkv_a0[j], rwkv_a1[j],
                 rwkv_a2[j], rwkv_g1[j], rwkv_g2[j], rwkv_k_k[j], rwkv_k_a[j], rwkv_r_k[j],
                 rwkv_lnx_g[j], rwkv_lnx_b[j], rwkv_w_o[j])
            h, v_first = _rwkv_layer(h, mix_norm_g[i], p, vres, v_first, bsz, seq)
        elif kind == 1:
            h = _mlstm_layer(h, mix_norm_g[i], mlstm_w_in[j], mlstm_b_if[j], mlstm_norm_g[j],
                             mlstm_w_o[j], bsz, seq)
        else:
            h = _dsa_layer(h, mix_norm_g[i], dsa_w_in[j], dsa_q_norm_g[j], dsa_k_norm_g[j], t5_bias,
                           dsa_w_o[j], bsz, seq)
        h = _ffn(h, ffn_norm_g[i], ffn_w_gate[i].astype(BF16), ffn_w_up[i].astype(BF16),
                 ffn_w_down[i].astype(BF16))
    return h.reshape(bsz, seq, d)
```

```python
import functools
import math

import jax
import jax.numpy as jnp
from jax import lax
from jax.experimental import pallas as pl
from jax.experimental.pallas import tpu as pltpu

F32 = jnp.float32
BF16 = jnp.bfloat16

V7X_VMEM_LIMIT_BYTES = 56 * 1024 * 1024
LANES = 128

RMS_EPS = 1e-6
RWKV_HEAD = 64
RWKV_DECAY_SCALE = math.exp(-0.5)
RWKV_GN_EPS = 64e-5
RWKV_CHUNK = 64
RWKV_CHAINS = 16
MLSTM_HEADS = 4
MLSTM_GATE_CAP = 15.0
DSA_HEADS = 16
DSA_HEAD_DIM = 128
IDX_HEADS = 16
IDX_DIM = 64
DSA_TOPK_MAX = 256
T5_BUCKETS = 32
T5_MAX_DISTANCE = 128


def _cparams(*sem):
    return pltpu.CompilerParams(dimension_semantics=sem, vmem_limit_bytes=V7X_VMEM_LIMIT_BYTES)


def _dot(a, b):
    return jnp.dot(a.astype(BF16), b.astype(BF16), preferred_element_type=F32)


def _dot_nt(a, b):
    return lax.dot_general(a.astype(BF16), b.astype(BF16), (((1,), (1,)), ((), ())),
                           preferred_element_type=F32)


def _dot_tn(a, b):
    return lax.dot_general(a.astype(BF16), b.astype(BF16), (((0,), (0,)), ((), ())),
                           preferred_element_type=F32)


def _split3(x):
    hi = x.astype(BF16)
    r1 = x - hi.astype(F32)
    mid = r1.astype(BF16)
    lo = (r1 - mid.astype(F32)).astype(BF16)
    return hi, mid, lo


def _dot_exact_lhs(a_bf16, x):
    hi, mid, lo = _split3(x)
    d = lambda p: jnp.dot(a_bf16, p, preferred_element_type=F32)
    return d(hi) + (d(mid) + d(lo))


def _rms(x, g):
    ms = jnp.mean(x * x, axis=-1, keepdims=True)
    return x * lax.rsqrt(ms + RMS_EPS) * g


def _sigmoid(x):
    return 1.0 / (1.0 + jnp.exp(-x))


def _mm_kernel(a_ref, w_ref, o_ref):
    o_ref[...] = jnp.dot(a_ref[...], w_ref[...], preferred_element_type=F32).astype(o_ref.dtype)


def _mm_res_kernel(a_ref, w_ref, r_ref, o_ref):
    o_ref[...] = (r_ref[...] + jnp.dot(a_ref[...], w_ref[...], preferred_element_type=F32)).astype(o_ref.dtype)


def _mm(a, w, res=None, out_dtype=F32, tm=1024, tn=512):
    m, k = a.shape
    n = w.shape[1]
    tm, tn = min(tm, m), min(tn, n)
    assert m % tm == 0 and n % tn == 0
    in_specs = [pl.BlockSpec((tm, k), lambda i, j: (i, 0)),
                pl.BlockSpec((k, tn), lambda i, j: (0, j))]
    args = [a, w]
    kern = _mm_kernel
    if res is not None:
        in_specs.append(pl.BlockSpec((tm, tn), lambda i, j: (i, j)))
        args.append(res)
        kern = _mm_res_kernel
    return pl.pallas_call(
        kern, grid=(m // tm, n // tn), in_specs=in_specs,
        out_specs=pl.BlockSpec((tm, tn), lambda i, j: (i, j)),
        out_shape=jax.ShapeDtypeStruct((m, n), out_dtype),
        compiler_params=_cparams("parallel", "arbitrary"), name="mm")(*args)


def _bmm(a, w, out_dtype=F32, tm=1024, tn=512):
    g, m, k = a.shape
    n = w.shape[2]
    tm, tn = min(tm, m), min(tn, n)
    assert m % tm == 0 and n % tn == 0
    return pl.pallas_call(
        _mm_kernel, grid=(g, m // tm, n // tn),
        in_specs=[pl.BlockSpec((None, tm, k), lambda b, i, j: (b, i, 0)),
                  pl.BlockSpec((None, k, tn), lambda b, i, j: (b, 0, j))],
        out_specs=pl.BlockSpec((None, tm, tn), lambda b, i, j: (b, i, j)),
        out_shape=jax.ShapeDtypeStruct((g, m, n), out_dtype),
        compiler_params=_cparams("parallel", "parallel", "arbitrary"), name="bmm")(a, w)


def _norm_mm_kernel(x_ref, g_ref, w_ref, o_ref, h_ref):
    @pl.when(pl.program_id(1) == 0)
    def _():
        h_ref[...] = _rms(x_ref[...], g_ref[...]).astype(BF16)

    o_ref[...] = jnp.dot(h_ref[...], w_ref[...], preferred_element_type=F32).astype(o_ref.dtype)


def _norm_mm(x, g, w, out_dtype=F32, tm=1024, tn=512):
    m, k = x.shape
    n = w.shape[1]
    tm, tn = min(tm, m), min(tn, n)
    assert m % tm == 0 and n % tn == 0
    return pl.pallas_call(
        _norm_mm_kernel, grid=(m // tm, n // tn),
        in_specs=[pl.BlockSpec((tm, k), lambda i, j: (i, 0)),
                  pl.BlockSpec((1, k), lambda i, j: (0, 0)),
                  pl.BlockSpec((k, tn), lambda i, j: (0, j))],
        out_specs=pl.BlockSpec((tm, tn), lambda i, j: (i, j)),
        out_shape=jax.ShapeDtypeStruct((m, n), out_dtype),
        scratch_shapes=[pltpu.VMEM((tm, k), BF16)],
        compiler_params=_cparams("parallel", "arbitrary"), name="norm_mm")(x, g.reshape(1, k), w)


def _ffn_kernel(x_ref, g_ref, wg_ref, wu_ref, wd_ref, o_ref, h_ref):
    @pl.when(pl.program_id(1) == 0)
    def _():
        x = x_ref[...]
        h_ref[...] = _rms(x, g_ref[...]).astype(BF16)
        o_ref[...] = x

    h = h_ref[...]
    gate = jnp.dot(h, wg_ref[...], preferred_element_type=F32)
    up = jnp.dot(h, wu_ref[...], preferred_element_type=F32)
    act = (gate * _sigmoid(gate) * up).astype(BF16)
    o_ref[...] += jnp.dot(act, wd_ref[...], preferred_element_type=F32)


def _ffn(x, g, wg, wu, wd, tm=512, tf=512):
    m, d = x.shape
    f = wg.shape[1]
    tm, tf = min(tm, m), min(tf, f)
    assert m % tm == 0 and f % tf == 0
    return pl.pallas_call(
        _ffn_kernel, grid=(m // tm, f // tf),
        in_specs=[pl.BlockSpec((tm, d), lambda i, j: (i, 0)),
                  pl.BlockSpec((1, d), lambda i, j: (0, 0)),
                  pl.BlockSpec((d, tf), lambda i, j: (0, j)),
                  pl.BlockSpec((d, tf), lambda i, j: (0, j)),
                  pl.BlockSpec((tf, d), lambda i, j: (j, 0))],
        out_specs=pl.BlockSpec((tm, d), lambda i, j: (i, 0)),
        out_shape=jax.ShapeDtypeStruct((m, d), F32),
        scratch_shapes=[pltpu.VMEM((tm, d), BF16)],
        compiler_params=_cparams("parallel", "arbitrary"), name="ffn")(x, g.reshape(1, d), wg, wu, wd)


def _rwkv_prep_kernel(*refs, seq, tm, has_v):
    if has_v:
        (x_ref, xp_ref, g_ref, mu_ref, w0_ref, w1_ref, w2_ref, a0_ref, a1_ref, a2_ref,
         g1_ref, g2_ref, v0_ref, v1_ref, v2_ref, xs_ref, lw_ref, a_ref, gate_ref, vg_ref) = refs
    else:
        (x_ref, xp_ref, g_ref, mu_ref, w0_ref, w1_ref, w2_ref, a0_ref, a1_ref, a2_ref,
         g1_ref, g2_ref, xs_ref, lw_ref, a_ref, gate_ref) = refs
    i = pl.program_id(0)
    gn = g_ref[...]
    h = _rms(x_ref[...], gn)
    hp = _rms(xp_ref[...], gn)
    seq_start = (i * tm) % seq == 0
    hp_row = jnp.where(seq_start, 0.0, hp[7:8, :])
    row = lax.broadcasted_iota(jnp.int32, (tm, 1), 0)
    h_prev = jnp.where(row == 0, hp_row, pltpu.roll(h, 1, 0))
    xx = h_prev - h
    mix = lambda n: h + xx * mu_ref[n:n + 1, :]
    xs_ref[0] = mix(0).astype(BF16)
    xs_ref[1] = mix(2).astype(BF16)
    xv = mix(3).astype(BF16)
    xs_ref[2] = xv
    lw_ref[...] = -RWKV_DECAY_SCALE * _sigmoid(
        w0_ref[...] + _dot(jnp.tanh(_dot(mix(1), w1_ref[...])), w2_ref[...]))
    a_ref[...] = _sigmoid(a0_ref[...] + _dot(_dot(mix(4), a1_ref[...]), a2_ref[...]))
    gate_ref[...] = _dot(_sigmoid(_dot(mix(5), g1_ref[...])), g2_ref[...])
    if has_v:
        vg_ref[...] = _sigmoid(v0_ref[...] + _dot(_dot(xv, v1_ref[...]), v2_ref[...]))


def _pad_lora(w_in, w_out):
    r = w_in.shape[1]
    rp = -(-r // LANES) * LANES
    return (jnp.pad(w_in, ((0, 0), (0, rp - r))).astype(BF16),
            jnp.pad(w_out, ((0, rp - r), (0, 0))).astype(BF16))


def _rwkv_prep(x, norm_g, mu, w0, w1, w2, a0, a1, a2, g1, g2, vres, seq, tm=256):
    t, d = x.shape
    tm = min(tm, seq)
    assert t % tm == 0 and seq % tm == 0 and tm % 8 == 0
    has_v = vres is not None
    row = lambda v: v.reshape(1, d)
    full = lambda a: pl.BlockSpec(a.shape, lambda i: (0,) * a.ndim)
    w1p, w2p = _pad_lora(w1, w2)
    a1p, a2p = _pad_lora(a1, a2)
    g1p, g2p = _pad_lora(g1, g2)
    mu8 = jnp.pad(mu, ((0, 2), (0, 0)))
    params = [row(norm_g), mu8, row(w0), w1p, w2p, row(a0), a1p, a2p, g1p, g2p]
    if has_v:
        v1p, v2p = _pad_lora(vres[1], vres[2])
        params += [row(vres[0]), v1p, v2p]
    tile = pl.BlockSpec((tm, d), lambda i: (i, 0))
    in_specs = [tile, pl.BlockSpec((8, d), lambda i: (jnp.maximum(i * (tm // 8) - 1, 0), 0))]
    in_specs += [full(p) for p in params]
    n_f32 = 4 if has_v else 3
    out_shape = [jax.ShapeDtypeStruct((3, t, d), BF16)] + [jax.ShapeDtypeStruct((t, d), F32)] * n_f32
    out_specs = [pl.BlockSpec((3, tm, d), lambda i: (0, i, 0))] + [tile] * n_f32
    return pl.pallas_call(
        functools.partial(_rwkv_prep_kernel, seq=seq, tm=tm, has_v=has_v),
        grid=(t // tm,), in_specs=in_specs, out_specs=out_specs, out_shape=out_shape,
        compiler_params=_cparams("parallel"), name="rwkv_prep")(x, x, *params)


def _seg_sum(x, seg):
    w = x.shape[1]
    hi = x.astype(BF16)
    lo = (x - hi.astype(F32)).astype(BF16)
    d = lambda p, q: jnp.dot(p[:, q:q + 256], seg, preferred_element_type=F32)
    return jnp.concatenate([d(hi, q) + d(lo, q) for q in range(0, w, 256)], axis=1)


def _rwkv_scan_kernel(*refs, L, tc, has_v):
    (s_ref, a2_ref, r2_ref, b2_ref, k2_ref, v2_ref, pl_ref, phi_ref, psi_ref, theta_ref, yloc_ref,
     y_ref, bonus_ref, mab_ref, tinv_ref, mak_ref, mrb_ref, mrk_ref, av_ref) = refs[-19:]
    refs = refs[:-19]
    if has_v:
        (r_ref, k_ref, v_ref, lw_ref, a_ref, gate_ref, vf_ref, vg_ref,
         kk_ref, ka_ref, rk_ref, lng_ref, lnb_ref, o_ref) = refs
    else:
        (r_ref, k_ref, v_ref, lw_ref, a_ref, gate_ref,
         kk_ref, ka_ref, rk_ref, lng_ref, lnb_ref, o_ref) = refs
    W = r_ref.shape[-1]
    P2 = 2 * L
    npair = W // P2
    N = RWKV_HEAD

    @pl.when(pl.program_id(2) == 0)
    def _():
        s_ref[...] = jnp.zeros_like(s_ref)

    ri = lax.broadcasted_iota(jnp.int32, (P2, P2), 0)
    ci = lax.broadcasted_iota(jnp.int32, (P2, P2), 1)
    strict = ri > ci
    incl = ri >= ci
    eye = (ri == ci).astype(F32)
    levels = []
    s = 1
    while s < L:
        levels.append(((ri // s) % 2 == 1) & ((ci // s) == (ri // s) - 1))
        s *= 2
    head0 = lax.broadcasted_iota(jnp.int32, (L, P2), 1) < N
    sr = lax.broadcasted_iota(jnp.int32, (256, 256), 0) // N
    sc = lax.broadcasted_iota(jnp.int32, (256, 256), 1) // N
    seg = (sr == sc).astype(BF16)

    nc = tc // L
    tr = lax.broadcasted_iota(jnp.int32, (tc, tc), 0)
    tcol = lax.broadcasted_iota(jnp.int32, (tc, tc), 1)
    tri = jnp.where(tr >= tcol, jnp.where(tr // L == tcol // L, 1.0, 0.0), 0.0).astype(BF16)
    del tr, tcol

    r = r_ref[...]
    k = k_ref[...]
    v = v_ref[...]
    lw = lw_ref[...]
    a = a_ref[...]
    if has_v:
        v = v + (vf_ref[...] - v) * vg_ref[...]
    kk = k * kk_ref[...]
    kk = kk / jnp.maximum(jnp.sqrt(_seg_sum(kk * kk, seg)), 1e-12)
    kmod = k * (1.0 + (a - 1.0) * ka_ref[...])
    c = _dot_exact_lhs(tri, lw)
    enc = jnp.exp(-c)
    bonus_ref[...] = _seg_sum(r * kmod * rk_ref[...], seg) * v
    operands = (-kk * jnp.exp(c - lw), r * jnp.exp(c), kk * a * enc, kmod * enc, v)
    for z, z_ref in zip(operands, (a2_ref, r2_ref, b2_ref, k2_ref, v2_ref)):
        for ch in range(nc):
            for p in range(npair):
                zz = z[ch * L:(ch + 1) * L, P2 * p:P2 * (p + 1)]
                z_ref[ch, p, :L] = jnp.where(head0, zz, 0.0).astype(BF16)
                z_ref[ch, p, L:] = jnp.where(head0, 0.0, zz).astype(BF16)
    for ch in range(nc):
        pl_ref[ch] = jnp.broadcast_to(jnp.exp(c[(ch + 1) * L - 1:(ch + 1) * L, :]), (8, W))
    del r, k, v, lw, a, kk, kmod, c, enc, operands

    chains = [(ch, p) for ch in range(nc) for p in range(npair)]
    for g0 in range(0, len(chains), RWKV_CHAINS):
        group = chains[g0:g0 + RWKV_CHAINS]
        for c in group:
            g = _dot_nt(jnp.concatenate([a2_ref[c], r2_ref[c]], axis=0),
                        jnp.concatenate([b2_ref[c], k2_ref[c]], axis=0))
            m_ab = jnp.where(strict, g[:P2, :P2], 0.0)
            mab_ref[c] = m_ab
            tinv_ref[c] = eye + jnp.where(levels[0], m_ab, 0.0)
            mak_ref[c] = jnp.where(strict, g[:P2, P2:], 0.0).astype(BF16)
            mrb_ref[c] = jnp.where(incl, g[P2:, :P2], 0.0).astype(BF16)
            mrk_ref[c] = jnp.where(incl, g[P2:, P2:], 0.0).astype(BF16)
        for lv in levels[1:]:
            steps = [_dot(jnp.where(lv, mab_ref[c], 0.0), tinv_ref[c]).astype(BF16) for c in group]
            for c, step in zip(group, steps):
                t_cur = tinv_ref[c]
                tinv_ref[c] = t_cur + _dot(t_cur, step)
        mvs = [_dot(mak_ref[c], v2_ref[c]).astype(BF16) for c in group]
        for c, mv in zip(group, mvs):
            av_ref[c] = _dot(tinv_ref[c], jnp.concatenate([a2_ref[c], mv], axis=1)).astype(BF16)
        ths = [_dot(mrb_ref[c], av_ref[c]) for c in group]
        yls = [_dot(mrk_ref[c], v2_ref[c]) for c in group]
        for c, th, yl in zip(group, ths, yls):
            theta_ref[c] = (r2_ref[c].astype(F32) + th[:, :P2]).astype(BF16)
            yloc_ref[c] = th[:, P2:] + yl
        for c in group:
            av = av_ref[c]
            pp = _dot_tn(jnp.concatenate([av[:, P2:], av[:, :P2]], axis=1), b2_ref[c])
            p_last = pl_ref[c[0]][0:1, P2 * c[1]:P2 * (c[1] + 1)]
            phi_ref[c] = ((eye + pp[P2:]) * p_last).astype(BF16)
            psi_ref[c] = (pp[:P2] + _dot_tn(v2_ref[c], k2_ref[c])) * p_last

    for ch in range(nc):
        for p in range(npair):
            s0 = s_ref[p]
            yo = _dot_nt(theta_ref[ch, p], s0) + yloc_ref[ch, p]
            y_ref[ch * L:(ch + 1) * L, P2 * p:P2 * (p + 1)] = yo[:L] + yo[L:]
            s_ref[p] = _dot(s0, phi_ref[ch, p]) + psi_ref[ch, p]

    y = y_ref[...]
    mean = _seg_sum(y, seg) * (1.0 / N)
    yc = y - mean
    var = _seg_sum(yc * yc, seg) * (1.0 / N)
    yn = yc * lax.rsqrt(var + RWKV_GN_EPS) * lng_ref[...] + lnb_ref[...]
    o_ref[...] = ((yn + bonus_ref[...]) * gate_ref[...]).astype(o_ref.dtype)


def _rwkv_scan(rkv, lw, a, gate, v_first, vgate, k_k, k_a, r_k, lnx_g, lnx_b, bsz, seq, tc=256, wb=512):
    _, t, d = rkv.shape
    L = RWKV_CHUNK
    tc, wb = min(tc, seq), min(wb, d)
    assert seq % tc == 0 and tc % L == 0 and d % wb == 0 and wb % 256 == 0
    has_v = v_first is not None
    nc, p2, npair = tc // L, 2 * L, wb // (2 * L)
    rkv4 = rkv.reshape(3, bsz, seq, d)
    b3 = lambda z: z.reshape(bsz, seq, d)
    blk = pl.BlockSpec((None, tc, wb), lambda b, h, c: (b, c, h))
    rkv_spec = lambda n: pl.BlockSpec((None, None, tc, wb), lambda b, h, c: (n, b, c, h))
    prow = pl.BlockSpec((1, wb), lambda b, h, c: (0, h))
    args = [rkv4, rkv4, rkv4, b3(lw), b3(a), b3(gate)]
    in_specs = [rkv_spec(0), rkv_spec(1), rkv_spec(2), blk, blk, blk]
    if has_v:
        args += [v_first.reshape(3, bsz, seq, d), b3(vgate)]
        in_specs += [rkv_spec(2), blk]
    args += [z.reshape(1, d) for z in (k_k, k_a, r_k, lnx_g, lnx_b)]
    in_specs += [prow] * 5
    out = pl.pallas_call(
        functools.partial(_rwkv_scan_kernel, L=L, tc=tc, has_v=has_v),
        grid=(bsz, d // wb, seq // tc), in_specs=in_specs, out_specs=blk,
        out_shape=jax.ShapeDtypeStruct((bsz, seq, d), BF16),
        scratch_shapes=[pltpu.VMEM((npair, p2, p2), F32)]
        + [pltpu.VMEM((nc, npair, p2, p2), BF16)] * 5
        + [pltpu.VMEM((nc, 8, wb), F32),
           pltpu.VMEM((nc, npair, p2, p2), BF16), pltpu.VMEM((nc, npair, p2, p2), F32),
           pltpu.VMEM((nc, npair, p2, p2), BF16), pltpu.VMEM((nc, npair, p2, p2), F32),
           pltpu.VMEM((tc, wb), F32), pltpu.VMEM((tc, wb), F32),
           pltpu.VMEM((nc, npair, p2, p2), F32), pltpu.VMEM((nc, npair, p2, p2), F32),
           pltpu.VMEM((nc, npair, p2, p2), BF16), pltpu.VMEM((nc, npair, p2, p2), BF16),
           pltpu.VMEM((nc, npair, p2, p2), BF16),
           pltpu.VMEM((nc, npair, p2, 2 * p2), BF16)],
        compiler_params=_cparams("parallel", "parallel", "arbitrary"), name="rwkv_scan")(*args)
    return out.reshape(t, d)


def _rwkv_layer(x, norm_g, p, vres, v_first, bsz, seq):
    (mu, w_rkv, w0, w1, w2, a0, a1, a2, g1, g2, k_k, k_a, r_k, lnx_g, lnx_b, w_o) = p
    outs = _rwkv_prep(x, norm_g, mu, w0, w1, w2, a0, a1, a2, g1, g2, vres, seq)
    xs, lw, a, gate = outs[:4]
    vgate = outs[4] if vres is not None else None
    rkv = _bmm(xs, w_rkv.astype(BF16))
    o = _rwkv_scan(rkv, lw, a, gate, v_first if vres is not None else None, vgate,
                   k_k, k_a, r_k, lnx_g, lnx_b, bsz, seq)
    x = _mm(o, w_o.astype(BF16), res=x)
    return x, (rkv if vres is None else v_first)


def _pick_tile(n, cap):
    best = LANES
    for tile in range(LANES, cap + 1, LANES):
        if n % tile == 0:
            best = tile
    return best


def _mlstm_kernel(q_ref, k_ref, v_ref, o_ref, gt_ref, bias_ref, ng_ref, out_ref, c_ref, m_ref, *, nh):
    lc, dk = q_ref.shape
    dv = v_ref.shape[1]
    h = pl.program_id(1)

    @pl.when(pl.program_id(2) == 0)
    def _():
        c_ref[...] = jnp.zeros_like(c_ref)
        m_ref[...] = jnp.zeros_like(m_ref)

    lane = lax.broadcasted_iota(jnp.int32, (lc, LANES), 1)
    z = gt_ref[...] + bias_ref[...]
    zc = MLSTM_GATE_CAP * jnp.tanh(z / MLSTM_GATE_CAP)
    lf_all = jnp.minimum(zc, 0.0) - jnp.log1p(jnp.exp(-jnp.abs(zc)))
    rr = lax.broadcasted_iota(jnp.int32, (lc, lc), 0)
    cc = lax.broadcasted_iota(jnp.int32, (lc, lc), 1)
    causal = rr >= cc
    bcum_all = _dot_exact_lhs(causal.astype(BF16), lf_all)
    comb = jnp.where(lane < nh, zc, bcum_all)
    li_col = jnp.sum(jnp.where(lane == h, comb, 0.0), axis=-1, keepdims=True)
    bc_col = jnp.sum(jnp.where(lane == h + nh, comb, 0.0), axis=-1, keepdims=True)
    er = lax.broadcasted_iota(jnp.int32, (8, LANES), 0)
    ec = lax.broadcasted_iota(jnp.int32, (8, LANES), 1)
    sel = jnp.where(er == 0, jnp.where(ec == h, 1.0, 0.0),
                    jnp.where(er == 1, jnp.where(ec == h + nh, 1.0, 0.0), 0.0)).astype(BF16)
    hi, mid, lo = _split3(comb)
    tr = lambda p: lax.dot_general(sel, p, (((1,), (1,)), ((), ())), preferred_element_type=F32)
    rows = tr(hi) + (tr(mid) + tr(lo))
    li_row, bc_row = rows[0:1], rows[1:2]

    m_st = m_ref[0:1, 0:1]
    dmat = jnp.where(causal, bc_col - bc_row + li_row, -jnp.inf)
    inter = bc_col + m_st
    m_t = jnp.maximum(inter, jnp.max(dmat, axis=-1, keepdims=True))
    q = q_ref[...] * (dk ** -0.5)
    k = k_ref[...]
    sc = _dot_nt(q, k) * jnp.exp(dmat - m_t)
    w_inter = jnp.exp(inter - m_t)
    ones_blk = (lane == 0).astype(BF16)
    v_ext = jnp.concatenate([v_ref[...].astype(BF16), ones_blk], axis=1)
    c_st = c_ref[...]
    nd = _dot(sc, v_ext) + w_inter * _dot(q, c_st)
    den = nd[:, dv:dv + 1]
    hc = nd[:, :dv] / jnp.maximum(jnp.abs(den), jnp.exp(-m_t))

    b_tot = bc_col[lc - 1:lc]
    log_wk = b_tot - bc_col + li_col
    m_new = jnp.maximum(b_tot + m_st, jnp.max(log_wk, axis=0, keepdims=True))
    c_ref[...] = jnp.exp(b_tot + m_st - m_new) * c_st + _dot_tn(k * jnp.exp(log_wk - m_new), v_ext)
    m_ref[...] = jnp.broadcast_to(m_new, m_ref.shape)

    hn = hc * lax.rsqrt(jnp.mean(hc * hc, axis=-1, keepdims=True) + RMS_EPS) * ng_ref[...]
    out_ref[...] = (hn * _sigmoid(o_ref[...])).astype(out_ref.dtype)


def _mlstm_layer(x, mix_g, w_in, b_if, norm_g, w_o, bsz, seq, lc=256):
    t, d = x.shape
    nh = MLSTM_HEADS
    dk, dv = d // 2 // nh, d // nh
    nq = 2 * nh * dk + 2 * nh * dv
    assert w_in.shape[1] == nq + 2 * nh and dk % LANES == 0
    lc = min(lc, seq)
    assert seq % lc == 0
    n = nq + LANES
    w = jnp.pad(w_in, ((0, 0), (0, n - w_in.shape[1]))).astype(BF16)
    proj = _norm_mm(x, mix_g, w, tn=_pick_tile(n, 1024)).reshape(bsz, seq, n)
    bias = jnp.zeros((1, LANES), F32).at[0, :nh].set(b_if[0]).at[0, nh:2 * nh].set(b_if[1])
    out = pl.pallas_call(
        functools.partial(_mlstm_kernel, nh=nh),
        grid=(bsz, nh, seq // lc),
        in_specs=[pl.BlockSpec((None, lc, dk), lambda b, h, c: (b, c, h)),
                  pl.BlockSpec((None, lc, dk), lambda b, h, c: (b, c, nh + h)),
                  pl.BlockSpec((None, lc, dv), lambda b, h, c: (b, c, nh + h)),
                  pl.BlockSpec((None, lc, dv), lambda b, h, c: (b, c, 2 * nh + h)),
                  pl.BlockSpec((None, lc, LANES), lambda b, h, c: (b, c, nq // LANES)),
                  pl.BlockSpec((1, LANES), lambda b, h, c: (0, 0)),
                  pl.BlockSpec((1, dv), lambda b, h, c: (0, h))],
        out_specs=pl.BlockSpec((None, lc, dv), lambda b, h, c: (b, c, h)),
        out_shape=jax.ShapeDtypeStruct((bsz, seq, nh * dv), BF16),
        scratch_shapes=[pltpu.VMEM((dk, dv + LANES), F32), pltpu.VMEM((8, LANES), F32)],
        compiler_params=_cparams("parallel", "parallel", "arbitrary"), name="mlstm")(
            proj, proj, proj, proj, proj, bias, norm_g.reshape(1, nh * dv))
    return _mm(out.reshape(t, nh * dv), w_o.astype(BF16), res=x)


DSA_TQ = 128
DSA_TK = 256
DSA_HEAD_GROUP = 4
DSA_TILES_PER_TRIP = 4
DSA_NEG = -1e30
KEY_NEG_INF = -2139095041
IDX_BIG = 2 ** 30
LOG2E = math.log2(math.e)


def _dsa_prep_kernel(p_ref, qg_ref, kg_ref, qn_ref, qi_ref, k_ref, v_ref, ki_ref, wi_ref, *, nh, nih):
    dh = DSA_HEAD_DIM
    qscale = dh ** -0.5 * LOG2E
    for h in range(nh):
        qn_ref[h] = (_rms(p_ref[:, h * dh:(h + 1) * dh], qg_ref[...]) * qscale).astype(BF16)
    base = nh * dh
    k_ref[...] = _rms(p_ref[:, base:base + dh], kg_ref[...]).astype(BF16)
    v_ref[:, :dh] = p_ref[:, base + dh:base + 2 * dh].astype(BF16)
    v_ref[:, dh:] = jnp.ones((v_ref.shape[0], LANES), BF16)
    base += 2 * dh
    for h in range(nih):
        qi_ref[h] = p_ref[:, base + h * LANES:base + (h + 1) * LANES].astype(BF16)
    base += nih * LANES
    ki_ref[...] = p_ref[:, base:base + LANES].astype(BF16)
    wi_ref[...] = p_ref[:, base + LANES:base + 2 * LANES]


def _dsa_kernel(qn_ref, qi_ref, wi_ref, k_ref, v_ref, ki_ref, nb_ref, o_ref,
                key_ref, hi_ref, lo_ref, lo2_ref, w_ref, acc_ref, m_ref, *, nh, nih, n_sel, idx_bits):
    tq, tk, dh = DSA_TQ, DSA_TK, DSA_HEAD_DIM
    hg = DSA_HEAD_GROUP
    q0 = pl.program_id(1) * tq
    jd = (q0 + tq - 1) // tk
    nt = jd + 1
    rowi = lax.broadcasted_iota(jnp.int32, (tq, tk), 0)
    coli = lax.broadcasted_iota(jnp.int32, (tq, tk), 1)
    ktile = lambda jt: pl.ds(pl.multiple_of(jt * tk, tk), tk)
    twice = lambda z: jnp.concatenate([z] * (tk // LANES), axis=1)

    wi = wi_ref[...]
    for h in range(nih):
        w_ref[h] = jnp.broadcast_to(wi[:, h:h + 1], (tq, LANES))

    def for_each_tile(n, body, per_trip=DSA_TILES_PER_TRIP):
        def trip(i, c):
            for u in range(per_trip):
                body(per_trip * i + u)
            return c
        lax.fori_loop(0, n // per_trip, trip, 0)
        done = (n // per_trip) * per_trip
        for u in range(per_trip - 1):
            @pl.when(n - done > u)
            def _():
                body(done + u)

    def score_tile(jt):
        ki_t = ki_ref[ktile(jt), :]
        score = jnp.zeros((tq, tk), F32)
        for g0 in range(0, nih, hg):
            lg = _dot_nt(qi_ref[g0:g0 + hg].reshape(hg * tq, LANES), ki_t)
            for h in range(hg):
                score = score + jnp.maximum(lg[h * tq:(h + 1) * tq], 0.0) * twice(w_ref[g0 + h])
        score = jnp.where(jt * tk + coli <= q0 + rowi, score, -jnp.inf)
        bits = pltpu.bitcast(score, jnp.int32)
        key = bits ^ ((bits >> 31) & 0x7FFFFFFF)
        key_ref[jt] = key
        key_t = pltpu.bitcast(jnp.transpose(pltpu.bitcast(key, F32)), jnp.int32)
        hi_ref[jt] = (key_t >> 16).astype(jnp.int16)
        lo_ref[jt] = ((key_t & 0xFFFF) - 32768).astype(jnp.int16)

    for_each_tile(nt, score_tile)

    def count16(ref, cand):
        cb = jnp.broadcast_to(cand.astype(jnp.int16), (32, tq))

        def body(jt, acc):
            tile = ref[jt]
            for r0 in range(0, tk, 32):
                acc = acc + jnp.where(tile[r0:r0 + 32] >= cb, jnp.int16(1), jnp.int16(0))
            return acc
        acc = lax.fori_loop(0, nt, body, jnp.zeros((32, tq), jnp.int16))
        return jnp.sum(acc.astype(F32), axis=0, keepdims=True)

    def kth_largest16(ref, kth):
        def bit(b, prefix):
            cand = prefix + jnp.left_shift(jnp.int32(1), 15 - b)
            return jnp.where(count16(ref, cand) >= kth, cand, prefix)
        return lax.fori_loop(0, 16, bit, jnp.full((1, tq), -32768, jnp.int32))

    hi_k = kth_largest16(hi_ref, n_sel)
    above = jnp.where(hi_k >= 32767, 0.0, count16(hi_ref, jnp.minimum(hi_k + 1, 32767)))
    hi_kb = jnp.broadcast_to(hi_k.astype(jnp.int16), (tk, tq))

    def bucket_tile(jt, c):
        lo2_ref[jt] = jnp.where(hi_ref[jt] == hi_kb, lo_ref[jt], jnp.int16(-32768))
        return c

    lax.fori_loop(0, nt, bucket_tile, 0)
    lo_k = kth_largest16(lo2_ref, n_sel - above)
    thr_row = hi_k * 65536 + (lo_k + 32768)
    tw = pltpu.bitcast(jnp.transpose(pltpu.bitcast(jnp.broadcast_to(thr_row, (tq, tq)), F32)), jnp.int32)
    thr = tw[:, :1]

    def count(hit):
        def body(jt, acc):
            keyt = key_ref[jt]
            for c0 in range(0, tk, LANES):
                acc = acc + hit(keyt[:, c0:c0 + LANES], jt * tk + c0)
            return acc
        acc = lax.fori_loop(0, nt, body, jnp.zeros((tq, LANES), F32))
        return jnp.sum(acc, axis=-1, keepdims=True)

    wide = lambda col: jnp.broadcast_to(col, (tq, LANES))
    lane = lax.broadcasted_iota(jnp.int32, (tq, LANES), 1)

    c_gt = count(lambda kv, base: jnp.where(kv > tw, 1.0, 0.0))
    c_ge = count(lambda kv, base: jnp.where(kv >= tw, 1.0, 0.0))
    need = n_sel - c_gt
    c_eq = c_ge - c_gt

    def tie_search():
        def index_bit(b, jcur):
            cand = jcur + jnp.left_shift(jnp.int32(1), idx_bits - 1 - b)
            cw = wide(cand)
            f = count(lambda kv, base: jnp.where(kv == tw, jnp.where(base + lane < cw, 1.0, 0.0), 0.0))
            return jnp.where(f <= need, cand, jcur)
        return lax.fori_loop(0, idx_bits, index_bit, jnp.zeros((tq, 1), jnp.int32))

    excess = jnp.max(c_eq - need) > 0.0
    jc = lax.cond(excess, tie_search, lambda: jnp.full((tq, 1), IDX_BIG, jnp.int32))
    jc = jnp.where(c_eq > need, jc, IDX_BIG)
    jc = jnp.where(thr == KEY_NEG_INF, 0, jc)

    m_ref[...] = jnp.full_like(m_ref, DSA_NEG)
    acc_ref[...] = jnp.zeros_like(acc_ref)

    def attend_tile(jt, near_idx):
        k_t = k_ref[ktile(jt), :]
        v_t = v_ref[ktile(jt), :]
        keyt = key_ref[jt]
        tie_mb = jnp.where(keyt == thr, jnp.where(jt * tk + coli < jc, 0.0, DSA_NEG), DSA_NEG)
        mb = jnp.where(keyt > thr, 0.0, tie_mb)
        for g0 in range(0, nh, hg):
            s_g = _dot_nt(qn_ref[g0:g0 + hg].reshape(hg * tq, dh), k_t)
            ps, alphas = [], []
            for h in range(hg):
                rows = slice((g0 + h) * tq, (g0 + h + 1) * tq)
                s = s_g[h * tq:(h + 1) * tq] + (mb if near_idx is None else nb_ref[near_idx, g0 + h] + mb)
                m_old = m_ref[rows, :]
                m_new = jnp.maximum(m_old, jnp.max(s, axis=-1, keepdims=True))
                m_ref[rows, :] = m_new
                ps.append(jnp.exp2(s - twice(m_new)).astype(BF16))
                alphas.append(jnp.exp2(m_old - m_new))
            pv = jnp.dot(jnp.concatenate(ps, axis=0), v_t, preferred_element_type=F32)
            rows_g = slice(g0 * tq, (g0 + hg) * tq)
            acc_ref[rows_g, :] = acc_ref[rows_g, :] * twice(jnp.concatenate(alphas, axis=0)) + pv

    on_tile_edge = q0 == jd * tk
    has_prev_near = jnp.logical_and(on_tile_edge, jd >= 1)
    n_far = jnp.where(has_prev_near, jd - 1, jd)

    for_each_tile(n_far, lambda jt: attend_tile(jt, None))

    @pl.when(has_prev_near)
    def _():
        attend_tile(jd - 1, 2)

    attend_tile(jd, jnp.where(on_tile_edge, 0, 1))

    for h in range(nh):
        rows = slice(h * tq, (h + 1) * tq)
        o_ref[:, h * dh:(h + 1) * dh] = (acc_ref[rows, :dh] / acc_ref[rows, dh:]).astype(o_ref.dtype)


def _t5_bucket(rel):
    n = jnp.maximum(rel, 0)
    exact = T5_BUCKETS // 2
    nf = jnp.maximum(n, exact).astype(F32)
    large = exact + (jnp.log(nf / exact) / math.log(T5_MAX_DISTANCE / exact)
                     * (T5_BUCKETS - exact)).astype(jnp.int32)
    return jnp.where(n < exact, n, jnp.minimum(large, T5_BUCKETS - 1))


def _dsa_layer(x, mix_g, w_in, q_norm_g, k_norm_g, t5_table, w_o, bsz, seq):
    t, d = x.shape
    nh, dh, nih, di = DSA_HEADS, DSA_HEAD_DIM, IDX_HEADS, IDX_DIM
    tq, tk = DSA_TQ, DSA_TK
    assert seq % tk == 0 and tk == 2 * tq and dh == LANES and di <= LANES and nih <= LANES
    assert T5_MAX_DISTANCE <= tq
    n_sel = min(DSA_TOPK_MAX, seq // 4)
    o1, o2, o3, o4, o5 = nh * dh, nh * dh + dh, nh * dh + 2 * dh, nh * dh + 2 * dh + nih * di, nh * dh + 2 * dh + nih * di + di
    w_qi = jnp.pad(w_in[:, o3:o4].reshape(d, nih, di), ((0, 0), (0, 0), (0, LANES - di))).reshape(d, nih * LANES)
    w_ki = jnp.pad(w_in[:, o4:o5], ((0, 0), (0, LANES - di)))
    w_wi = jnp.pad(w_in[:, o5:o5 + nih] * (nih ** -0.5 * di ** -0.5), ((0, 0), (0, LANES - nih)))
    w = jnp.concatenate([w_in[:, :o3], w_qi, w_ki, w_wi], axis=1).astype(BF16)
    n = w.shape[1]
    proj = _norm_mm(x, mix_g, w, tn=_pick_tile(n, 1024)).reshape(bsz, seq, n)

    tm = min(256, seq)
    tok = lambda: pl.BlockSpec((None, tm, LANES), lambda b, i: (b, i, 0))
    qn, qi, kn, vb, kib, wis = pl.pallas_call(
        functools.partial(_dsa_prep_kernel, nh=nh, nih=nih),
        grid=(bsz, seq // tm),
        in_specs=[pl.BlockSpec((None, tm, n), lambda b, i: (b, i, 0)),
                  pl.BlockSpec((1, dh), lambda b, i: (0, 0)),
                  pl.BlockSpec((1, dh), lambda b, i: (0, 0))],
        out_specs=[pl.BlockSpec((None, nh, tm, dh), lambda b, i: (b, 0, i, 0)),
                   pl.BlockSpec((None, nih, tm, LANES), lambda b, i: (b, 0, i, 0)),
                   tok(), pl.BlockSpec((None, tm, dh + LANES), lambda b, i: (b, i, 0)), tok(), tok()],
        out_shape=[jax.ShapeDtypeStruct((bsz, nh, seq, dh), BF16),
                   jax.ShapeDtypeStruct((bsz, nih, seq, LANES), BF16),
                   jax.ShapeDtypeStruct((bsz, seq, dh), BF16),
                   jax.ShapeDtypeStruct((bsz, seq, dh + LANES), BF16),
                   jax.ShapeDtypeStruct((bsz, seq, LANES), BF16),
                   jax.ShapeDtypeStruct((bsz, seq, LANES), F32)],
        compiler_params=_cparams("parallel", "parallel"), name="dsa_prep")(
            proj, q_norm_g.reshape(1, dh), k_norm_g.reshape(1, dh))

    ii = jnp.arange(tq, dtype=jnp.int32)[:, None]
    jj = jnp.arange(tk, dtype=jnp.int32)[None, :]
    buckets = jnp.stack([_t5_bucket(off + ii - jj) for off in (0, tq, 2 * tq)])
    rel_table = (t5_table - t5_table[T5_BUCKETS - 1]).astype(F32)
    near = jnp.einsum("otkb,bh->ohtk", jax.nn.one_hot(buckets, T5_BUCKETS, dtype=F32), rel_table,
                      precision=lax.Precision.HIGHEST) * LOG2E

    seqblk = lambda: pl.BlockSpec((None, seq, LANES), lambda b, i: (b, 0, 0))
    out = pl.pallas_call(
        functools.partial(_dsa_kernel, nh=nh, nih=nih, n_sel=n_sel, idx_bits=int(seq).bit_length()),
        grid=(bsz, seq // tq),
        in_specs=[pl.BlockSpec((None, nh, tq, dh), lambda b, i: (b, 0, i, 0)),
                  pl.BlockSpec((None, nih, tq, LANES), lambda b, i: (b, 0, i, 0)),
                  pl.BlockSpec((None, tq, LANES), lambda b, i: (b, i, 0)),
                  seqblk(), pl.BlockSpec((None, seq, dh + LANES), lambda b, i: (b, 0, 0)), seqblk(),
                  pl.BlockSpec((3, nh, tq, tk), lambda b, i: (0, 0, 0, 0))],
        out_specs=pl.BlockSpec((None, tq, nh * dh), lambda b, i: (b, i, 0)),
        out_shape=jax.ShapeDtypeStruct((bsz, seq, nh * dh), BF16),
        scratch_shapes=[pltpu.VMEM((seq // tk, tq, tk), jnp.int32),
                        pltpu.VMEM((seq // tk, tk, tq), jnp.int16),
                        pltpu.VMEM((seq // tk, tk, tq), jnp.int16),
                        pltpu.VMEM((seq // tk, tk, tq), jnp.int16),
                        pltpu.VMEM((nih, tq, LANES), F32),
                        pltpu.VMEM((nh * tq, dh + LANES), F32),
                        pltpu.VMEM((nh * tq, LANES), F32)],
        compiler_params=_cparams("parallel", "arbitrary"), name="dsa_attn")(
            qn, qi, wis, kn, vb, kib, near)
    return _mm(out.reshape(t, nh * dh), w_o.astype(BF16), res=x)


def kernel(x, rwkv_mu, rwkv_w_rkv, rwkv_w0, rwkv_w1, rwkv_w2, rwkv_a0, rwkv_a1, rwkv_a2, rwkv_v0, rwkv_v1,
           rwkv_v2, rwkv_g1, rwkv_g2, rwkv_k_k, rwkv_k_a, rwkv_r_k, rwkv_lnx_g, rwkv_lnx_b, rwkv_w_o,
           mlstm_w_in, mlstm_b_if, mlstm_norm_g, mlstm_w_o, dsa_w_in, dsa_q_norm_g, dsa_k_norm_g, dsa_w_o,
           t5_bias, mix_norm_g, ffn_norm_g, ffn_w_gate, ffn_w_up, ffn_w_down):
    bsz, seq, d = x.shape
    depth = mix_norm_g.shape[0]
    h = x.reshape(bsz * seq, d)
    v_first = None
    for i in range(depth):
        kind, j = i % 3, i // 3
        if kind == 0:
            vres = None if j == 0 else (rwkv_v0[j - 1], rwkv_v1[j - 1], rwkv_v2[j - 1])
            p = (rwkv_mu[j], rwkv_w_rkv[j], rwkv_w0[j], rwkv_w1[j], rwkv_w2[j], rwkv_a0[j], rwkv_a1[j],
                 rwkv_a2[j], rwkv_g1[j], rwkv_g2[j], rwkv_k_k[j], rwkv_k_a[j], rwkv_r_k[j],
                 rwkv_lnx_g[j], rwkv_lnx_b[j], rwkv_w_o[j])
            h, v_first = _rwkv_layer(h, mix_norm_g[i], p, vres, v_first, bsz, seq)
        elif kind == 1:
            h = _mlstm_layer(h, mix_norm_g[i], mlstm_w_in[j], mlstm_b_if[j], mlstm_norm_g[j],
                             mlstm_w_o[j], bsz, seq)
        else:
            h = _dsa_layer(h, mix_norm_g[i], dsa_w_in[j], dsa_q_norm_g[j], dsa_k_norm_g[j], t5_bias,
                           dsa_w_o[j], bsz, seq)
        h = _ffn(h, ffn_norm_g[i], ffn_w_gate[i].astype(BF16), ffn_w_up[i].astype(BF16),
                 ffn_w_down[i].astype(BF16))
    return h.reshape(bsz, seq, d)
```

```python
import functools
import math

import jax
import jax.numpy as jnp
from jax import lax
from jax.experimental import pallas as pl
from jax.experimental.pallas import tpu as pltpu

F32 = jnp.float32
BF16 = jnp.bfloat16

V7X_VMEM_LIMIT_BYTES = 56 * 1024 * 1024
LANES = 128

RMS_EPS = 1e-6
RWKV_HEAD = 64
RWKV_DECAY_SCALE = math.exp(-0.5)
RWKV_GN_EPS = 64e-5
RWKV_CHUNK = 64
RWKV_CHAINS = 16
MLSTM_HEADS = 4
MLSTM_HEADS_PER_STEP = 2
MLSTM_GATE_CAP = 15.0
DSA_HEADS = 16
DSA_HEAD_DIM = 128
IDX_HEADS = 16
IDX_DIM = 64
DSA_TOPK_MAX = 256
T5_BUCKETS = 32
T5_MAX_DISTANCE = 128


def _cparams(*sem):
    return pltpu.CompilerParams(dimension_semantics=sem, vmem_limit_bytes=V7X_VMEM_LIMIT_BYTES)


def _dot(a, b):
    return jnp.dot(a.astype(BF16), b.astype(BF16), preferred_element_type=F32)


def _dot_nt(a, b):
    return lax.dot_general(a.astype(BF16), b.astype(BF16), (((1,), (1,)), ((), ())),
                           preferred_element_type=F32)


def _dot_tn(a, b):
    return lax.dot_general(a.astype(BF16), b.astype(BF16), (((0,), (0,)), ((), ())),
                           preferred_element_type=F32)


def _split3(x):
    hi = x.astype(BF16)
    r1 = x - hi.astype(F32)
    mid = r1.astype(BF16)
    lo = (r1 - mid.astype(F32)).astype(BF16)
    return hi, mid, lo


def _dot_exact_lhs(a_bf16, x):
    hi, mid, lo = _split3(x)
    d = lambda p: jnp.dot(a_bf16, p, preferred_element_type=F32)
    return d(hi) + (d(mid) + d(lo))


def _rms(x, g):
    ms = jnp.mean(x * x, axis=-1, keepdims=True)
    return x * lax.rsqrt(ms + RMS_EPS) * g


def _sigmoid(x):
    return 1.0 / (1.0 + jnp.exp(-x))


def _mm_kernel(a_ref, w_ref, o_ref):
    o_ref[...] = jnp.dot(a_ref[...], w_ref[...], preferred_element_type=F32).astype(o_ref.dtype)


def _mm_res_kernel(a_ref, w_ref, r_ref, o_ref):
    o_ref[...] = (r_ref[...] + jnp.dot(a_ref[...], w_ref[...], preferred_element_type=F32)).astype(o_ref.dtype)


def _mm(a, w, res=None, out_dtype=F32, tm=1024, tn=512):
    m, k = a.shape
    n = w.shape[1]
    tm, tn = min(tm, m), min(tn, n)
    assert m % tm == 0 and n % tn == 0
    in_specs = [pl.BlockSpec((tm, k), lambda i, j: (i, 0)),
                pl.BlockSpec((k, tn), lambda i, j: (0, j))]
    args = [a, w]
    kern = _mm_kernel
    if res is not None:
        in_specs.append(pl.BlockSpec((tm, tn), lambda i, j: (i, j)))
        args.append(res)
        kern = _mm_res_kernel
    return pl.pallas_call(
        kern, grid=(m // tm, n // tn), in_specs=in_specs,
        out_specs=pl.BlockSpec((tm, tn), lambda i, j: (i, j)),
        out_shape=jax.ShapeDtypeStruct((m, n), out_dtype),
        compiler_params=_cparams("parallel", "arbitrary"), name="mm")(*args)


def _bmm(a, w, out_dtype=F32, tm=1024, tn=512):
    g, m, k = a.shape
    n = w.shape[2]
    tm, tn = min(tm, m), min(tn, n)
    assert m % tm == 0 and n % tn == 0
    return pl.pallas_call(
        _mm_kernel, grid=(g, m // tm, n // tn),
        in_specs=[pl.BlockSpec((None, tm, k), lambda b, i, j: (b, i, 0)),
                  pl.BlockSpec((None, k, tn), lambda b, i, j: (b, 0, j))],
        out_specs=pl.BlockSpec((None, tm, tn), lambda b, i, j: (b, i, j)),
        out_shape=jax.ShapeDtypeStruct((g, m, n), out_dtype),
        compiler_params=_cparams("parallel", "parallel", "arbitrary"), name="bmm")(a, w)


def _norm_mm_kernel(x_ref, g_ref, w_ref, *rest):
    h_ref = rest[-1]
    has_side = len(rest) == 4
    o_ref = rest[1] if has_side else rest[0]

    @pl.when(pl.program_id(1) == 0)
    def _():
        h_ref[...] = _rms(x_ref[...], g_ref[...]).astype(BF16)
        if has_side:
            rest[2][...] = jnp.dot(h_ref[...], rest[0][...], preferred_element_type=F32)

    o_ref[...] = jnp.dot(h_ref[...], w_ref[...], preferred_element_type=F32).astype(o_ref.dtype)


def _norm_mm(x, g, w, out_dtype=F32, w_side=None, tm=1024, tn=512):
    m, k = x.shape
    n = w.shape[1]
    tm, tn = min(tm, m), min(tn, n)
    assert m % tm == 0 and n % tn == 0
    in_specs = [pl.BlockSpec((tm, k), lambda i, j: (i, 0)),
                pl.BlockSpec((1, k), lambda i, j: (0, 0)),
                pl.BlockSpec((k, tn), lambda i, j: (0, j))]
    out_specs = [pl.BlockSpec((tm, tn), lambda i, j: (i, j))]
    out_shape = [jax.ShapeDtypeStruct((m, n), out_dtype)]
    args = [x, g.reshape(1, k), w]
    if w_side is not None:
        ns = w_side.shape[1]
        in_specs.append(pl.BlockSpec((k, ns), lambda i, j: (0, 0)))
        out_specs.append(pl.BlockSpec((tm, ns), lambda i, j: (i, 0)))
        out_shape.append(jax.ShapeDtypeStruct((m, ns), F32))
        args.append(w_side)
    out = pl.pallas_call(
        _norm_mm_kernel, grid=(m // tm, n // tn), in_specs=in_specs, out_specs=out_specs, out_shape=out_shape,
        scratch_shapes=[pltpu.VMEM((tm, k), BF16)],
        compiler_params=_cparams("parallel", "arbitrary"), name="norm_mm")(*args)
    return out if w_side is not None else out[0]


def _ffn_kernel(x_ref, g_ref, wg_ref, wu_ref, wd_ref, o_ref, h_ref):
    @pl.when(pl.program_id(1) == 0)
    def _():
        x = x_ref[...]
        h_ref[...] = _rms(x, g_ref[...]).astype(BF16)
        o_ref[...] = x

    h = h_ref[...]
    gate = jnp.dot(h, wg_ref[...], preferred_element_type=F32)
    up = jnp.dot(h, wu_ref[...], preferred_element_type=F32)
    act = (gate * _sigmoid(gate) * up).astype(BF16)
    o_ref[...] += jnp.dot(act, wd_ref[...], preferred_element_type=F32)


def _ffn(x, g, wg, wu, wd, tm=512, tf=512):
    m, d = x.shape
    f = wg.shape[1]
    tm, tf = min(tm, m), min(tf, f)
    assert m % tm == 0 and f % tf == 0
    return pl.pallas_call(
        _ffn_kernel, grid=(m // tm, f // tf),
        in_specs=[pl.BlockSpec((tm, d), lambda i, j: (i, 0)),
                  pl.BlockSpec((1, d), lambda i, j: (0, 0)),
                  pl.BlockSpec((d, tf), lambda i, j: (0, j)),
                  pl.BlockSpec((d, tf), lambda i, j: (0, j)),
                  pl.BlockSpec((tf, d), lambda i, j: (j, 0))],
        out_specs=pl.BlockSpec((tm, d), lambda i, j: (i, 0)),
        out_shape=jax.ShapeDtypeStruct((m, d), F32),
        scratch_shapes=[pltpu.VMEM((tm, d), BF16)],
        compiler_params=_cparams("parallel", "arbitrary"), name="ffn")(x, g.reshape(1, d), wg, wu, wd)


def _rwkv_prep_kernel(*refs, seq, tm, has_v):
    if has_v:
        (x_ref, xp_ref, g_ref, mu_ref, w0_ref, w1_ref, w2_ref, a0_ref, a1_ref, a2_ref,
         g1_ref, g2_ref, v0_ref, v1_ref, v2_ref, xs_ref, lw_ref, a_ref, gate_ref, vg_ref) = refs
    else:
        (x_ref, xp_ref, g_ref, mu_ref, w0_ref, w1_ref, w2_ref, a0_ref, a1_ref, a2_ref,
         g1_ref, g2_ref, xs_ref, lw_ref, a_ref, gate_ref) = refs
    i = pl.program_id(0)
    gn = g_ref[...]
    h = _rms(x_ref[...], gn)
    hp = _rms(xp_ref[...], gn)
    seq_start = (i * tm) % seq == 0
    hp_row = jnp.where(seq_start, 0.0, hp[7:8, :])
    row = lax.broadcasted_iota(jnp.int32, (tm, 1), 0)
    h_prev = jnp.where(row == 0, hp_row, pltpu.roll(h, 1, 0))
    xx = h_prev - h
    mix = lambda n: h + xx * mu_ref[n:n + 1, :]
    xs_ref[0] = mix(0).astype(BF16)
    xs_ref[1] = mix(2).astype(BF16)
    xv = mix(3).astype(BF16)
    xs_ref[2] = xv
    lw_ref[...] = -RWKV_DECAY_SCALE * _sigmoid(
        w0_ref[...] + _dot(jnp.tanh(_dot(mix(1), w1_ref[...])), w2_ref[...]))
    a_ref[...] = _sigmoid(a0_ref[...] + _dot(_dot(mix(4), a1_ref[...]), a2_ref[...])).astype(a_ref.dtype)
    gate_ref[...] = _dot(_sigmoid(_dot(mix(5), g1_ref[...])), g2_ref[...]).astype(gate_ref.dtype)
    if has_v:
        vg_ref[...] = _sigmoid(v0_ref[...] + _dot(_dot(xv, v1_ref[...]), v2_ref[...])).astype(vg_ref.dtype)


def _pad_lora(w_in, w_out):
    r = w_in.shape[1]
    rp = -(-r // LANES) * LANES
    return (jnp.pad(w_in, ((0, 0), (0, rp - r))).astype(BF16),
            jnp.pad(w_out, ((0, rp - r), (0, 0))).astype(BF16))


def _rwkv_prep(x, norm_g, mu, w0, w1, w2, a0, a1, a2, g1, g2, vres, seq, tm=256):
    t, d = x.shape
    tm = min(tm, seq)
    assert t % tm == 0 and seq % tm == 0 and tm % 8 == 0
    has_v = vres is not None
    row = lambda v: v.reshape(1, d)
    full = lambda a: pl.BlockSpec(a.shape, lambda i: (0,) * a.ndim)
    w1p, w2p = _pad_lora(w1, w2)
    a1p, a2p = _pad_lora(a1, a2)
    g1p, g2p = _pad_lora(g1, g2)
    mu8 = jnp.pad(mu, ((0, 2), (0, 0)))
    params = [row(norm_g), mu8, row(w0), w1p, w2p, row(a0), a1p, a2p, g1p, g2p]
    if has_v:
        v1p, v2p = _pad_lora(vres[1], vres[2])
        params += [row(vres[0]), v1p, v2p]
    tile = pl.BlockSpec((tm, d), lambda i: (i, 0))
    in_specs = [tile, pl.BlockSpec((8, d), lambda i: (jnp.maximum(i * (tm // 8) - 1, 0), 0))]
    in_specs += [full(p) for p in params]
    n_gates = 3 if has_v else 2
    out_shape = ([jax.ShapeDtypeStruct((3, t, d), BF16), jax.ShapeDtypeStruct((t, d), F32)]
                 + [jax.ShapeDtypeStruct((t, d), BF16)] * n_gates)
    out_specs = [pl.BlockSpec((3, tm, d), lambda i: (0, i, 0))] + [tile] * (1 + n_gates)
    return pl.pallas_call(
        functools.partial(_rwkv_prep_kernel, seq=seq, tm=tm, has_v=has_v),
        grid=(t // tm,), in_specs=in_specs, out_specs=out_specs, out_shape=out_shape,
        compiler_params=_cparams("parallel"), name="rwkv_prep")(x, x, *params)


def _seg_sum(x, seg):
    w = x.shape[1]
    hi = x.astype(BF16)
    lo = (x - hi.astype(F32)).astype(BF16)
    d = lambda p, q: jnp.dot(p[:, q:q + 256], seg, preferred_element_type=F32)
    return jnp.concatenate([d(hi, q) + d(lo, q) for q in range(0, w, 256)], axis=1)


def _rwkv_scan_kernel(*refs, L, tc, has_v):
    (s_ref, a2_ref, r2_ref, b2_ref, k2_ref, v2_ref, pl_ref, phi_ref, psi_ref, theta_ref, yloc_ref,
     y_ref, bonus_ref, mab_ref, tinv_ref, mak_ref, mrb_ref, mrk_ref, av_ref) = refs[-19:]
    refs = refs[:-19]
    if has_v:
        (r_ref, k_ref, v_ref, lw_ref, a_ref, gate_ref, vf_ref, vg_ref,
         kk_ref, ka_ref, rk_ref, lng_ref, lnb_ref, o_ref) = refs
    else:
        (r_ref, k_ref, v_ref, lw_ref, a_ref, gate_ref,
         kk_ref, ka_ref, rk_ref, lng_ref, lnb_ref, o_ref) = refs
    W = r_ref.shape[-1]
    P2 = 2 * L
    npair = W // P2
    N = RWKV_HEAD

    @pl.when(pl.program_id(2) == 0)
    def _():
        s_ref[...] = jnp.zeros_like(s_ref)

    ri = lax.broadcasted_iota(jnp.int32, (P2, P2), 0)
    ci = lax.broadcasted_iota(jnp.int32, (P2, P2), 1)
    strict = ri > ci
    incl = ri >= ci
    eye = (ri == ci).astype(F32)
    levels = []
    s = 1
    while s < L:
        levels.append(((ri // s) % 2 == 1) & ((ci // s) == (ri // s) - 1))
        s *= 2
    head0 = lax.broadcasted_iota(jnp.int32, (L, P2), 1) < N
    sr = lax.broadcasted_iota(jnp.int32, (256, 256), 0) // N
    sc = lax.broadcasted_iota(jnp.int32, (256, 256), 1) // N
    seg = (sr == sc).astype(BF16)

    nc = tc // L
    tr = lax.broadcasted_iota(jnp.int32, (tc, tc), 0)
    tcol = lax.broadcasted_iota(jnp.int32, (tc, tc), 1)
    tri = jnp.where(tr >= tcol, jnp.where(tr // L == tcol // L, 1.0, 0.0), 0.0).astype(BF16)
    del tr, tcol

    r = r_ref[...]
    k = k_ref[...]
    v = v_ref[...]
    lw = lw_ref[...]
    a = a_ref[...].astype(F32)
    if has_v:
        v = v + (vf_ref[...] - v) * vg_ref[...].astype(F32)
    kk = k * kk_ref[...]
    kk = kk / jnp.maximum(jnp.sqrt(_seg_sum(kk * kk, seg)), 1e-12)
    kmod = k * (1.0 + (a - 1.0) * ka_ref[...])
    c = _dot_exact_lhs(tri, lw)
    enc = jnp.exp(-c)
    bonus_ref[...] = _seg_sum(r * kmod * rk_ref[...], seg) * v
    operands = (-kk * jnp.exp(c - lw), r * jnp.exp(c), kk * a * enc, kmod * enc, v)
    for z, z_ref in zip(operands, (a2_ref, r2_ref, b2_ref, k2_ref, v2_ref)):
        for ch in range(nc):
            for p in range(npair):
                zz = z[ch * L:(ch + 1) * L, P2 * p:P2 * (p + 1)]
                z_ref[ch, p, :L] = jnp.where(head0, zz, 0.0).astype(BF16)
                z_ref[ch, p, L:] = jnp.where(head0, 0.0, zz).astype(BF16)
    for ch in range(nc):
        pl_ref[ch] = jnp.broadcast_to(jnp.exp(c[(ch + 1) * L - 1:(ch + 1) * L, :]), (8, W))
    del r, k, v, lw, a, kk, kmod, c, enc, operands

    chains = [(ch, p) for ch in range(nc) for p in range(npair)]
    for g0 in range(0, len(chains), RWKV_CHAINS):
        group = chains[g0:g0 + RWKV_CHAINS]
        for c in group:
            g = _dot_nt(jnp.concatenate([a2_ref[c], r2_ref[c]], axis=0),
                        jnp.concatenate([b2_ref[c], k2_ref[c]], axis=0))
            m_ab = jnp.where(strict, g[:P2, :P2], 0.0)
            mab_ref[c] = m_ab
            tinv_ref[c] = eye + jnp.where(levels[0], m_ab, 0.0)
            mak_ref[c] = jnp.where(strict, g[:P2, P2:], 0.0).astype(BF16)
            mrb_ref[c] = jnp.where(incl, g[P2:, :P2], 0.0).astype(BF16)
            mrk_ref[c] = jnp.where(incl, g[P2:, P2:], 0.0).astype(BF16)
        for lv in levels[1:]:
            steps = [_dot(jnp.where(lv, mab_ref[c], 0.0), tinv_ref[c]).astype(BF16) for c in group]
            for c, step in zip(group, steps):
                t_cur = tinv_ref[c]
                tinv_ref[c] = t_cur + _dot(t_cur, step)
        mvs = [_dot(mak_ref[c], v2_ref[c]).astype(BF16) for c in group]
        for c, mv in zip(group, mvs):
            av_ref[c] = _dot(tinv_ref[c], jnp.concatenate([a2_ref[c], mv], axis=1)).astype(BF16)
        ths = [_dot(mrb_ref[c], av_ref[c]) for c in group]
        yls = [_dot(mrk_ref[c], v2_ref[c]) for c in group]
        for c, th, yl in zip(group, ths, yls):
            theta_ref[c] = (r2_ref[c].astype(F32) + th[:, :P2]).astype(BF16)
            yloc_ref[c] = th[:, P2:] + yl
        for c in group:
            av = av_ref[c]
            pp = _dot_tn(jnp.concatenate([av[:, P2:], av[:, :P2]], axis=1), b2_ref[c])
            p_last = pl_ref[c[0]][0:1, P2 * c[1]:P2 * (c[1] + 1)]
            phi_ref[c] = ((eye + pp[P2:]) * p_last).astype(BF16)
            psi_ref[c] = (pp[:P2] + _dot_tn(v2_ref[c], k2_ref[c])) * p_last

    for ch in range(nc):
        for p in range(npair):
            s0 = s_ref[p]
            yo = _dot_nt(theta_ref[ch, p], s0) + yloc_ref[ch, p]
            y_ref[ch * L:(ch + 1) * L, P2 * p:P2 * (p + 1)] = yo[:L] + yo[L:]
            s_ref[p] = _dot(s0, phi_ref[ch, p]) + psi_ref[ch, p]

    y = y_ref[...]
    mean = _seg_sum(y, seg) * (1.0 / N)
    yc = y - mean
    var = _seg_sum(yc * yc, seg) * (1.0 / N)
    yn = yc * lax.rsqrt(var + RWKV_GN_EPS) * lng_ref[...] + lnb_ref[...]
    o_ref[...] = ((yn + bonus_ref[...]) * gate_ref[...].astype(F32)).astype(o_ref.dtype)


def _rwkv_scan(rkv, lw, a, gate, v_first, vgate, k_k, k_a, r_k, lnx_g, lnx_b, bsz, seq, tc=256, wb=512):
    _, t, d = rkv.shape
    L = RWKV_CHUNK
    tc, wb = min(tc, seq), min(wb, d)
    assert seq % tc == 0 and tc % L == 0 and d % wb == 0 and wb % 256 == 0
    has_v = v_first is not None
    nc, p2, npair = tc // L, 2 * L, wb // (2 * L)
    rkv4 = rkv.reshape(3, bsz, seq, d)
    b3 = lambda z: z.reshape(bsz, seq, d)
    blk = pl.BlockSpec((None, tc, wb), lambda b, h, c: (b, c, h))
    rkv_spec = lambda n: pl.BlockSpec((None, None, tc, wb), lambda b, h, c: (n, b, c, h))
    prow = pl.BlockSpec((1, wb), lambda b, h, c: (0, h))
    args = [rkv4, rkv4, rkv4, b3(lw), b3(a), b3(gate)]
    in_specs = [rkv_spec(0), rkv_spec(1), rkv_spec(2), blk, blk, blk]
    if has_v:
        args += [v_first.reshape(3, bsz, seq, d), b3(vgate)]
        in_specs += [rkv_spec(2), blk]
    args += [z.reshape(1, d) for z in (k_k, k_a, r_k, lnx_g, lnx_b)]
    in_specs += [prow] * 5
    out = pl.pallas_call(
        functools.partial(_rwkv_scan_kernel, L=L, tc=tc, has_v=has_v),
        grid=(bsz, d // wb, seq // tc), in_specs=in_specs, out_specs=blk,
        out_shape=jax.ShapeDtypeStruct((bsz, seq, d), BF16),
        scratch_shapes=[pltpu.VMEM((npair, p2, p2), F32)]
        + [pltpu.VMEM((nc, npair, p2, p2), BF16)] * 5
        + [pltpu.VMEM((nc, 8, wb), F32),
           pltpu.VMEM((nc, npair, p2, p2), BF16), pltpu.VMEM((nc, npair, p2, p2), F32),
           pltpu.VMEM((nc, npair, p2, p2), BF16), pltpu.VMEM((nc, npair, p2, p2), F32),
           pltpu.VMEM((tc, wb), F32), pltpu.VMEM((tc, wb), F32),
           pltpu.VMEM((nc, npair, p2, p2), F32), pltpu.VMEM((nc, npair, p2, p2), F32),
           pltpu.VMEM((nc, npair, p2, p2), BF16), pltpu.VMEM((nc, npair, p2, p2), BF16),
           pltpu.VMEM((nc, npair, p2, p2), BF16),
           pltpu.VMEM((nc, npair, p2, 2 * p2), BF16)],
        compiler_params=_cparams("parallel", "parallel", "arbitrary"), name="rwkv_scan")(*args)
    return out.reshape(t, d)


def _rwkv_layer(x, norm_g, p, vres, v_first, bsz, seq):
    (mu, w_rkv, w0, w1, w2, a0, a1, a2, g1, g2, k_k, k_a, r_k, lnx_g, lnx_b, w_o) = p
    outs = _rwkv_prep(x, norm_g, mu, w0, w1, w2, a0, a1, a2, g1, g2, vres, seq)
    xs, lw, a, gate = outs[:4]
    vgate = outs[4] if vres is not None else None
    rkv = _bmm(xs, w_rkv.astype(BF16))
    o = _rwkv_scan(rkv, lw, a, gate, v_first if vres is not None else None, vgate,
                   k_k, k_a, r_k, lnx_g, lnx_b, bsz, seq)
    x = _mm(o, w_o.astype(BF16), res=x)
    return x, (rkv if vres is None else v_first)


def _pick_tile(n, cap):
    best = LANES
    for tile in range(LANES, cap + 1, LANES):
        if n % tile == 0:
            best = tile
    return best


def _mlstm_kernel(q_ref, k_ref, v_ref, o_ref, gt_ref, bias_ref, ng_ref, out_ref, c_ref, m_ref, *, nh, hps):
    lc = q_ref.shape[0]
    dk = q_ref.shape[1] // hps
    dv = v_ref.shape[1] // hps

    @pl.when(pl.program_id(2) == 0)
    def _():
        c_ref[...] = jnp.zeros_like(c_ref)
        m_ref[...] = jnp.zeros_like(m_ref)

    lane = lax.broadcasted_iota(jnp.int32, (lc, LANES), 1)
    z = gt_ref[...] + bias_ref[...]
    zc = MLSTM_GATE_CAP * jnp.tanh(z / MLSTM_GATE_CAP)
    lf_all = jnp.minimum(zc, 0.0) - jnp.log1p(jnp.exp(-jnp.abs(zc)))
    rr = lax.broadcasted_iota(jnp.int32, (lc, lc), 0)
    cc = lax.broadcasted_iota(jnp.int32, (lc, lc), 1)
    causal = rr >= cc
    bcum_all = _dot_exact_lhs(causal.astype(BF16), lf_all)
    comb = jnp.where(lane < nh, zc, bcum_all)
    er = lax.broadcasted_iota(jnp.int32, (8, LANES), 0)
    ec = lax.broadcasted_iota(jnp.int32, (8, LANES), 1)
    head_of_row = pl.program_id(1) * hps + er // 2
    sel = jnp.where(er < 2 * hps, jnp.where(ec == head_of_row + nh * (er % 2), 1.0, 0.0), 0.0).astype(BF16)
    hi, mid, lo = _split3(comb)
    tr = lambda p: lax.dot_general(sel, p, (((1,), (1,)), ((), ())), preferred_element_type=F32)
    rows = tr(hi) + (tr(mid) + tr(lo))
    ones_blk = (lane == 0).astype(BF16)

    for s in range(hps):
        h = pl.program_id(1) * hps + s
        li_col = jnp.sum(jnp.where(lane == h, comb, 0.0), axis=-1, keepdims=True)
        bc_col = jnp.sum(jnp.where(lane == h + nh, comb, 0.0), axis=-1, keepdims=True)
        li_row, bc_row = rows[2 * s:2 * s + 1], rows[2 * s + 1:2 * s + 2]
        m_st = m_ref[s, 0:1, 0:1]
        dmat = jnp.where(causal, bc_col - bc_row + li_row, -jnp.inf)
        inter = bc_col + m_st
        m_t = jnp.maximum(inter, jnp.max(dmat, axis=-1, keepdims=True))
        q = q_ref[:, s * dk:(s + 1) * dk] * (dk ** -0.5)
        k = k_ref[:, s * dk:(s + 1) * dk]
        sc = _dot_nt(q, k) * jnp.exp(dmat - m_t)
        w_inter = jnp.exp(inter - m_t)
        v_ext = jnp.concatenate([v_ref[:, s * dv:(s + 1) * dv].astype(BF16), ones_blk], axis=1)
        c_st = c_ref[s]
        nd = _dot(sc, v_ext) + w_inter * _dot(q, c_st)
        den = nd[:, dv:dv + 1]
        hc = nd[:, :dv] / jnp.maximum(jnp.abs(den), jnp.exp(-m_t))

        b_tot = bc_col[lc - 1:lc]
        log_wk = b_tot - bc_col + li_col
        m_new = jnp.maximum(b_tot + m_st, jnp.max(log_wk, axis=0, keepdims=True))
        c_ref[s] = jnp.exp(b_tot + m_st - m_new) * c_st + _dot_tn(k * jnp.exp(log_wk - m_new), v_ext)
        m_ref[s] = jnp.broadcast_to(m_new, m_ref.shape[1:])

        hn = hc * lax.rsqrt(jnp.mean(hc * hc, axis=-1, keepdims=True) + RMS_EPS) * ng_ref[:, s * dv:(s + 1) * dv]
        o_gate = _sigmoid(o_ref[:, s * dv:(s + 1) * dv].astype(F32))
        out_ref[:, s * dv:(s + 1) * dv] = (hn * o_gate).astype(out_ref.dtype)


def _mlstm_layer(x, mix_g, w_in, b_if, norm_g, w_o, bsz, seq, lc=256):
    t, d = x.shape
    nh = MLSTM_HEADS
    dk, dv = d // 2 // nh, d // nh
    nq = 2 * nh * dk + 2 * nh * dv
    assert w_in.shape[1] == nq + 2 * nh and dk % LANES == 0
    lc = min(lc, seq)
    assert seq % lc == 0
    w_gates = jnp.pad(w_in[:, nq:], ((0, 0), (0, LANES - 2 * nh))).astype(BF16)
    proj, gates = _norm_mm(x, mix_g, w_in[:, :nq].astype(BF16), out_dtype=BF16, w_side=w_gates,
                           tn=_pick_tile(nq, 1024))
    proj = proj.reshape(bsz, seq, nq)
    gates = gates.reshape(bsz, seq, LANES)
    bias = jnp.zeros((1, LANES), F32).at[0, :nh].set(b_if[0]).at[0, nh:2 * nh].set(b_if[1])
    hps = MLSTM_HEADS_PER_STEP
    assert nh % hps == 0
    ng = nh // hps
    out = pl.pallas_call(
        functools.partial(_mlstm_kernel, nh=nh, hps=hps),
        grid=(bsz, ng, seq // lc),
        in_specs=[pl.BlockSpec((None, lc, hps * dk), lambda b, h, c: (b, c, h)),
                  pl.BlockSpec((None, lc, hps * dk), lambda b, h, c: (b, c, ng + h)),
                  pl.BlockSpec((None, lc, hps * dv), lambda b, h, c: (b, c, ng + h)),
                  pl.BlockSpec((None, lc, hps * dv), lambda b, h, c: (b, c, 2 * ng + h)),
                  pl.BlockSpec((None, lc, LANES), lambda b, h, c: (b, c, 0)),
                  pl.BlockSpec((1, LANES), lambda b, h, c: (0, 0)),
                  pl.BlockSpec((1, hps * dv), lambda b, h, c: (0, h))],
        out_specs=pl.BlockSpec((None, lc, hps * dv), lambda b, h, c: (b, c, h)),
        out_shape=jax.ShapeDtypeStruct((bsz, seq, nh * dv), BF16),
        scratch_shapes=[pltpu.VMEM((hps, dk, dv + LANES), F32), pltpu.VMEM((hps, 8, LANES), F32)],
        compiler_params=_cparams("parallel", "parallel", "arbitrary"), name="mlstm")(
            proj, proj, proj, proj, gates, bias, norm_g.reshape(1, nh * dv))
    return _mm(out.reshape(t, nh * dv), w_o.astype(BF16), res=x)


DSA_TQ = 128
DSA_TK = 256
DSA_HEAD_GROUP = 4
DSA_TILES_PER_TRIP = 4
DSA_NEG = -1e30
KEY_NEG_INF = -2139095041
IDX_BIG = 2 ** 30
LOG2E = math.log2(math.e)


def _dsa_prep_kernel(p_ref, qg_ref, kg_ref, qn_ref, qi_ref, k_ref, v_ref, ki_ref, wi_ref, *, nh, nih):
    dh = DSA_HEAD_DIM
    qscale = dh ** -0.5 * LOG2E
    for h in range(nh):
        qn_ref[h] = (_rms(p_ref[:, h * dh:(h + 1) * dh], qg_ref[...]) * qscale).astype(BF16)
    base = nh * dh
    k_ref[...] = _rms(p_ref[:, base:base + dh], kg_ref[...]).astype(BF16)
    v_ref[:, :dh] = p_ref[:, base + dh:base + 2 * dh].astype(BF16)
    v_ref[:, dh:] = jnp.ones((v_ref.shape[0], LANES), BF16)
    base += 2 * dh
    for h in range(nih):
        qi_ref[h] = p_ref[:, base + h * LANES:base + (h + 1) * LANES].astype(BF16)
    base += nih * LANES
    ki_ref[...] = p_ref[:, base:base + LANES].astype(BF16)
    wi_ref[...] = p_ref[:, base + LANES:base + 2 * LANES]


def _dsa_kernel(qn_ref, qi_ref, wi_ref, k_ref, v_ref, ki_ref, nb_ref, o_ref,
                key_ref, hi_ref, lo_ref, lo2_ref, w_ref, acc_ref, m_ref, *, nh, nih, n_sel, idx_bits):
    tq, tk, dh = DSA_TQ, DSA_TK, DSA_HEAD_DIM
    hg = DSA_HEAD_GROUP
    q0 = pl.program_id(1) * tq
    jd = (q0 + tq - 1) // tk
    nt = jd + 1
    rowi = lax.broadcasted_iota(jnp.int32, (tq, tk), 0)
    coli = lax.broadcasted_iota(jnp.int32, (tq, tk), 1)
    ktile = lambda jt: pl.ds(pl.multiple_of(jt * tk, tk), tk)
    twice = lambda z: jnp.concatenate([z] * (tk // LANES), axis=1)

    wi = wi_ref[...]
    for h in range(nih):
        w_ref[h] = jnp.broadcast_to(wi[:, h:h + 1], (tq, LANES))

    def for_each_tile(n, body, per_trip=DSA_TILES_PER_TRIP):
        def trip(i, c):
            for u in range(per_trip):
                body(per_trip * i + u)
            return c
        lax.fori_loop(0, n // per_trip, trip, 0)
        done = (n // per_trip) * per_trip
        for u in range(per_trip - 1):
            @pl.when(n - done > u)
            def _():
                body(done + u)

    def score_tile(jt):
        ki_t = ki_ref[ktile(jt), :]
        score = jnp.zeros((tq, tk), F32)
        for g0 in range(0, nih, hg):
            lg = _dot_nt(qi_ref[g0:g0 + hg].reshape(hg * tq, LANES), ki_t)
            for h in range(hg):
                score = score + jnp.maximum(lg[h * tq:(h + 1) * tq], 0.0) * twice(w_ref[g0 + h])
        score = jnp.where(jt * tk + coli <= q0 + rowi, score, -jnp.inf)
        bits = pltpu.bitcast(score, jnp.int32)
        key = bits ^ ((bits >> 31) & 0x7FFFFFFF)
        key_ref[jt] = key
        key_t = pltpu.bitcast(jnp.transpose(pltpu.bitcast(key, F32)), jnp.int32)
        hi_ref[jt] = (key_t >> 16).astype(jnp.int16)
        lo_ref[jt] = ((key_t & 0xFFFF) - 32768).astype(jnp.int16)

    for_each_tile(nt, score_tile)

    unroll = DSA_TILES_PER_TRIP
    n_trips = (nt + unroll - 1) // unroll
    floor16 = jnp.full((tk, tq), -32768, jnp.int16)

    def pad_tiles(ref):
        for u in range(unroll - 1):
            @pl.when(n_trips * unroll - nt > u)
            def _():
                ref[nt + u] = floor16

    def count16(ref, cand):
        cb = jnp.broadcast_to(cand.astype(jnp.int16), (32, tq))

        def body(i, acc):
            for u in range(unroll):
                tile = ref[unroll * i + u]
                for r0 in range(0, tk, 32):
                    acc = acc + jnp.where(tile[r0:r0 + 32] >= cb, jnp.int16(1), jnp.int16(0))
            return acc
        acc = lax.fori_loop(0, n_trips, body, jnp.zeros((32, tq), jnp.int16))
        return jnp.sum(acc.astype(F32), axis=0, keepdims=True)

    pad_tiles(hi_ref)

    def kth_largest16(ref, kth):
        def bit(b, prefix):
            cand = prefix + jnp.left_shift(jnp.int32(1), 15 - b)
            return jnp.where(count16(ref, cand) >= kth, cand, prefix)
        return lax.fori_loop(0, 16, bit, jnp.full((1, tq), -32768, jnp.int32))

    hi_k = kth_largest16(hi_ref, n_sel)
    above = jnp.where(hi_k >= 32767, 0.0, count16(hi_ref, jnp.minimum(hi_k + 1, 32767)))
    hi_kb = jnp.broadcast_to(hi_k.astype(jnp.int16), (tk, tq))

    def bucket_tile(jt, c):
        lo2_ref[jt] = jnp.where(hi_ref[jt] == hi_kb, lo_ref[jt], floor16)
        return c

    lax.fori_loop(0, nt, bucket_tile, 0)
    pad_tiles(lo2_ref)
    lo_k = kth_largest16(lo2_ref, n_sel - above)
    thr_row = hi_k * 65536 + (lo_k + 32768)
    tw = pltpu.bitcast(jnp.transpose(pltpu.bitcast(jnp.broadcast_to(thr_row, (tq, tq)), F32)), jnp.int32)
    thr = tw[:, :1]

    def count(hit):
        def body(jt, acc):
            keyt = key_ref[jt]
            for c0 in range(0, tk, LANES):
                acc = acc + hit(keyt[:, c0:c0 + LANES], jt * tk + c0)
            return acc
        acc = lax.fori_loop(0, nt, body, jnp.zeros((tq, LANES), F32))
        return jnp.sum(acc, axis=-1, keepdims=True)

    wide = lambda col: jnp.broadcast_to(col, (tq, LANES))
    lane = lax.broadcasted_iota(jnp.int32, (tq, LANES), 1)

    c_gt = count(lambda kv, base: jnp.where(kv > tw, 1.0, 0.0))
    c_ge = count(lambda kv, base: jnp.where(kv >= tw, 1.0, 0.0))
    need = n_sel - c_gt
    c_eq = c_ge - c_gt

    def tie_search():
        def index_bit(b, jcur):
            cand = jcur + jnp.left_shift(jnp.int32(1), idx_bits - 1 - b)
            cw = wide(cand)
            f = count(lambda kv, base: jnp.where(kv == tw, jnp.where(base + lane < cw, 1.0, 0.0), 0.0))
            return jnp.where(f <= need, cand, jcur)
        return lax.fori_loop(0, idx_bits, index_bit, jnp.zeros((tq, 1), jnp.int32))

    excess = jnp.max(c_eq - need) > 0.0
    jc = lax.cond(excess, tie_search, lambda: jnp.full((tq, 1), IDX_BIG, jnp.int32))
    jc = jnp.where(c_eq > need, jc, IDX_BIG)
    jc = jnp.where(thr == KEY_NEG_INF, 0, jc)

    m_ref[...] = jnp.full_like(m_ref, DSA_NEG)
    acc_ref[...] = jnp.zeros_like(acc_ref)

    def attend_tile(jt, near_idx):
        k_t = k_ref[ktile(jt), :]
        v_t = v_ref[ktile(jt), :]
        keyt = key_ref[jt]
        tie_mb = jnp.where(keyt == thr, jnp.where(jt * tk + coli < jc, 0.0, DSA_NEG), DSA_NEG)
        mb = jnp.where(keyt > thr, 0.0, tie_mb)
        for g0 in range(0, nh, hg):
            s_g = _dot_nt(qn_ref[g0:g0 + hg].reshape(hg * tq, dh), k_t)
            ps, alphas = [], []
            for h in range(hg):
                rows = slice((g0 + h) * tq, (g0 + h + 1) * tq)
                s = s_g[h * tq:(h + 1) * tq] + (mb if near_idx is None else nb_ref[near_idx, g0 + h] + mb)
                m_old = m_ref[rows, :]
                m_new = jnp.maximum(m_old, jnp.max(s, axis=-1, keepdims=True))
                m_ref[rows, :] = m_new
                ps.append(jnp.exp2(s - twice(m_new)).astype(BF16))
                alphas.append(jnp.exp2(m_old - m_new))
            pv = jnp.dot(jnp.concatenate(ps, axis=0), v_t, preferred_element_type=F32)
            rows_g = slice(g0 * tq, (g0 + hg) * tq)
            acc_ref[rows_g, :] = acc_ref[rows_g, :] * twice(jnp.concatenate(alphas, axis=0)) + pv

    on_tile_edge = q0 == jd * tk
    has_prev_near = jnp.logical_and(on_tile_edge, jd >= 1)
    n_far = jnp.where(has_prev_near, jd - 1, jd)

    for_each_tile(n_far, lambda jt: attend_tile(jt, None))

    @pl.when(has_prev_near)
    def _():
        attend_tile(jd - 1, 2)
        attend_tile(jd, 0)

    @pl.when(jnp.logical_not(has_prev_near))
    def _():
        attend_tile(jd, jnp.where(on_tile_edge, 0, 1))

    for h in range(nh):
        rows = slice(h * tq, (h + 1) * tq)
        o_ref[:, h * dh:(h + 1) * dh] = (acc_ref[rows, :dh] / acc_ref[rows, dh:]).astype(o_ref.dtype)


def _t5_bucket(rel):
    n = jnp.maximum(rel, 0)
    exact = T5_BUCKETS // 2
    nf = jnp.maximum(n, exact).astype(F32)
    large = exact + (jnp.log(nf / exact) / math.log(T5_MAX_DISTANCE / exact)
                     * (T5_BUCKETS - exact)).astype(jnp.int32)
    return jnp.where(n < exact, n, jnp.minimum(large, T5_BUCKETS - 1))


def _dsa_layer(x, mix_g, w_in, q_norm_g, k_norm_g, t5_table, w_o, bsz, seq):
    t, d = x.shape
    nh, dh, nih, di = DSA_HEADS, DSA_HEAD_DIM, IDX_HEADS, IDX_DIM
    tq, tk = DSA_TQ, DSA_TK
    assert seq % (tk * DSA_TILES_PER_TRIP) == 0 and tk == 2 * tq and dh == LANES and di <= LANES and nih <= LANES
    assert T5_MAX_DISTANCE <= tq
    n_sel = min(DSA_TOPK_MAX, seq // 4)
    o1, o2, o3, o4, o5 = nh * dh, nh * dh + dh, nh * dh + 2 * dh, nh * dh + 2 * dh + nih * di, nh * dh + 2 * dh + nih * di + di
    w_qi = jnp.pad(w_in[:, o3:o4].reshape(d, nih, di), ((0, 0), (0, 0), (0, LANES - di))).reshape(d, nih * LANES)
    w_ki = jnp.pad(w_in[:, o4:o5], ((0, 0), (0, LANES - di)))
    w_wi = jnp.pad(w_in[:, o5:o5 + nih] * (nih ** -0.5 * di ** -0.5), ((0, 0), (0, LANES - nih)))
    w = jnp.concatenate([w_in[:, :o3], w_qi, w_ki, w_wi], axis=1).astype(BF16)
    n = w.shape[1]
    proj = _norm_mm(x, mix_g, w, tn=_pick_tile(n, 1024)).reshape(bsz, seq, n)

    tm = min(256, seq)
    tok = lambda: pl.BlockSpec((None, tm, LANES), lambda b, i: (b, i, 0))
    qn, qi, kn, vb, kib, wis = pl.pallas_call(
        functools.partial(_dsa_prep_kernel, nh=nh, nih=nih),
        grid=(bsz, seq // tm),
        in_specs=[pl.BlockSpec((None, tm, n), lambda b, i: (b, i, 0)),
                  pl.BlockSpec((1, dh), lambda b, i: (0, 0)),
                  pl.BlockSpec((1, dh), lambda b, i: (0, 0))],
        out_specs=[pl.BlockSpec((None, nh, tm, dh), lambda b, i: (b, 0, i, 0)),
                   pl.BlockSpec((None, nih, tm, LANES), lambda b, i: (b, 0, i, 0)),
                   tok(), pl.BlockSpec((None, tm, dh + LANES), lambda b, i: (b, i, 0)), tok(), tok()],
        out_shape=[jax.ShapeDtypeStruct((bsz, nh, seq, dh), BF16),
                   jax.ShapeDtypeStruct((bsz, nih, seq, LANES), BF16),
                   jax.ShapeDtypeStruct((bsz, seq, dh), BF16),
                   jax.ShapeDtypeStruct((bsz, seq, dh + LANES), BF16),
                   jax.ShapeDtypeStruct((bsz, seq, LANES), BF16),
                   jax.ShapeDtypeStruct((bsz, seq, LANES), F32)],
        compiler_params=_cparams("parallel", "parallel"), name="dsa_prep")(
            proj, q_norm_g.reshape(1, dh), k_norm_g.reshape(1, dh))

    ii = jnp.arange(tq, dtype=jnp.int32)[:, None]
    jj = jnp.arange(tk, dtype=jnp.int32)[None, :]
    buckets = jnp.stack([_t5_bucket(off + ii - jj) for off in (0, tq, 2 * tq)])
    rel_table = (t5_table - t5_table[T5_BUCKETS - 1]).astype(F32)
    near = jnp.einsum("otkb,bh->ohtk", jax.nn.one_hot(buckets, T5_BUCKETS, dtype=F32), rel_table,
                      precision=lax.Precision.HIGHEST) * LOG2E

    seqblk = lambda: pl.BlockSpec((None, seq, LANES), lambda b, i: (b, 0, 0))
    out = pl.pallas_call(
        functools.partial(_dsa_kernel, nh=nh, nih=nih, n_sel=n_sel, idx_bits=int(seq).bit_length()),
        grid=(bsz, seq // tq),
        in_specs=[pl.BlockSpec((None, nh, tq, dh), lambda b, i: (b, 0, i, 0)),
                  pl.BlockSpec((None, nih, tq, LANES), lambda b, i: (b, 0, i, 0)),
                  pl.BlockSpec((None, tq, LANES), lambda b, i: (b, i, 0)),
                  seqblk(), pl.BlockSpec((None, seq, dh + LANES), lambda b, i: (b, 0, 0)), seqblk(),
                  pl.BlockSpec((3, nh, tq, tk), lambda b, i: (0, 0, 0, 0))],
        out_specs=pl.BlockSpec((None, tq, nh * dh), lambda b, i: (b, i, 0)),
        out_shape=jax.ShapeDtypeStruct((bsz, seq, nh * dh), BF16),
        scratch_shapes=[pltpu.VMEM((seq // tk, tq, tk), jnp.int32),
                        pltpu.VMEM((seq // tk, tk, tq), jnp.int16),
                        pltpu.VMEM((seq // tk, tk, tq), jnp.int16),
                        pltpu.VMEM((seq // tk, tk, tq), jnp.int16),
                        pltpu.VMEM((nih, tq, LANES), F32),
                        pltpu.VMEM((nh * tq, dh + LANES), F32),
                        pltpu.VMEM((nh * tq, LANES), F32)],
        compiler_params=_cparams("parallel", "arbitrary"), name="dsa_attn")(
            qn, qi, wis, kn, vb, kib, near)
    return _mm(out.reshape(t, nh * dh), w_o.astype(BF16), res=x)


def kernel(x, rwkv_mu, rwkv_w_rkv, rwkv_w0, rwkv_w1, rwkv_w2, rwkv_a0, rwkv_a1, rwkv_a2, rwkv_v0, rwkv_v1,
           rwkv_v2, rwkv_g1, rwkv_g2, rwkv_k_k, rwkv_k_a, rwkv_r_k, rwkv_lnx_g, rwkv_lnx_b, rwkv_w_o,
           mlstm_w_in, mlstm_b_if, mlstm_norm_g, mlstm_w_o, dsa_w_in, dsa_q_norm_g, dsa_k_norm_g, dsa_w_o,
           t5_bias, mix_norm_g, ffn_norm_g, ffn_w_gate, ffn_w_up, ffn_w_down):
    bsz, seq, d = x.shape
    depth = mix_norm_g.shape[0]
    h = x.reshape(bsz * seq, d)
    v_first = None
    for i in range(depth):
        kind, j = i % 3, i // 3
        if kind == 0:
            vres = None if j == 0 else (rwkv_v0[j - 1], rwkv_v1[j - 1], rwkv_v2[j - 1])
            p = (rwkv_mu[j], rwkv_w_rkv[j], rwkv_w0[j], rwkv_w1[j], rwkv_w2[j], rwkv_a0[j], rwkv_a1[j],
                 rwkv_a2[j], rwkv_g1[j], rwkv_g2[j], rwkv_k_k[j], rwkv_k_a[j], rwkv_r_k[j],
                 rwkv_lnx_g[j], rwkv_lnx_b[j], rwkv_w_o[j])
            h, v_first = _rwkv_layer(h, mix_norm_g[i], p, vres, v_first, bsz, seq)
        elif kind == 1:
            h = _mlstm_layer(h, mix_norm_g[i], mlstm_w_in[j], mlstm_b_if[j], mlstm_norm_g[j],
                             mlstm_w_o[j], bsz, seq)
        else:
            h = _dsa_layer(h, mix_norm_g[i], dsa_w_in[j], dsa_q_norm_g[j], dsa_k_norm_g[j], t5_bias,
                           dsa_w_o[j], bsz, seq)
        h = _ffn(h, ffn_norm_g[i], ffn_w_gate[i].astype(BF16), ffn_w_up[i].astype(BF16),
                 ffn_w_down[i].astype(BF16))
    return h.reshape(bsz, seq, d)
```

```python
import functools
import math

import jax
import jax.numpy as jnp
from jax import lax
from jax.experimental import pallas as pl
from jax.experimental.pallas import tpu as pltpu

F32 = jnp.float32
BF16 = jnp.bfloat16

V7X_VMEM_LIMIT_BYTES = 56 * 1024 * 1024
LANES = 128

RMS_EPS = 1e-6
RWKV_HEAD = 64
RWKV_DECAY_SCALE = math.exp(-0.5)
RWKV_GN_EPS = 64e-5
RWKV_CHUNK = 64
RWKV_CHAINS = 16
MLSTM_HEADS = 4
MLSTM_HEADS_PER_STEP = 2
MLSTM_GATE_CAP = 15.0
DSA_HEADS = 16
DSA_HEAD_DIM = 128
IDX_HEADS = 16
IDX_DIM = 64
DSA_TOPK_MAX = 256
T5_BUCKETS = 32
T5_MAX_DISTANCE = 128


def _cparams(*sem):
    return pltpu.CompilerParams(dimension_semantics=sem, vmem_limit_bytes=V7X_VMEM_LIMIT_BYTES)


def _dot(a, b):
    return jnp.dot(a.astype(BF16), b.astype(BF16), preferred_element_type=F32)


def _dot_nt(a, b):
    return lax.dot_general(a.astype(BF16), b.astype(BF16), (((1,), (1,)), ((), ())),
                           preferred_element_type=F32)


def _dot_tn(a, b):
    return lax.dot_general(a.astype(BF16), b.astype(BF16), (((0,), (0,)), ((), ())),
                           preferred_element_type=F32)


def _split3(x):
    hi = x.astype(BF16)
    r1 = x - hi.astype(F32)
    mid = r1.astype(BF16)
    lo = (r1 - mid.astype(F32)).astype(BF16)
    return hi, mid, lo


def _dot_exact_lhs(a_bf16, x):
    hi, mid, lo = _split3(x)
    d = lambda p: jnp.dot(a_bf16, p, preferred_element_type=F32)
    return d(hi) + (d(mid) + d(lo))


def _rms(x, g):
    ms = jnp.mean(x * x, axis=-1, keepdims=True)
    return x * lax.rsqrt(ms + RMS_EPS) * g


def _sigmoid(x):
    return 1.0 / (1.0 + jnp.exp(-x))


def _mm_kernel(a_ref, w_ref, o_ref):
    o_ref[...] = jnp.dot(a_ref[...], w_ref[...], preferred_element_type=F32).astype(o_ref.dtype)


def _bmm(a, w, out_dtype=F32, tm=1024, tn=512):
    g, m, k = a.shape
    n = w.shape[2]
    tm, tn = min(tm, m), min(tn, n)
    assert m % tm == 0 and n % tn == 0
    return pl.pallas_call(
        _mm_kernel, grid=(g, m // tm, n // tn),
        in_specs=[pl.BlockSpec((None, tm, k), lambda b, i, j: (b, i, 0)),
                  pl.BlockSpec((None, k, tn), lambda b, i, j: (b, 0, j))],
        out_specs=pl.BlockSpec((None, tm, tn), lambda b, i, j: (b, i, j)),
        out_shape=jax.ShapeDtypeStruct((g, m, n), out_dtype),
        compiler_params=_cparams("parallel", "parallel", "arbitrary"), name="bmm")(a, w)


def _norm_mm_kernel(x_ref, g_ref, w_ref, *rest):
    h_ref = rest[-1]
    has_side = len(rest) == 4
    o_ref = rest[1] if has_side else rest[0]

    @pl.when(pl.program_id(1) == 0)
    def _():
        h_ref[...] = _rms(x_ref[...], g_ref[...]).astype(BF16)
        if has_side:
            rest[2][...] = jnp.dot(h_ref[...], rest[0][...], preferred_element_type=F32)

    o_ref[...] = jnp.dot(h_ref[...], w_ref[...], preferred_element_type=F32).astype(o_ref.dtype)


def _norm_mm(x, g, w, out_dtype=F32, w_side=None, tm=1024, tn=512):
    m, k = x.shape
    n = w.shape[1]
    tm, tn = min(tm, m), min(tn, n)
    assert m % tm == 0 and n % tn == 0
    in_specs = [pl.BlockSpec((tm, k), lambda i, j: (i, 0)),
                pl.BlockSpec((1, k), lambda i, j: (0, 0)),
                pl.BlockSpec((k, tn), lambda i, j: (0, j))]
    out_specs = [pl.BlockSpec((tm, tn), lambda i, j: (i, j))]
    out_shape = [jax.ShapeDtypeStruct((m, n), out_dtype)]
    args = [x, g.reshape(1, k), w]
    if w_side is not None:
        ns = w_side.shape[1]
        in_specs.append(pl.BlockSpec((k, ns), lambda i, j: (0, 0)))
        out_specs.append(pl.BlockSpec((tm, ns), lambda i, j: (i, 0)))
        out_shape.append(jax.ShapeDtypeStruct((m, ns), F32))
        args.append(w_side)
    out = pl.pallas_call(
        _norm_mm_kernel, grid=(m // tm, n // tn), in_specs=in_specs, out_specs=out_specs, out_shape=out_shape,
        scratch_shapes=[pltpu.VMEM((tm, k), BF16)],
        compiler_params=_cparams("parallel", "arbitrary"), name="norm_mm")(*args)
    return out if w_side is not None else out[0]


def _ffn_kernel(x_ref, y_ref, wo_ref, g_ref, wg_ref, wu_ref, wd_ref, o_ref, h_ref):
    @pl.when(pl.program_id(1) == 0)
    def _():
        x1 = x_ref[...] + jnp.dot(y_ref[...], wo_ref[...], preferred_element_type=F32)
        h_ref[...] = _rms(x1, g_ref[...]).astype(BF16)
        o_ref[...] = x1

    h = h_ref[...]
    gate = jnp.dot(h, wg_ref[...], preferred_element_type=F32)
    up = jnp.dot(h, wu_ref[...], preferred_element_type=F32)
    act = (gate * _sigmoid(gate) * up).astype(BF16)
    o_ref[...] += jnp.dot(act, wd_ref[...], preferred_element_type=F32)


def _proj_ffn(x, y, wo, g, wg, wu, wd, tm=512, tf=512):
    m, d = x.shape
    dy = y.shape[1]
    f = wg.shape[1]
    tm, tf = min(tm, m), min(tf, f)
    assert m % tm == 0 and f % tf == 0
    return pl.pallas_call(
        _ffn_kernel, grid=(m // tm, f // tf),
        in_specs=[pl.BlockSpec((tm, d), lambda i, j: (i, 0)),
                  pl.BlockSpec((tm, dy), lambda i, j: (i, 0)),
                  pl.BlockSpec((dy, d), lambda i, j: (0, 0), pipeline_mode=pl.Buffered(1)),
                  pl.BlockSpec((1, d), lambda i, j: (0, 0)),
                  pl.BlockSpec((d, tf), lambda i, j: (0, j)),
                  pl.BlockSpec((d, tf), lambda i, j: (0, j)),
                  pl.BlockSpec((tf, d), lambda i, j: (j, 0))],
        out_specs=pl.BlockSpec((tm, d), lambda i, j: (i, 0)),
        out_shape=jax.ShapeDtypeStruct((m, d), F32),
        scratch_shapes=[pltpu.VMEM((tm, d), BF16)],
        compiler_params=_cparams("parallel", "arbitrary"), name="proj_ffn")(
            x, y, wo, g.reshape(1, d), wg, wu, wd)


def _rwkv_prep_kernel(*refs, seq, tm, has_v):
    if has_v:
        (x_ref, xp_ref, g_ref, mu_ref, w0_ref, w1_ref, w2_ref, a0_ref, a1_ref, a2_ref,
         g1_ref, g2_ref, v0_ref, v1_ref, v2_ref, xs_ref, lw_ref, a_ref, gate_ref, vg_ref) = refs
    else:
        (x_ref, xp_ref, g_ref, mu_ref, w0_ref, w1_ref, w2_ref, a0_ref, a1_ref, a2_ref,
         g1_ref, g2_ref, xs_ref, lw_ref, a_ref, gate_ref) = refs
    i = pl.program_id(0)
    gn = g_ref[...]
    h = _rms(x_ref[...], gn)
    hp = _rms(xp_ref[...], gn)
    seq_start = (i * tm) % seq == 0
    hp_row = jnp.where(seq_start, 0.0, hp[7:8, :])
    row = lax.broadcasted_iota(jnp.int32, (tm, 1), 0)
    h_prev = jnp.where(row == 0, hp_row, pltpu.roll(h, 1, 0))
    xx = h_prev - h
    mix = lambda n: h + xx * mu_ref[n:n + 1, :]
    xs_ref[0] = mix(0).astype(BF16)
    xs_ref[1] = mix(2).astype(BF16)
    xv = mix(3).astype(BF16)
    xs_ref[2] = xv
    lw_ref[...] = -RWKV_DECAY_SCALE * _sigmoid(
        w0_ref[...] + _dot(jnp.tanh(_dot(mix(1), w1_ref[...])), w2_ref[...]))
    a_ref[...] = _sigmoid(a0_ref[...] + _dot(_dot(mix(4), a1_ref[...]), a2_ref[...])).astype(a_ref.dtype)
    gate_ref[...] = _dot(_sigmoid(_dot(mix(5), g1_ref[...])), g2_ref[...]).astype(gate_ref.dtype)
    if has_v:
        vg_ref[...] = _sigmoid(v0_ref[...] + _dot(_dot(xv, v1_ref[...]), v2_ref[...])).astype(vg_ref.dtype)


def _pad_lora(w_in, w_out):
    r = w_in.shape[1]
    rp = -(-r // LANES) * LANES
    return (jnp.pad(w_in, ((0, 0), (0, rp - r))).astype(BF16),
            jnp.pad(w_out, ((0, rp - r), (0, 0))).astype(BF16))


def _rwkv_prep(x, norm_g, mu, w0, w1, w2, a0, a1, a2, g1, g2, vres, seq, tm=256):
    t, d = x.shape
    tm = min(tm, seq)
    assert t % tm == 0 and seq % tm == 0 and tm % 8 == 0
    has_v = vres is not None
    row = lambda v: v.reshape(1, d)
    full = lambda a: pl.BlockSpec(a.shape, lambda i: (0,) * a.ndim)
    w1p, w2p = _pad_lora(w1, w2)
    a1p, a2p = _pad_lora(a1, a2)
    g1p, g2p = _pad_lora(g1, g2)
    mu8 = jnp.pad(mu, ((0, 2), (0, 0)))
    params = [row(norm_g), mu8, row(w0), w1p, w2p, row(a0), a1p, a2p, g1p, g2p]
    if has_v:
        v1p, v2p = _pad_lora(vres[1], vres[2])
        params += [row(vres[0]), v1p, v2p]
    tile = pl.BlockSpec((tm, d), lambda i: (i, 0))
    in_specs = [tile, pl.BlockSpec((8, d), lambda i: (jnp.maximum(i * (tm // 8) - 1, 0), 0))]
    in_specs += [full(p) for p in params]
    n_gates = 3 if has_v else 2
    out_shape = ([jax.ShapeDtypeStruct((3, t, d), BF16), jax.ShapeDtypeStruct((t, d), F32)]
                 + [jax.ShapeDtypeStruct((t, d), BF16)] * n_gates)
    out_specs = [pl.BlockSpec((3, tm, d), lambda i: (0, i, 0))] + [tile] * (1 + n_gates)
    return pl.pallas_call(
        functools.partial(_rwkv_prep_kernel, seq=seq, tm=tm, has_v=has_v),
        grid=(t // tm,), in_specs=in_specs, out_specs=out_specs, out_shape=out_shape,
        compiler_params=_cparams("parallel"), name="rwkv_prep")(x, x, *params)


def _seg_sum(x, seg):
    w = x.shape[1]
    hi = x.astype(BF16)
    lo = (x - hi.astype(F32)).astype(BF16)
    d = lambda p, q: jnp.dot(p[:, q:q + 256], seg, preferred_element_type=F32)
    return jnp.concatenate([d(hi, q) + d(lo, q) for q in range(0, w, 256)], axis=1)


def _rwkv_scan_kernel(*refs, L, tc, has_v):
    (s_ref, a2_ref, r2_ref, b2_ref, k2_ref, v2_ref, pl_ref, phi_ref, psi_ref, theta_ref, yloc_ref,
     y_ref, bonus_ref, mab_ref, tinv_ref, mak_ref, mrb_ref, mrk_ref, av_ref) = refs[-19:]
    refs = refs[:-19]
    if has_v:
        (r_ref, k_ref, v_ref, lw_ref, a_ref, gate_ref, vf_ref, vg_ref,
         kk_ref, ka_ref, rk_ref, lng_ref, lnb_ref, o_ref) = refs
    else:
        (r_ref, k_ref, v_ref, lw_ref, a_ref, gate_ref,
         kk_ref, ka_ref, rk_ref, lng_ref, lnb_ref, o_ref) = refs
    W = r_ref.shape[-1]
    P2 = 2 * L
    npair = W // P2
    N = RWKV_HEAD

    @pl.when(pl.program_id(2) == 0)
    def _():
        s_ref[...] = jnp.zeros_like(s_ref)

    ri = lax.broadcasted_iota(jnp.int32, (P2, P2), 0)
    ci = lax.broadcasted_iota(jnp.int32, (P2, P2), 1)
    strict = ri > ci
    incl = ri >= ci
    eye = (ri == ci).astype(F32)
    levels = []
    s = 1
    while s < L:
        levels.append(((ri // s) % 2 == 1) & ((ci // s) == (ri // s) - 1))
        s *= 2
    head0 = lax.broadcasted_iota(jnp.int32, (L, P2), 1) < N
    sr = lax.broadcasted_iota(jnp.int32, (256, 256), 0) // N
    sc = lax.broadcasted_iota(jnp.int32, (256, 256), 1) // N
    seg = (sr == sc).astype(BF16)

    nc = tc // L
    tr = lax.broadcasted_iota(jnp.int32, (tc, tc), 0)
    tcol = lax.broadcasted_iota(jnp.int32, (tc, tc), 1)
    tri = jnp.where(tr >= tcol, jnp.where(tr // L == tcol // L, 1.0, 0.0), 0.0).astype(BF16)
    del tr, tcol

    r = r_ref[...]
    k = k_ref[...]
    v = v_ref[...]
    lw = lw_ref[...]
    a = a_ref[...].astype(F32)
    if has_v:
        v = v + (vf_ref[...] - v) * vg_ref[...].astype(F32)
    kk = k * kk_ref[...]
    kk = kk / jnp.maximum(jnp.sqrt(_seg_sum(kk * kk, seg)), 1e-12)
    kmod = k * (1.0 + (a - 1.0) * ka_ref[...])
    c = _dot_exact_lhs(tri, lw)
    enc = jnp.exp(-c)
    bonus_ref[...] = _seg_sum(r * kmod * rk_ref[...], seg) * v
    operands = (-kk * jnp.exp(c - lw), r * jnp.exp(c), kk * a * enc, kmod * enc, v)
    for z, z_ref in zip(operands, (a2_ref, r2_ref, b2_ref, k2_ref, v2_ref)):
        for ch in range(nc):
            for p in range(npair):
                zz = z[ch * L:(ch + 1) * L, P2 * p:P2 * (p + 1)]
                z_ref[ch, p, :L] = jnp.where(head0, zz, 0.0).astype(BF16)
                z_ref[ch, p, L:] = jnp.where(head0, 0.0, zz).astype(BF16)
    for ch in range(nc):
        pl_ref[ch] = jnp.broadcast_to(jnp.exp(c[(ch + 1) * L - 1:(ch + 1) * L, :]), (8, W))
    del r, k, v, lw, a, kk, kmod, c, enc, operands

    chains = [(ch, p) for ch in range(nc) for p in range(npair)]
    for g0 in range(0, len(chains), RWKV_CHAINS):
        group = chains[g0:g0 + RWKV_CHAINS]
        for c in group:
            g = _dot_nt(jnp.concatenate([a2_ref[c], r2_ref[c]], axis=0),
                        jnp.concatenate([b2_ref[c], k2_ref[c]], axis=0))
            m_ab = jnp.where(strict, g[:P2, :P2], 0.0)
            mab_ref[c] = m_ab
            tinv_ref[c] = eye + jnp.where(levels[0], m_ab, 0.0)
            mak_ref[c] = jnp.where(strict, g[:P2, P2:], 0.0).astype(BF16)
            mrb_ref[c] = jnp.where(incl, g[P2:, :P2], 0.0).astype(BF16)
            mrk_ref[c] = jnp.where(incl, g[P2:, P2:], 0.0).astype(BF16)
        for lv in levels[1:]:
            steps = [_dot(jnp.where(lv, mab_ref[c], 0.0), tinv_ref[c]).astype(BF16) for c in group]
            for c, step in zip(group, steps):
                t_cur = tinv_ref[c]
                tinv_ref[c] = t_cur + _dot(t_cur, step)
        mvs = [_dot(mak_ref[c], v2_ref[c]).astype(BF16) for c in group]
        for c, mv in zip(group, mvs):
            av_ref[c] = _dot(tinv_ref[c], jnp.concatenate([a2_ref[c], mv], axis=1)).astype(BF16)
        ths = [_dot(mrb_ref[c], av_ref[c]) for c in group]
        yls = [_dot(mrk_ref[c], v2_ref[c]) for c in group]
        for c, th, yl in zip(group, ths, yls):
            theta_ref[c] = (r2_ref[c].astype(F32) + th[:, :P2]).astype(BF16)
            yloc_ref[c] = th[:, P2:] + yl
        for c in group:
            av = av_ref[c]
            pp = _dot_tn(jnp.concatenate([av[:, P2:], av[:, :P2]], axis=1), b2_ref[c])
            p_last = pl_ref[c[0]][0:1, P2 * c[1]:P2 * (c[1] + 1)]
            phi_ref[c] = ((eye + pp[P2:]) * p_last).astype(BF16)
            psi_ref[c] = (pp[:P2] + _dot_tn(v2_ref[c], k2_ref[c])) * p_last

    for ch in range(nc):
        for p in range(npair):
            s0 = s_ref[p]
            yo = _dot_nt(theta_ref[ch, p], s0) + yloc_ref[ch, p]
            y_ref[ch * L:(ch + 1) * L, P2 * p:P2 * (p + 1)] = yo[:L] + yo[L:]
            s_ref[p] = _dot(s0, phi_ref[ch, p]) + psi_ref[ch, p]

    y = y_ref[...]
    mean = _seg_sum(y, seg) * (1.0 / N)
    yc = y - mean
    var = _seg_sum(yc * yc, seg) * (1.0 / N)
    yn = yc * lax.rsqrt(var + RWKV_GN_EPS) * lng_ref[...] + lnb_ref[...]
    o_ref[...] = ((yn + bonus_ref[...]) * gate_ref[...].astype(F32)).astype(o_ref.dtype)


def _rwkv_scan(rkv, lw, a, gate, v_first, vgate, k_k, k_a, r_k, lnx_g, lnx_b, bsz, seq, tc=256, wb=512):
    _, t, d = rkv.shape
    L = RWKV_CHUNK
    tc, wb = min(tc, seq), min(wb, d)
    assert seq % tc == 0 and tc % L == 0 and d % wb == 0 and wb % 256 == 0
    has_v = v_first is not None
    nc, p2, npair = tc // L, 2 * L, wb // (2 * L)
    rkv4 = rkv.reshape(3, bsz, seq, d)
    b3 = lambda z: z.reshape(bsz, seq, d)
    blk = pl.BlockSpec((None, tc, wb), lambda b, h, c: (b, c, h))
    rkv_spec = lambda n: pl.BlockSpec((None, None, tc, wb), lambda b, h, c: (n, b, c, h))
    prow = pl.BlockSpec((1, wb), lambda b, h, c: (0, h))
    args = [rkv4, rkv4, rkv4, b3(lw), b3(a), b3(gate)]
    in_specs = [rkv_spec(0), rkv_spec(1), rkv_spec(2), blk, blk, blk]
    if has_v:
        args += [v_first.reshape(3, bsz, seq, d), b3(vgate)]
        in_specs += [rkv_spec(2), blk]
    args += [z.reshape(1, d) for z in (k_k, k_a, r_k, lnx_g, lnx_b)]
    in_specs += [prow] * 5
    out = pl.pallas_call(
        functools.partial(_rwkv_scan_kernel, L=L, tc=tc, has_v=has_v),
        grid=(bsz, d // wb, seq // tc), in_specs=in_specs, out_specs=blk,
        out_shape=jax.ShapeDtypeStruct((bsz, seq, d), BF16),
        scratch_shapes=[pltpu.VMEM((npair, p2, p2), F32)]
        + [pltpu.VMEM((nc, npair, p2, p2), BF16)] * 5
        + [pltpu.VMEM((nc, 8, wb), F32),
           pltpu.VMEM((nc, npair, p2, p2), BF16), pltpu.VMEM((nc, npair, p2, p2), F32),
           pltpu.VMEM((nc, npair, p2, p2), BF16), pltpu.VMEM((nc, npair, p2, p2), F32),
           pltpu.VMEM((tc, wb), F32), pltpu.VMEM((tc, wb), F32),
           pltpu.VMEM((nc, npair, p2, p2), F32), pltpu.VMEM((nc, npair, p2, p2), F32),
           pltpu.VMEM((nc, npair, p2, p2), BF16), pltpu.VMEM((nc, npair, p2, p2), BF16),
           pltpu.VMEM((nc, npair, p2, p2), BF16),
           pltpu.VMEM((nc, npair, p2, 2 * p2), BF16)],
        compiler_params=_cparams("parallel", "parallel", "arbitrary"), name="rwkv_scan")(*args)
    return out.reshape(t, d)


def _rwkv_layer(x, norm_g, p, vres, v_first, bsz, seq):
    (mu, w_rkv, w0, w1, w2, a0, a1, a2, g1, g2, k_k, k_a, r_k, lnx_g, lnx_b, w_o) = p
    outs = _rwkv_prep(x, norm_g, mu, w0, w1, w2, a0, a1, a2, g1, g2, vres, seq)
    xs, lw, a, gate = outs[:4]
    vgate = outs[4] if vres is not None else None
    rkv = _bmm(xs, w_rkv.astype(BF16))
    o = _rwkv_scan(rkv, lw, a, gate, v_first if vres is not None else None, vgate,
                   k_k, k_a, r_k, lnx_g, lnx_b, bsz, seq)
    return o, w_o.astype(BF16), (rkv if vres is None else v_first)


def _pick_tile(n, cap):
    best = LANES
    for tile in range(LANES, cap + 1, LANES):
        if n % tile == 0:
            best = tile
    return best


def _mlstm_kernel(q_ref, k_ref, v_ref, o_ref, gt_ref, bias_ref, ng_ref, out_ref, c_ref, m_ref, *, nh, hps):
    lc = q_ref.shape[0]
    dk = q_ref.shape[1] // hps
    dv = v_ref.shape[1] // hps

    @pl.when(pl.program_id(2) == 0)
    def _():
        c_ref[...] = jnp.zeros_like(c_ref)
        m_ref[...] = jnp.zeros_like(m_ref)

    lane = lax.broadcasted_iota(jnp.int32, (lc, LANES), 1)
    z = gt_ref[...] + bias_ref[...]
    zc = MLSTM_GATE_CAP * jnp.tanh(z / MLSTM_GATE_CAP)
    lf_all = jnp.minimum(zc, 0.0) - jnp.log1p(jnp.exp(-jnp.abs(zc)))
    rr = lax.broadcasted_iota(jnp.int32, (lc, lc), 0)
    cc = lax.broadcasted_iota(jnp.int32, (lc, lc), 1)
    causal = rr >= cc
    bcum_all = _dot_exact_lhs(causal.astype(BF16), lf_all)
    comb = jnp.where(lane < nh, zc, bcum_all)
    er = lax.broadcasted_iota(jnp.int32, (8, LANES), 0)
    ec = lax.broadcasted_iota(jnp.int32, (8, LANES), 1)
    head_of_row = pl.program_id(1) * hps + er // 2
    sel = jnp.where(er < 2 * hps, jnp.where(ec == head_of_row + nh * (er % 2), 1.0, 0.0), 0.0).astype(BF16)
    hi, mid, lo = _split3(comb)
    tr = lambda p: lax.dot_general(sel, p, (((1,), (1,)), ((), ())), preferred_element_type=F32)
    rows = tr(hi) + (tr(mid) + tr(lo))
    ones_blk = (lane == 0).astype(BF16)

    for s in range(hps):
        h = pl.program_id(1) * hps + s
        li_col = jnp.sum(jnp.where(lane == h, comb, 0.0), axis=-1, keepdims=True)
        bc_col = jnp.sum(jnp.where(lane == h + nh, comb, 0.0), axis=-1, keepdims=True)
        li_row, bc_row = rows[2 * s:2 * s + 1], rows[2 * s + 1:2 * s + 2]
        m_st = m_ref[s, 0:1, 0:1]
        dmat = jnp.where(causal, bc_col - bc_row + li_row, -jnp.inf)
        inter = bc_col + m_st
        m_t = jnp.maximum(inter, jnp.max(dmat, axis=-1, keepdims=True))
        q = q_ref[:, s * dk:(s + 1) * dk] * (dk ** -0.5)
        k = k_ref[:, s * dk:(s + 1) * dk]
        sc = _dot_nt(q, k) * jnp.exp(dmat - m_t)
        w_inter = jnp.exp(inter - m_t)
        v_ext = jnp.concatenate([v_ref[:, s * dv:(s + 1) * dv].astype(BF16), ones_blk], axis=1)
        c_st = c_ref[s]
        nd = _dot(sc, v_ext) + w_inter * _dot(q, c_st)
        den = nd[:, dv:dv + 1]
        hc = nd[:, :dv] / jnp.maximum(jnp.abs(den), jnp.exp(-m_t))

        b_tot = bc_col[lc - 1:lc]
        log_wk = b_tot - bc_col + li_col
        m_new = jnp.maximum(b_tot + m_st, jnp.max(log_wk, axis=0, keepdims=True))
        c_ref[s] = jnp.exp(b_tot + m_st - m_new) * c_st + _dot_tn(k * jnp.exp(log_wk - m_new), v_ext)
        m_ref[s] = jnp.broadcast_to(m_new, m_ref.shape[1:])

        hn = hc * lax.rsqrt(jnp.mean(hc * hc, axis=-1, keepdims=True) + RMS_EPS) * ng_ref[:, s * dv:(s + 1) * dv]
        o_gate = _sigmoid(o_ref[:, s * dv:(s + 1) * dv].astype(F32))
        out_ref[:, s * dv:(s + 1) * dv] = (hn * o_gate).astype(out_ref.dtype)


def _mlstm_layer(x, mix_g, w_in, b_if, norm_g, w_o, bsz, seq, lc=256):
    t, d = x.shape
    nh = MLSTM_HEADS
    dk, dv = d // 2 // nh, d // nh
    nq = 2 * nh * dk + 2 * nh * dv
    assert w_in.shape[1] == nq + 2 * nh and dk % LANES == 0
    lc = min(lc, seq)
    assert seq % lc == 0
    w_gates = jnp.pad(w_in[:, nq:], ((0, 0), (0, LANES - 2 * nh))).astype(BF16)
    proj, gates = _norm_mm(x, mix_g, w_in[:, :nq].astype(BF16), out_dtype=BF16, w_side=w_gates,
                           tn=_pick_tile(nq, 1024))
    proj = proj.reshape(bsz, seq, nq)
    gates = gates.reshape(bsz, seq, LANES)
    bias = jnp.zeros((1, LANES), F32).at[0, :nh].set(b_if[0]).at[0, nh:2 * nh].set(b_if[1])
    hps = MLSTM_HEADS_PER_STEP
    assert nh % hps == 0
    ng = nh // hps
    out = pl.pallas_call(
        functools.partial(_mlstm_kernel, nh=nh, hps=hps),
        grid=(bsz, ng, seq // lc),
        in_specs=[pl.BlockSpec((None, lc, hps * dk), lambda b, h, c: (b, c, h)),
                  pl.BlockSpec((None, lc, hps * dk), lambda b, h, c: (b, c, ng + h)),
                  pl.BlockSpec((None, lc, hps * dv), lambda b, h, c: (b, c, ng + h)),
                  pl.BlockSpec((None, lc, hps * dv), lambda b, h, c: (b, c, 2 * ng + h)),
                  pl.BlockSpec((None, lc, LANES), lambda b, h, c: (b, c, 0)),
                  pl.BlockSpec((1, LANES), lambda b, h, c: (0, 0)),
                  pl.BlockSpec((1, hps * dv), lambda b, h, c: (0, h))],
        out_specs=pl.BlockSpec((None, lc, hps * dv), lambda b, h, c: (b, c, h)),
        out_shape=jax.ShapeDtypeStruct((bsz, seq, nh * dv), BF16),
        scratch_shapes=[pltpu.VMEM((hps, dk, dv + LANES), F32), pltpu.VMEM((hps, 8, LANES), F32)],
        compiler_params=_cparams("parallel", "parallel", "arbitrary"), name="mlstm")(
            proj, proj, proj, proj, gates, bias, norm_g.reshape(1, nh * dv))
    return out.reshape(t, nh * dv), w_o.astype(BF16)


DSA_TQ = 128
DSA_TK = 256
DSA_HEAD_GROUP = 4
DSA_TILES_PER_TRIP = 4
DSA_NEG = -1e30
KEY_NEG_INF = -2139095041
IDX_BIG = 2 ** 30
LOG2E = math.log2(math.e)


def _dsa_prep_kernel(p_ref, qg_ref, kg_ref, qn_ref, qi_ref, k_ref, v_ref, ki_ref, wi_ref, *, nh, nih):
    dh = DSA_HEAD_DIM
    qscale = dh ** -0.5 * LOG2E
    for h in range(nh):
        qn_ref[h] = (_rms(p_ref[:, h * dh:(h + 1) * dh], qg_ref[...]) * qscale).astype(BF16)
    base = nh * dh
    k_ref[...] = _rms(p_ref[:, base:base + dh], kg_ref[...]).astype(BF16)
    v_ref[:, :dh] = p_ref[:, base + dh:base + 2 * dh].astype(BF16)
    v_ref[:, dh:] = jnp.ones((v_ref.shape[0], LANES), BF16)
    base += 2 * dh
    for h in range(nih):
        qi_ref[h] = p_ref[:, base + h * LANES:base + (h + 1) * LANES].astype(BF16)
    base += nih * LANES
    ki_ref[...] = p_ref[:, base:base + LANES].astype(BF16)
    wi_ref[...] = p_ref[:, base + LANES:base + 2 * LANES]


def _dsa_kernel(qn_ref, qi_ref, wi_ref, k_ref, v_ref, ki_ref, nb_ref, o_ref,
                key_ref, hi_ref, lo_ref, lo2_ref, w_ref, acc_ref, m_ref, *, nh, nih, n_sel, idx_bits):
    tq, tk, dh = DSA_TQ, DSA_TK, DSA_HEAD_DIM
    hg = DSA_HEAD_GROUP
    q0 = pl.program_id(1) * tq
    jd = (q0 + tq - 1) // tk
    nt = jd + 1
    rowi = lax.broadcasted_iota(jnp.int32, (tq, tk), 0)
    coli = lax.broadcasted_iota(jnp.int32, (tq, tk), 1)
    ktile = lambda jt: pl.ds(pl.multiple_of(jt * tk, tk), tk)
    twice = lambda z: jnp.concatenate([z] * (tk // LANES), axis=1)

    wi = wi_ref[...]
    for h in range(nih):
        w_ref[h] = jnp.broadcast_to(wi[:, h:h + 1], (tq, LANES))

    def for_each_tile(n, body, per_trip=DSA_TILES_PER_TRIP):
        def trip(i, c):
            for u in range(per_trip):
                body(per_trip * i + u)
            return c
        lax.fori_loop(0, n // per_trip, trip, 0)
        done = (n // per_trip) * per_trip
        for u in range(per_trip - 1):
            @pl.when(n - done > u)
            def _():
                body(done + u)

    def score_tile(jt):
        ki_t = ki_ref[ktile(jt), :]
        score = jnp.zeros((tq, tk), F32)
        for g0 in range(0, nih, hg):
            lg = _dot_nt(qi_ref[g0:g0 + hg].reshape(hg * tq, LANES), ki_t)
            for h in range(hg):
                score = score + jnp.maximum(lg[h * tq:(h + 1) * tq], 0.0) * twice(w_ref[g0 + h])
        score = jnp.where(jt * tk + coli <= q0 + rowi, score, -jnp.inf)
        bits = pltpu.bitcast(score, jnp.int32)
        key = bits ^ ((bits >> 31) & 0x7FFFFFFF)
        key_ref[jt] = key
        key_t = pltpu.bitcast(jnp.transpose(pltpu.bitcast(key, F32)), jnp.int32)
        hi_ref[jt] = (key_t >> 16).astype(jnp.int16)
        lo_ref[jt] = ((key_t & 0xFFFF) - 32768).astype(jnp.int16)

    for_each_tile(nt, score_tile)

    unroll = DSA_TILES_PER_TRIP
    n_trips = (nt + unroll - 1) // unroll
    floor16 = jnp.full((tk, tq), -32768, jnp.int16)

    def pad_tiles(ref):
        for u in range(unroll - 1):
            @pl.when(n_trips * unroll - nt > u)
            def _():
                ref[nt + u] = floor16

    def count16(ref, cand):
        cb = jnp.broadcast_to(cand.astype(jnp.int16), (32, tq))

        def body(i, acc):
            for u in range(unroll):
                tile = ref[unroll * i + u]
                for r0 in range(0, tk, 32):
                    acc = acc + jnp.where(tile[r0:r0 + 32] >= cb, jnp.int16(1), jnp.int16(0))
            return acc
        acc = lax.fori_loop(0, n_trips, body, jnp.zeros((32, tq), jnp.int16))
        return jnp.sum(acc.astype(F32), axis=0, keepdims=True)

    pad_tiles(hi_ref)

    def kth_largest16(ref, kth):
        def bit(b, prefix):
            cand = prefix + jnp.left_shift(jnp.int32(1), 15 - b)
            return jnp.where(count16(ref, cand) >= kth, cand, prefix)
        return lax.fori_loop(0, 16, bit, jnp.full((1, tq), -32768, jnp.int32))

    hi_k = kth_largest16(hi_ref, n_sel)
    above = jnp.where(hi_k >= 32767, 0.0, count16(hi_ref, jnp.minimum(hi_k + 1, 32767)))
    hi_kb = jnp.broadcast_to(hi_k.astype(jnp.int16), (tk, tq))

    def bucket_tile(jt, c):
        lo2_ref[jt] = jnp.where(hi_ref[jt] == hi_kb, lo_ref[jt], floor16)
        return c

    lax.fori_loop(0, nt, bucket_tile, 0)
    pad_tiles(lo2_ref)
    lo_k = kth_largest16(lo2_ref, n_sel - above)
    thr_row = hi_k * 65536 + (lo_k + 32768)
    tw = pltpu.bitcast(jnp.transpose(pltpu.bitcast(jnp.broadcast_to(thr_row, (tq, tq)), F32)), jnp.int32)
    thr = tw[:, :1]

    def count(hit):
        def body(jt, acc):
            keyt = key_ref[jt]
            for c0 in range(0, tk, LANES):
                acc = acc + hit(keyt[:, c0:c0 + LANES], jt * tk + c0)
            return acc
        acc = lax.fori_loop(0, nt, body, jnp.zeros((tq, LANES), F32))
        return jnp.sum(acc, axis=-1, keepdims=True)

    wide = lambda col: jnp.broadcast_to(col, (tq, LANES))
    lane = lax.broadcasted_iota(jnp.int32, (tq, LANES), 1)

    c_gt = count(lambda kv, base: jnp.where(kv > tw, 1.0, 0.0))
    c_ge = count(lambda kv, base: jnp.where(kv >= tw, 1.0, 0.0))
    need = n_sel - c_gt
    c_eq = c_ge - c_gt

    def tie_search():
        def index_bit(b, jcur):
            cand = jcur + jnp.left_shift(jnp.int32(1), idx_bits - 1 - b)
            cw = wide(cand)
            f = count(lambda kv, base: jnp.where(kv == tw, jnp.where(base + lane < cw, 1.0, 0.0), 0.0))
            return jnp.where(f <= need, cand, jcur)
        return lax.fori_loop(0, idx_bits, index_bit, jnp.zeros((tq, 1), jnp.int32))

    excess = jnp.max(c_eq - need) > 0.0
    jc = lax.cond(excess, tie_search, lambda: jnp.full((tq, 1), IDX_BIG, jnp.int32))
    jc = jnp.where(c_eq > need, jc, IDX_BIG)
    jc = jnp.where(thr == KEY_NEG_INF, 0, jc)

    m_ref[...] = jnp.full_like(m_ref, DSA_NEG)
    acc_ref[...] = jnp.zeros_like(acc_ref)

    def attend_tile(jt, near_idx):
        k_t = k_ref[ktile(jt), :]
        v_t = v_ref[ktile(jt), :]
        keyt = key_ref[jt]
        tie_mb = jnp.where(keyt == thr, jnp.where(jt * tk + coli < jc, 0.0, DSA_NEG), DSA_NEG)
        mb = jnp.where(keyt > thr, 0.0, tie_mb)
        for g0 in range(0, nh, hg):
            s_g = _dot_nt(qn_ref[g0:g0 + hg].reshape(hg * tq, dh), k_t)
            ps, alphas = [], []
            for h in range(hg):
                rows = slice((g0 + h) * tq, (g0 + h + 1) * tq)
                s = s_g[h * tq:(h + 1) * tq] + (mb if near_idx is None else nb_ref[near_idx, g0 + h] + mb)
                m_old = m_ref[rows, :]
                m_new = jnp.maximum(m_old, jnp.max(s, axis=-1, keepdims=True))
                m_ref[rows, :] = m_new
                ps.append(jnp.exp2(s - twice(m_new)).astype(BF16))
                alphas.append(jnp.exp2(m_old - m_new))
            pv = jnp.dot(jnp.concatenate(ps, axis=0), v_t, preferred_element_type=F32)
            rows_g = slice(g0 * tq, (g0 + hg) * tq)
            acc_ref[rows_g, :] = acc_ref[rows_g, :] * twice(jnp.concatenate(alphas, axis=0)) + pv

    on_tile_edge = q0 == jd * tk
    has_prev_near = jnp.logical_and(on_tile_edge, jd >= 1)
    n_far = jnp.where(has_prev_near, jd - 1, jd)

    for_each_tile(n_far, lambda jt: attend_tile(jt, None))

    @pl.when(has_prev_near)
    def _():
        attend_tile(jd - 1, 2)
        attend_tile(jd, 0)

    @pl.when(jnp.logical_not(has_prev_near))
    def _():
        attend_tile(jd, jnp.where(on_tile_edge, 0, 1))

    for h in range(nh):
        rows = slice(h * tq, (h + 1) * tq)
        o_ref[:, h * dh:(h + 1) * dh] = (acc_ref[rows, :dh] / acc_ref[rows, dh:]).astype(o_ref.dtype)


def _t5_bucket(rel):
    n = jnp.maximum(rel, 0)
    exact = T5_BUCKETS // 2
    nf = jnp.maximum(n, exact).astype(F32)
    large = exact + (jnp.log(nf / exact) / math.log(T5_MAX_DISTANCE / exact)
                     * (T5_BUCKETS - exact)).astype(jnp.int32)
    return jnp.where(n < exact, n, jnp.minimum(large, T5_BUCKETS - 1))


def _dsa_layer(x, mix_g, w_in, q_norm_g, k_norm_g, t5_table, w_o, bsz, seq):
    t, d = x.shape
    nh, dh, nih, di = DSA_HEADS, DSA_HEAD_DIM, IDX_HEADS, IDX_DIM
    tq, tk = DSA_TQ, DSA_TK
    assert seq % (tk * DSA_TILES_PER_TRIP) == 0 and tk == 2 * tq and dh == LANES and di <= LANES and nih <= LANES
    assert T5_MAX_DISTANCE <= tq
    n_sel = min(DSA_TOPK_MAX, seq // 4)
    o1, o2, o3, o4, o5 = nh * dh, nh * dh + dh, nh * dh + 2 * dh, nh * dh + 2 * dh + nih * di, nh * dh + 2 * dh + nih * di + di
    w_qi = jnp.pad(w_in[:, o3:o4].reshape(d, nih, di), ((0, 0), (0, 0), (0, LANES - di))).reshape(d, nih * LANES)
    w_ki = jnp.pad(w_in[:, o4:o5], ((0, 0), (0, LANES - di)))
    w_wi = jnp.pad(w_in[:, o5:o5 + nih] * (nih ** -0.5 * di ** -0.5), ((0, 0), (0, LANES - nih)))
    w = jnp.concatenate([w_in[:, :o3], w_qi, w_ki, w_wi], axis=1).astype(BF16)
    n = w.shape[1]
    proj = _norm_mm(x, mix_g, w, tn=_pick_tile(n, 1024)).reshape(bsz, seq, n)

    tm = min(256, seq)
    tok = lambda: pl.BlockSpec((None, tm, LANES), lambda b, i: (b, i, 0))
    qn, qi, kn, vb, kib, wis = pl.pallas_call(
        functools.partial(_dsa_prep_kernel, nh=nh, nih=nih),
        grid=(bsz, seq // tm),
        in_specs=[pl.BlockSpec((None, tm, n), lambda b, i: (b, i, 0)),
                  pl.BlockSpec((1, dh), lambda b, i: (0, 0)),
                  pl.BlockSpec((1, dh), lambda b, i: (0, 0))],
        out_specs=[pl.BlockSpec((None, nh, tm, dh), lambda b, i: (b, 0, i, 0)),
                   pl.BlockSpec((None, nih, tm, LANES), lambda b, i: (b, 0, i, 0)),
                   tok(), pl.BlockSpec((None, tm, dh + LANES), lambda b, i: (b, i, 0)), tok(), tok()],
        out_shape=[jax.ShapeDtypeStruct((bsz, nh, seq, dh), BF16),
                   jax.ShapeDtypeStruct((bsz, nih, seq, LANES), BF16),
                   jax.ShapeDtypeStruct((bsz, seq, dh), BF16),
                   jax.ShapeDtypeStruct((bsz, seq, dh + LANES), BF16),
                   jax.ShapeDtypeStruct((bsz, seq, LANES), BF16),
                   jax.ShapeDtypeStruct((bsz, seq, LANES), F32)],
        compiler_params=_cparams("parallel", "parallel"), name="dsa_prep")(
            proj, q_norm_g.reshape(1, dh), k_norm_g.reshape(1, dh))

    ii = jnp.arange(tq, dtype=jnp.int32)[:, None]
    jj = jnp.arange(tk, dtype=jnp.int32)[None, :]
    buckets = jnp.stack([_t5_bucket(off + ii - jj) for off in (0, tq, 2 * tq)])
    rel_table = (t5_table - t5_table[T5_BUCKETS - 1]).astype(F32)
    near = jnp.einsum("otkb,bh->ohtk", jax.nn.one_hot(buckets, T5_BUCKETS, dtype=F32), rel_table,
                      precision=lax.Precision.HIGHEST) * LOG2E

    seqblk = lambda: pl.BlockSpec((None, seq, LANES), lambda b, i: (b, 0, 0))
    out = pl.pallas_call(
        functools.partial(_dsa_kernel, nh=nh, nih=nih, n_sel=n_sel, idx_bits=int(seq).bit_length()),
        grid=(bsz, seq // tq),
        in_specs=[pl.BlockSpec((None, nh, tq, dh), lambda b, i: (b, 0, i, 0)),
                  pl.BlockSpec((None, nih, tq, LANES), lambda b, i: (b, 0, i, 0)),
                  pl.BlockSpec((None, tq, LANES), lambda b, i: (b, i, 0)),
                  seqblk(), pl.BlockSpec((None, seq, dh + LANES), lambda b, i: (b, 0, 0)), seqblk(),
                  pl.BlockSpec((3, nh, tq, tk), lambda b, i: (0, 0, 0, 0))],
        out_specs=pl.BlockSpec((None, tq, nh * dh), lambda b, i: (b, i, 0)),
        out_shape=jax.ShapeDtypeStruct((bsz, seq, nh * dh), BF16),
        scratch_shapes=[pltpu.VMEM((seq // tk, tq, tk), jnp.int32),
                        pltpu.VMEM((seq // tk, tk, tq), jnp.int16),
                        pltpu.VMEM((seq // tk, tk, tq), jnp.int16),
                        pltpu.VMEM((seq // tk, tk, tq), jnp.int16),
                        pltpu.VMEM((nih, tq, LANES), F32),
                        pltpu.VMEM((nh * tq, dh + LANES), F32),
                        pltpu.VMEM((nh * tq, LANES), F32)],
        compiler_params=_cparams("parallel", "arbitrary"), name="dsa_attn")(
            qn, qi, wis, kn, vb, kib, near)
    return out.reshape(t, nh * dh), w_o.astype(BF16)


def kernel(x, rwkv_mu, rwkv_w_rkv, rwkv_w0, rwkv_w1, rwkv_w2, rwkv_a0, rwkv_a1, rwkv_a2, rwkv_v0, rwkv_v1,
           rwkv_v2, rwkv_g1, rwkv_g2, rwkv_k_k, rwkv_k_a, rwkv_r_k, rwkv_lnx_g, rwkv_lnx_b, rwkv_w_o,
           mlstm_w_in, mlstm_b_if, mlstm_norm_g, mlstm_w_o, dsa_w_in, dsa_q_norm_g, dsa_k_norm_g, dsa_w_o,
           t5_bias, mix_norm_g, ffn_norm_g, ffn_w_gate, ffn_w_up, ffn_w_down):
    bsz, seq, d = x.shape
    depth = mix_norm_g.shape[0]
    h = x.reshape(bsz * seq, d)
    v_first = None
    for i in range(depth):
        kind, j = i % 3, i // 3
        if kind == 0:
            vres = None if j == 0 else (rwkv_v0[j - 1], rwkv_v1[j - 1], rwkv_v2[j - 1])
            p = (rwkv_mu[j], rwkv_w_rkv[j], rwkv_w0[j], rwkv_w1[j], rwkv_w2[j], rwkv_a0[j], rwkv_a1[j],
                 rwkv_a2[j], rwkv_g1[j], rwkv_g2[j], rwkv_k_k[j], rwkv_k_a[j], rwkv_r_k[j],
                 rwkv_lnx_g[j], rwkv_lnx_b[j], rwkv_w_o[j])
            y, w_o, v_first = _rwkv_layer(h, mix_norm_g[i], p, vres, v_first, bsz, seq)
        elif kind == 1:
            y, w_o = _mlstm_layer(h, mix_norm_g[i], mlstm_w_in[j], mlstm_b_if[j], mlstm_norm_g[j],
                                  mlstm_w_o[j], bsz, seq)
        else:
            y, w_o = _dsa_layer(h, mix_norm_g[i], dsa_w_in[j], dsa_q_norm_g[j], dsa_k_norm_g[j], t5_bias,
                                dsa_w_o[j], bsz, seq)
        h = _proj_ffn(h, y, w_o, ffn_norm_g[i], ffn_w_gate[i].astype(BF16), ffn_w_up[i].astype(BF16),
                      ffn_w_down[i].astype(BF16))
    return h.reshape(bsz, seq, d)
```

```python
import functools
import math

import jax
import jax.numpy as jnp
from jax import lax
from jax.experimental import pallas as pl
from jax.experimental.pallas import tpu as pltpu

F32 = jnp.float32
BF16 = jnp.bfloat16

V7X_VMEM_LIMIT_BYTES = 56 * 1024 * 1024
LANES = 128

RMS_EPS = 1e-6
RWKV_HEAD = 64
RWKV_DECAY_SCALE = math.exp(-0.5)
RWKV_GN_EPS = 64e-5
RWKV_CHUNK = 64
RWKV_CHAINS = 16
MLSTM_HEADS = 4
MLSTM_HEADS_PER_STEP = 2
MLSTM_GATE_CAP = 15.0
DSA_HEADS = 16
DSA_HEAD_DIM = 128
IDX_HEADS = 16
IDX_DIM = 64
DSA_TOPK_MAX = 256
T5_BUCKETS = 32
T5_MAX_DISTANCE = 128


def _cparams(*sem):
    return pltpu.CompilerParams(dimension_semantics=sem, vmem_limit_bytes=V7X_VMEM_LIMIT_BYTES)


def _dot(a, b):
    return jnp.dot(a.astype(BF16), b.astype(BF16), preferred_element_type=F32)


def _dot_nt(a, b):
    return lax.dot_general(a.astype(BF16), b.astype(BF16), (((1,), (1,)), ((), ())),
                           preferred_element_type=F32)


def _dot_tn(a, b):
    return lax.dot_general(a.astype(BF16), b.astype(BF16), (((0,), (0,)), ((), ())),
                           preferred_element_type=F32)


def _split3(x):
    hi = x.astype(BF16)
    r1 = x - hi.astype(F32)
    mid = r1.astype(BF16)
    lo = (r1 - mid.astype(F32)).astype(BF16)
    return hi, mid, lo


def _dot_exact_lhs(a_bf16, x):
    hi, mid, lo = _split3(x)
    d = lambda p: jnp.dot(a_bf16, p, preferred_element_type=F32)
    return d(hi) + (d(mid) + d(lo))


def _rms(x, g):
    ms = jnp.mean(x * x, axis=-1, keepdims=True)
    return x * lax.rsqrt(ms + RMS_EPS) * g


def _sigmoid(x):
    return 1.0 / (1.0 + jnp.exp(-x))


def _mm_kernel(a_ref, w_ref, o_ref):
    o_ref[...] = jnp.dot(a_ref[...], w_ref[...], preferred_element_type=F32).astype(o_ref.dtype)


def _bmm(a, w, out_dtype=F32, tm=1024, tn=512):
    g, m, k = a.shape
    n = w.shape[2]
    tm, tn = min(tm, m), min(tn, n)
    assert m % tm == 0 and n % tn == 0
    return pl.pallas_call(
        _mm_kernel, grid=(g, m // tm, n // tn),
        in_specs=[pl.BlockSpec((None, tm, k), lambda b, i, j: (b, i, 0)),
                  pl.BlockSpec((None, k, tn), lambda b, i, j: (b, 0, j))],
        out_specs=pl.BlockSpec((None, tm, tn), lambda b, i, j: (b, i, j)),
        out_shape=jax.ShapeDtypeStruct((g, m, n), out_dtype),
        compiler_params=_cparams("parallel", "parallel", "arbitrary"), name="bmm")(a, w)


def _norm_mm_kernel(x_ref, g_ref, w_ref, *rest):
    h_ref = rest[-1]
    has_side = len(rest) == 4
    o_ref = rest[1] if has_side else rest[0]

    @pl.when(pl.program_id(1) == 0)
    def _():
        h_ref[...] = _rms(x_ref[...], g_ref[...]).astype(BF16)
        if has_side:
            rest[2][...] = jnp.dot(h_ref[...], rest[0][...], preferred_element_type=F32)

    o_ref[...] = jnp.dot(h_ref[...], w_ref[...], preferred_element_type=F32).astype(o_ref.dtype)


def _norm_mm(x, g, w, out_dtype=F32, w_side=None, tm=1024, tn=512):
    m, k = x.shape
    n = w.shape[1]
    tm, tn = min(tm, m), min(tn, n)
    assert m % tm == 0 and n % tn == 0
    in_specs = [pl.BlockSpec((tm, k), lambda i, j: (i, 0)),
                pl.BlockSpec((1, k), lambda i, j: (0, 0)),
                pl.BlockSpec((k, tn), lambda i, j: (0, j))]
    out_specs = [pl.BlockSpec((tm, tn), lambda i, j: (i, j))]
    out_shape = [jax.ShapeDtypeStruct((m, n), out_dtype)]
    args = [x, g.reshape(1, k), w]
    if w_side is not None:
        ns = w_side.shape[1]
        in_specs.append(pl.BlockSpec((k, ns), lambda i, j: (0, 0)))
        out_specs.append(pl.BlockSpec((tm, ns), lambda i, j: (i, 0)))
        out_shape.append(jax.ShapeDtypeStruct((m, ns), F32))
        args.append(w_side)
    out = pl.pallas_call(
        _norm_mm_kernel, grid=(m // tm, n // tn), in_specs=in_specs, out_specs=out_specs, out_shape=out_shape,
        scratch_shapes=[pltpu.VMEM((tm, k), BF16)],
        compiler_params=_cparams("parallel", "arbitrary"), name="norm_mm")(*args)
    return out if w_side is not None else out[0]


def _ffn_kernel(x_ref, y_ref, wo_ref, g_ref, wg_ref, wu_ref, wd_ref, o_ref, h_ref):
    @pl.when(pl.program_id(1) == 0)
    def _():
        x1 = x_ref[...] + jnp.dot(y_ref[...], wo_ref[...], preferred_element_type=F32)
        h_ref[...] = _rms(x1, g_ref[...]).astype(BF16)
        o_ref[...] = x1

    h = h_ref[...]
    gate = jnp.dot(h, wg_ref[...], preferred_element_type=F32)
    up = jnp.dot(h, wu_ref[...], preferred_element_type=F32)
    act = (gate * _sigmoid(gate) * up).astype(BF16)
    o_ref[...] += jnp.dot(act, wd_ref[...], preferred_element_type=F32)


def _proj_ffn(x, y, wo, g, wg, wu, wd, tm=512, tf=512):
    m, d = x.shape
    dy = y.shape[1]
    f = wg.shape[1]
    tm, tf = min(tm, m), min(tf, f)
    assert m % tm == 0 and f % tf == 0
    return pl.pallas_call(
        _ffn_kernel, grid=(m // tm, f // tf),
        in_specs=[pl.BlockSpec((tm, d), lambda i, j: (i, 0)),
                  pl.BlockSpec((tm, dy), lambda i, j: (i, 0)),
                  pl.BlockSpec((dy, d), lambda i, j: (0, 0), pipeline_mode=pl.Buffered(1)),
                  pl.BlockSpec((1, d), lambda i, j: (0, 0)),
                  pl.BlockSpec((d, tf), lambda i, j: (0, j)),
                  pl.BlockSpec((d, tf), lambda i, j: (0, j)),
                  pl.BlockSpec((tf, d), lambda i, j: (j, 0))],
        out_specs=pl.BlockSpec((tm, d), lambda i, j: (i, 0)),
        out_shape=jax.ShapeDtypeStruct((m, d), F32),
        scratch_shapes=[pltpu.VMEM((tm, d), BF16)],
        compiler_params=_cparams("parallel", "arbitrary"), name="proj_ffn")(
            x, y, wo, g.reshape(1, d), wg, wu, wd)


def _rwkv_prep_kernel(*refs, seq, tm, has_v):
    if has_v:
        (x_ref, xp_ref, g_ref, mu_ref, w0_ref, w1_ref, w2_ref, a0_ref, a1_ref, a2_ref,
         g1_ref, g2_ref, v0_ref, v1_ref, v2_ref, xs_ref, lw_ref, a_ref, gate_ref, vg_ref) = refs
    else:
        (x_ref, xp_ref, g_ref, mu_ref, w0_ref, w1_ref, w2_ref, a0_ref, a1_ref, a2_ref,
         g1_ref, g2_ref, xs_ref, lw_ref, a_ref, gate_ref) = refs
    i = pl.program_id(0)
    gn = g_ref[...]
    h = _rms(x_ref[...], gn)
    hp = _rms(xp_ref[...], gn)
    seq_start = (i * tm) % seq == 0
    hp_row = jnp.where(seq_start, 0.0, hp[7:8, :])
    row = lax.broadcasted_iota(jnp.int32, (tm, 1), 0)
    h_prev = jnp.where(row == 0, hp_row, pltpu.roll(h, 1, 0))
    xx = h_prev - h
    mix = lambda n: h + xx * mu_ref[n:n + 1, :]
    xs_ref[0] = mix(0).astype(BF16)
    xs_ref[1] = mix(2).astype(BF16)
    xv = mix(3).astype(BF16)
    xs_ref[2] = xv
    lw_ref[...] = -RWKV_DECAY_SCALE * _sigmoid(
        w0_ref[...] + _dot(jnp.tanh(_dot(mix(1), w1_ref[...])), w2_ref[...]))
    a_ref[...] = _sigmoid(a0_ref[...] + _dot(_dot(mix(4), a1_ref[...]), a2_ref[...])).astype(a_ref.dtype)
    gate_ref[...] = _dot(_sigmoid(_dot(mix(5), g1_ref[...])), g2_ref[...]).astype(gate_ref.dtype)
    if has_v:
        vg_ref[...] = _sigmoid(v0_ref[...] + _dot(_dot(xv, v1_ref[...]), v2_ref[...])).astype(vg_ref.dtype)


def _pad_lora(w_in, w_out):
    r = w_in.shape[1]
    rp = -(-r // LANES) * LANES
    return (jnp.pad(w_in, ((0, 0), (0, rp - r))).astype(BF16),
            jnp.pad(w_out, ((0, rp - r), (0, 0))).astype(BF16))


def _rwkv_prep(x, norm_g, mu, w0, w1, w2, a0, a1, a2, g1, g2, vres, seq, tm=256):
    t, d = x.shape
    tm = min(tm, seq)
    assert t % tm == 0 and seq % tm == 0 and tm % 8 == 0
    has_v = vres is not None
    row = lambda v: v.reshape(1, d)
    full = lambda a: pl.BlockSpec(a.shape, lambda i: (0,) * a.ndim)
    w1p, w2p = _pad_lora(w1, w2)
    a1p, a2p = _pad_lora(a1, a2)
    g1p, g2p = _pad_lora(g1, g2)
    mu8 = jnp.pad(mu, ((0, 2), (0, 0)))
    params = [row(norm_g), mu8, row(w0), w1p, w2p, row(a0), a1p, a2p, g1p, g2p]
    if has_v:
        v1p, v2p = _pad_lora(vres[1], vres[2])
        params += [row(vres[0]), v1p, v2p]
    tile = pl.BlockSpec((tm, d), lambda i: (i, 0))
    in_specs = [tile, pl.BlockSpec((8, d), lambda i: (jnp.maximum(i * (tm // 8) - 1, 0), 0))]
    in_specs += [full(p) for p in params]
    n_gates = 3 if has_v else 2
    out_shape = ([jax.ShapeDtypeStruct((3, t, d), BF16), jax.ShapeDtypeStruct((t, d), F32)]
                 + [jax.ShapeDtypeStruct((t, d), BF16)] * n_gates)
    out_specs = [pl.BlockSpec((3, tm, d), lambda i: (0, i, 0))] + [tile] * (1 + n_gates)
    return pl.pallas_call(
        functools.partial(_rwkv_prep_kernel, seq=seq, tm=tm, has_v=has_v),
        grid=(t // tm,), in_specs=in_specs, out_specs=out_specs, out_shape=out_shape,
        compiler_params=_cparams("parallel"), name="rwkv_prep")(x, x, *params)


def _seg_sum(x, seg):
    w = x.shape[1]
    hi = x.astype(BF16)
    lo = (x - hi.astype(F32)).astype(BF16)
    d = lambda p, q: jnp.dot(p[:, q:q + 256], seg, preferred_element_type=F32)
    return jnp.concatenate([d(hi, q) + d(lo, q) for q in range(0, w, 256)], axis=1)


def _rwkv_scan_kernel(*refs, L, tc, has_v):
    (s_ref, a2_ref, r2_ref, b2_ref, k2_ref, v2_ref, pl_ref, phi_ref, psi_ref, theta_ref, yloc_ref,
     y_ref, bonus_ref, mab_ref, tinv_ref, mak_ref, mrb_ref, mrk_ref, av_ref) = refs[-19:]
    refs = refs[:-19]
    if has_v:
        (r_ref, k_ref, v_ref, lw_ref, a_ref, gate_ref, vf_ref, vg_ref,
         kk_ref, ka_ref, rk_ref, lng_ref, lnb_ref, o_ref) = refs
    else:
        (r_ref, k_ref, v_ref, lw_ref, a_ref, gate_ref,
         kk_ref, ka_ref, rk_ref, lng_ref, lnb_ref, o_ref) = refs
    W = r_ref.shape[-1]
    P2 = 2 * L
    npair = W // P2
    N = RWKV_HEAD

    @pl.when(pl.program_id(2) == 0)
    def _():
        s_ref[...] = jnp.zeros_like(s_ref)

    ri = lax.broadcasted_iota(jnp.int32, (P2, P2), 0)
    ci = lax.broadcasted_iota(jnp.int32, (P2, P2), 1)
    strict = ri > ci
    incl = ri >= ci
    eye = (ri == ci).astype(F32)
    levels = []
    s = 1
    while s < L:
        levels.append(((ri // s) % 2 == 1) & ((ci // s) == (ri // s) - 1))
        s *= 2
    head0 = lax.broadcasted_iota(jnp.int32, (L, P2), 1) < N
    sr = lax.broadcasted_iota(jnp.int32, (256, 256), 0) // N
    sc = lax.broadcasted_iota(jnp.int32, (256, 256), 1) // N
    seg = (sr == sc).astype(BF16)

    nc = tc // L
    tr = lax.broadcasted_iota(jnp.int32, (tc, tc), 0)
    tcol = lax.broadcasted_iota(jnp.int32, (tc, tc), 1)
    tri = jnp.where(tr >= tcol, jnp.where(tr // L == tcol // L, 1.0, 0.0), 0.0).astype(BF16)
    del tr, tcol

    r = r_ref[...]
    k = k_ref[...]
    v = v_ref[...]
    lw = lw_ref[...]
    a = a_ref[...].astype(F32)
    if has_v:
        v = v + (vf_ref[...] - v) * vg_ref[...].astype(F32)
    kk = k * kk_ref[...]
    kk = kk / jnp.maximum(jnp.sqrt(_seg_sum(kk * kk, seg)), 1e-12)
    kmod = k * (1.0 + (a - 1.0) * ka_ref[...])
    c = _dot_exact_lhs(tri, lw)
    enc = jnp.exp(-c)
    bonus_ref[...] = _seg_sum(r * kmod * rk_ref[...], seg) * v
    operands = (-kk * jnp.exp(c - lw), r * jnp.exp(c), kk * a * enc, kmod * enc, v)
    for z, z_ref in zip(operands, (a2_ref, r2_ref, b2_ref, k2_ref, v2_ref)):
        for ch in range(nc):
            for p in range(npair):
                zz = z[ch * L:(ch + 1) * L, P2 * p:P2 * (p + 1)]
                z_ref[ch, p, :L] = jnp.where(head0, zz, 0.0).astype(BF16)
                z_ref[ch, p, L:] = jnp.where(head0, 0.0, zz).astype(BF16)
    for ch in range(nc):
        pl_ref[ch] = jnp.broadcast_to(jnp.exp(c[(ch + 1) * L - 1:(ch + 1) * L, :]), (8, W))
    del r, k, v, lw, a, kk, kmod, c, enc, operands

    chains = [(ch, p) for ch in range(nc) for p in range(npair)]
    for g0 in range(0, len(chains), RWKV_CHAINS):
        group = chains[g0:g0 + RWKV_CHAINS]
        for c in group:
            g = _dot_nt(jnp.concatenate([a2_ref[c], r2_ref[c]], axis=0),
                        jnp.concatenate([b2_ref[c], k2_ref[c]], axis=0))
            m_ab = jnp.where(strict, g[:P2, :P2], 0.0)
            mab_ref[c] = m_ab
            tinv_ref[c] = eye + jnp.where(levels[0], m_ab, 0.0)
            mak_ref[c] = jnp.where(strict, g[:P2, P2:], 0.0).astype(BF16)
            mrb_ref[c] = jnp.where(incl, g[P2:, :P2], 0.0).astype(BF16)
            mrk_ref[c] = jnp.where(incl, g[P2:, P2:], 0.0).astype(BF16)
        for lv in levels[1:]:
            steps = [_dot(jnp.where(lv, mab_ref[c], 0.0), tinv_ref[c]).astype(BF16) for c in group]
            for c, step in zip(group, steps):
                t_cur = tinv_ref[c]
                tinv_ref[c] = t_cur + _dot(t_cur, step)
        mvs = [_dot(mak_ref[c], v2_ref[c]).astype(BF16) for c in group]
        for c, mv in zip(group, mvs):
            av_ref[c] = _dot(tinv_ref[c], jnp.concatenate([a2_ref[c], mv], axis=1)).astype(BF16)
        ths = [_dot(mrb_ref[c], av_ref[c]) for c in group]
        yls = [_dot(mrk_ref[c], v2_ref[c]) for c in group]
        for c, th, yl in zip(group, ths, yls):
            theta_ref[c] = (r2_ref[c].astype(F32) + th[:, :P2]).astype(BF16)
            yloc_ref[c] = th[:, P2:] + yl
        for c in group:
            av = av_ref[c]
            pp = _dot_tn(jnp.concatenate([av[:, P2:], av[:, :P2]], axis=1), b2_ref[c])
            p_last = pl_ref[c[0]][0:1, P2 * c[1]:P2 * (c[1] + 1)]
            phi_ref[c] = ((eye + pp[P2:]) * p_last).astype(BF16)
            psi_ref[c] = (pp[:P2] + _dot_tn(v2_ref[c], k2_ref[c])) * p_last

    for ch in range(nc):
        for p in range(npair):
            s0 = s_ref[p]
            yo = _dot_nt(theta_ref[ch, p], s0) + yloc_ref[ch, p]
            y_ref[ch * L:(ch + 1) * L, P2 * p:P2 * (p + 1)] = yo[:L] + yo[L:]
            s_ref[p] = _dot(s0, phi_ref[ch, p]) + psi_ref[ch, p]

    y = y_ref[...]
    mean = _seg_sum(y, seg) * (1.0 / N)
    yc = y - mean
    var = _seg_sum(yc * yc, seg) * (1.0 / N)
    yn = yc * lax.rsqrt(var + RWKV_GN_EPS) * lng_ref[...] + lnb_ref[...]
    o_ref[...] = ((yn + bonus_ref[...]) * gate_ref[...].astype(F32)).astype(o_ref.dtype)


def _rwkv_scan(rkv, lw, a, gate, v_first, vgate, k_k, k_a, r_k, lnx_g, lnx_b, bsz, seq, tc=256, wb=512):
    _, t, d = rkv.shape
    L = RWKV_CHUNK
    tc, wb = min(tc, seq), min(wb, d)
    assert seq % tc == 0 and tc % L == 0 and d % wb == 0 and wb % 256 == 0
    has_v = v_first is not None
    nc, p2, npair = tc // L, 2 * L, wb // (2 * L)
    rkv4 = rkv.reshape(3, bsz, seq, d)
    b3 = lambda z: z.reshape(bsz, seq, d)
    blk = pl.BlockSpec((None, tc, wb), lambda b, h, c: (b, c, h))
    rkv_spec = lambda n: pl.BlockSpec((None, None, tc, wb), lambda b, h, c: (n, b, c, h))
    prow = pl.BlockSpec((1, wb), lambda b, h, c: (0, h))
    args = [rkv4, rkv4, rkv4, b3(lw), b3(a), b3(gate)]
    in_specs = [rkv_spec(0), rkv_spec(1), rkv_spec(2), blk, blk, blk]
    if has_v:
        args += [v_first.reshape(3, bsz, seq, d), b3(vgate)]
        in_specs += [rkv_spec(2), blk]
    args += [z.reshape(1, d) for z in (k_k, k_a, r_k, lnx_g, lnx_b)]
    in_specs += [prow] * 5
    out = pl.pallas_call(
        functools.partial(_rwkv_scan_kernel, L=L, tc=tc, has_v=has_v),
        grid=(bsz, d // wb, seq // tc), in_specs=in_specs, out_specs=blk,
        out_shape=jax.ShapeDtypeStruct((bsz, seq, d), BF16),
        scratch_shapes=[pltpu.VMEM((npair, p2, p2), F32)]
        + [pltpu.VMEM((nc, npair, p2, p2), BF16)] * 5
        + [pltpu.VMEM((nc, 8, wb), F32),
           pltpu.VMEM((nc, npair, p2, p2), BF16), pltpu.VMEM((nc, npair, p2, p2), F32),
           pltpu.VMEM((nc, npair, p2, p2), BF16), pltpu.VMEM((nc, npair, p2, p2), F32),
           pltpu.VMEM((tc, wb), F32), pltpu.VMEM((tc, wb), F32),
           pltpu.VMEM((nc, npair, p2, p2), F32), pltpu.VMEM((nc, npair, p2, p2), F32),
           pltpu.VMEM((nc, npair, p2, p2), BF16), pltpu.VMEM((nc, npair, p2, p2), BF16),
           pltpu.VMEM((nc, npair, p2, p2), BF16),
           pltpu.VMEM((nc, npair, p2, 2 * p2), BF16)],
        compiler_params=_cparams("parallel", "parallel", "arbitrary"), name="rwkv_scan")(*args)
    return out.reshape(t, d)


def _rwkv_layer(x, norm_g, p, vres, v_first, bsz, seq):
    (mu, w_rkv, w0, w1, w2, a0, a1, a2, g1, g2, k_k, k_a, r_k, lnx_g, lnx_b, w_o) = p
    outs = _rwkv_prep(x, norm_g, mu, w0, w1, w2, a0, a1, a2, g1, g2, vres, seq)
    xs, lw, a, gate = outs[:4]
    vgate = outs[4] if vres is not None else None
    rkv = _bmm(xs, w_rkv.astype(BF16))
    o = _rwkv_scan(rkv, lw, a, gate, v_first if vres is not None else None, vgate,
                   k_k, k_a, r_k, lnx_g, lnx_b, bsz, seq)
    return o, w_o.astype(BF16), (rkv if vres is None else v_first)


def _pick_tile(n, cap):
    best = LANES
    for tile in range(LANES, cap + 1, LANES):
        if n % tile == 0:
            best = tile
    return best


def _mlstm_kernel(q_ref, k_ref, v_ref, o_ref, gt_ref, bias_ref, ng_ref, out_ref, c_ref, m_ref, *, nh, hps):
    lc = q_ref.shape[0]
    dk = q_ref.shape[1] // hps
    dv = v_ref.shape[1] // hps

    @pl.when(pl.program_id(2) == 0)
    def _():
        c_ref[...] = jnp.zeros_like(c_ref)
        m_ref[...] = jnp.zeros_like(m_ref)

    lane = lax.broadcasted_iota(jnp.int32, (lc, LANES), 1)
    z = gt_ref[...] + bias_ref[...]
    zc = MLSTM_GATE_CAP * jnp.tanh(z / MLSTM_GATE_CAP)
    lf_all = jnp.minimum(zc, 0.0) - jnp.log1p(jnp.exp(-jnp.abs(zc)))
    rr = lax.broadcasted_iota(jnp.int32, (lc, lc), 0)
    cc = lax.broadcasted_iota(jnp.int32, (lc, lc), 1)
    causal = rr >= cc
    bcum_all = _dot_exact_lhs(causal.astype(BF16), lf_all)
    comb = jnp.where(lane < nh, zc, bcum_all)
    er = lax.broadcasted_iota(jnp.int32, (8, LANES), 0)
    ec = lax.broadcasted_iota(jnp.int32, (8, LANES), 1)
    head_of_row = pl.program_id(1) * hps + er // 2
    sel = jnp.where(er < 2 * hps, jnp.where(ec == head_of_row + nh * (er % 2), 1.0, 0.0), 0.0).astype(BF16)
    hi, mid, lo = _split3(comb)
    tr = lambda p: lax.dot_general(sel, p, (((1,), (1,)), ((), ())), preferred_element_type=F32)
    rows = tr(hi) + (tr(mid) + tr(lo))
    ones_blk = (lane == 0).astype(BF16)

    for s in range(hps):
        h = pl.program_id(1) * hps + s
        li_col = jnp.sum(jnp.where(lane == h, comb, 0.0), axis=-1, keepdims=True)
        bc_col = jnp.sum(jnp.where(lane == h + nh, comb, 0.0), axis=-1, keepdims=True)
        li_row, bc_row = rows[2 * s:2 * s + 1], rows[2 * s + 1:2 * s + 2]
        m_st = m_ref[s, 0:1, 0:1]
        dmat = jnp.where(causal, bc_col - bc_row + li_row, -jnp.inf)
        inter = bc_col + m_st
        m_t = jnp.maximum(inter, jnp.max(dmat, axis=-1, keepdims=True))
        q = q_ref[:, s * dk:(s + 1) * dk] * (dk ** -0.5)
        k = k_ref[:, s * dk:(s + 1) * dk]
        sc = _dot_nt(q, k) * jnp.exp(dmat - m_t)
        w_inter = jnp.exp(inter - m_t)
        v_ext = jnp.concatenate([v_ref[:, s * dv:(s + 1) * dv].astype(BF16), ones_blk], axis=1)
        c_st = c_ref[s]
        nd = _dot(sc, v_ext) + w_inter * _dot(q, c_st)
        den = nd[:, dv:dv + 1]
        hc = nd[:, :dv] / jnp.maximum(jnp.abs(den), jnp.exp(-m_t))

        b_tot = bc_col[lc - 1:lc]
        log_wk = b_tot - bc_col + li_col
        m_new = jnp.maximum(b_tot + m_st, jnp.max(log_wk, axis=0, keepdims=True))
        c_ref[s] = jnp.exp(b_tot + m_st - m_new) * c_st + _dot_tn(k * jnp.exp(log_wk - m_new), v_ext)
        m_ref[s] = jnp.broadcast_to(m_new, m_ref.shape[1:])

        hn = hc * lax.rsqrt(jnp.mean(hc * hc, axis=-1, keepdims=True) + RMS_EPS) * ng_ref[:, s * dv:(s + 1) * dv]
        o_gate = _sigmoid(o_ref[:, s * dv:(s + 1) * dv].astype(F32))
        out_ref[:, s * dv:(s + 1) * dv] = (hn * o_gate).astype(out_ref.dtype)


def _mlstm_layer(x, mix_g, w_in, b_if, norm_g, w_o, bsz, seq, lc=256):
    t, d = x.shape
    nh = MLSTM_HEADS
    dk, dv = d // 2 // nh, d // nh
    nq = 2 * nh * dk + 2 * nh * dv
    assert w_in.shape[1] == nq + 2 * nh and dk % LANES == 0
    lc = min(lc, seq)
    assert seq % lc == 0
    w_gates = jnp.pad(w_in[:, nq:], ((0, 0), (0, LANES - 2 * nh))).astype(BF16)
    proj, gates = _norm_mm(x, mix_g, w_in[:, :nq].astype(BF16), out_dtype=BF16, w_side=w_gates,
                           tn=_pick_tile(nq, 1024))
    proj = proj.reshape(bsz, seq, nq)
    gates = gates.reshape(bsz, seq, LANES)
    bias = jnp.zeros((1, LANES), F32).at[0, :nh].set(b_if[0]).at[0, nh:2 * nh].set(b_if[1])
    hps = MLSTM_HEADS_PER_STEP
    assert nh % hps == 0
    ng = nh // hps
    out = pl.pallas_call(
        functools.partial(_mlstm_kernel, nh=nh, hps=hps),
        grid=(bsz, ng, seq // lc),
        in_specs=[pl.BlockSpec((None, lc, hps * dk), lambda b, h, c: (b, c, h)),
                  pl.BlockSpec((None, lc, hps * dk), lambda b, h, c: (b, c, ng + h)),
                  pl.BlockSpec((None, lc, hps * dv), lambda b, h, c: (b, c, ng + h)),
                  pl.BlockSpec((None, lc, hps * dv), lambda b, h, c: (b, c, 2 * ng + h)),
                  pl.BlockSpec((None, lc, LANES), lambda b, h, c: (b, c, 0)),
                  pl.BlockSpec((1, LANES), lambda b, h, c: (0, 0)),
                  pl.BlockSpec((1, hps * dv), lambda b, h, c: (0, h))],
        out_specs=pl.BlockSpec((None, lc, hps * dv), lambda b, h, c: (b, c, h)),
        out_shape=jax.ShapeDtypeStruct((bsz, seq, nh * dv), BF16),
        scratch_shapes=[pltpu.VMEM((hps, dk, dv + LANES), F32), pltpu.VMEM((hps, 8, LANES), F32)],
        compiler_params=_cparams("parallel", "parallel", "arbitrary"), name="mlstm")(
            proj, proj, proj, proj, gates, bias, norm_g.reshape(1, nh * dv))
    return out.reshape(t, nh * dv), w_o.astype(BF16)


DSA_TQ = 128
DSA_TK = 256
DSA_HEAD_GROUP = 4
DSA_TILES_PER_TRIP = 4
DSA_NEG = -1e30
KEY_NEG_INF = -2139095041
IDX_BIG = 2 ** 30
LOG2E = math.log2(math.e)


def _dsa_proj_kernel(x_ref, g_ref, w_ref, qg_ref, kg_ref, qn_ref, qi_ref, k_ref, v_ref, ki_ref, wi_ref, h_ref,
                     *, nh, nih):
    dh = DSA_HEAD_DIM
    j = pl.program_id(2)
    nq, nqi = nh // 2, nih // 2

    @pl.when(j == 0)
    def _():
        h_ref[...] = _rms(x_ref[...], g_ref[...]).astype(BF16)

    y = jnp.dot(h_ref[...], w_ref[...], preferred_element_type=F32)
    first, second = y[:, :dh], y[:, dh:]

    @pl.when(j < nq)
    def _():
        qscale = dh ** -0.5 * LOG2E
        qn_ref[0] = (_rms(first, qg_ref[...]) * qscale).astype(BF16)
        qn_ref[1] = (_rms(second, qg_ref[...]) * qscale).astype(BF16)

    @pl.when(j == nq)
    def _():
        k_ref[...] = _rms(first, kg_ref[...]).astype(BF16)
        v_ref[:, :dh] = second.astype(BF16)
        v_ref[:, dh:] = jnp.ones((v_ref.shape[0], LANES), BF16)

    @pl.when(jnp.logical_and(j > nq, j <= nq + nqi))
    def _():
        qi_ref[0] = first.astype(BF16)
        qi_ref[1] = second.astype(BF16)

    @pl.when(j == nq + nqi + 1)
    def _():
        ki_ref[...] = first.astype(BF16)
        wi_ref[...] = second


def _dsa_kernel(qn_ref, qi_ref, wi_ref, k_ref, v_ref, ki_ref, nb_ref, o_ref,
                key_ref, hi_ref, lo_ref, lo2_ref, w_ref, acc_ref, m_ref, *, nh, nih, n_sel, idx_bits):
    tq, tk, dh = DSA_TQ, DSA_TK, DSA_HEAD_DIM
    hg = DSA_HEAD_GROUP
    q0 = pl.program_id(1) * tq
    jd = (q0 + tq - 1) // tk
    nt = jd + 1
    rowi = lax.broadcasted_iota(jnp.int32, (tq, tk), 0)
    coli = lax.broadcasted_iota(jnp.int32, (tq, tk), 1)
    ktile = lambda jt: pl.ds(pl.multiple_of(jt * tk, tk), tk)
    twice = lambda z: jnp.concatenate([z] * (tk // LANES), axis=1)

    wi = wi_ref[...]
    for h in range(nih):
        w_ref[h] = jnp.broadcast_to(wi[:, h:h + 1], (tq, LANES))

    def for_each_tile(n, body, per_trip=DSA_TILES_PER_TRIP):
        def trip(i, c):
            for u in range(per_trip):
                body(per_trip * i + u)
            return c
        lax.fori_loop(0, n // per_trip, trip, 0)
        done = (n // per_trip) * per_trip
        for u in range(per_trip - 1):
            @pl.when(n - done > u)
            def _():
                body(done + u)

    def score_tile(jt):
        ki_t = ki_ref[ktile(jt), :]
        score = jnp.zeros((tq, tk), F32)
        for g0 in range(0, nih, hg):
            lg = _dot_nt(qi_ref[g0:g0 + hg].reshape(hg * tq, LANES), ki_t)
            for h in range(hg):
                score = score + jnp.maximum(lg[h * tq:(h + 1) * tq], 0.0) * twice(w_ref[g0 + h])
        score = jnp.where(jt * tk + coli <= q0 + rowi, score, -jnp.inf)
        bits = pltpu.bitcast(score, jnp.int32)
        key = bits ^ ((bits >> 31) & 0x7FFFFFFF)
        key_ref[jt] = key
        key_t = pltpu.bitcast(jnp.transpose(pltpu.bitcast(key, F32)), jnp.int32)
        hi_ref[jt] = (key_t >> 16).astype(jnp.int16)
        lo_ref[jt] = ((key_t & 0xFFFF) - 32768).astype(jnp.int16)

    for_each_tile(nt, score_tile)

    unroll = DSA_TILES_PER_TRIP
    n_trips = (nt + unroll - 1) // unroll
    floor16 = jnp.full((tk, tq), -32768, jnp.int16)

    def pad_tiles(ref):
        for u in range(unroll - 1):
            @pl.when(n_trips * unroll - nt > u)
            def _():
                ref[nt + u] = floor16

    def count16(ref, cand):
        cb = jnp.broadcast_to(cand.astype(jnp.int16), (32, tq))

        def body(i, acc):
            for u in range(unroll):
                tile = ref[unroll * i + u]
                for r0 in range(0, tk, 32):
                    acc = acc + jnp.where(tile[r0:r0 + 32] >= cb, jnp.int16(1), jnp.int16(0))
            return acc
        acc = lax.fori_loop(0, n_trips, body, jnp.zeros((32, tq), jnp.int16))
        return jnp.sum(acc.astype(F32), axis=0, keepdims=True)

    pad_tiles(hi_ref)

    def kth_largest16(ref, kth):
        def bit(b, prefix):
            cand = prefix + jnp.left_shift(jnp.int32(1), 15 - b)
            return jnp.where(count16(ref, cand) >= kth, cand, prefix)
        return lax.fori_loop(0, 16, bit, jnp.full((1, tq), -32768, jnp.int32))

    hi_k = kth_largest16(hi_ref, n_sel)
    above = jnp.where(hi_k >= 32767, 0.0, count16(hi_ref, jnp.minimum(hi_k + 1, 32767)))
    hi_kb = jnp.broadcast_to(hi_k.astype(jnp.int16), (tk, tq))

    def bucket_tile(jt, c):
        lo2_ref[jt] = jnp.where(hi_ref[jt] == hi_kb, lo_ref[jt], floor16)
        return c

    lax.fori_loop(0, nt, bucket_tile, 0)
    pad_tiles(lo2_ref)
    lo_k = kth_largest16(lo2_ref, n_sel - above)
    thr_row = hi_k * 65536 + (lo_k + 32768)
    tw = pltpu.bitcast(jnp.transpose(pltpu.bitcast(jnp.broadcast_to(thr_row, (tq, tq)), F32)), jnp.int32)
    thr = tw[:, :1]

    def count(hit):
        def body(jt, acc):
            keyt = key_ref[jt]
            for c0 in range(0, tk, LANES):
                acc = acc + hit(keyt[:, c0:c0 + LANES], jt * tk + c0)
            return acc
        acc = lax.fori_loop(0, nt, body, jnp.zeros((tq, LANES), F32))
        return jnp.sum(acc, axis=-1, keepdims=True)

    wide = lambda col: jnp.broadcast_to(col, (tq, LANES))
    lane = lax.broadcasted_iota(jnp.int32, (tq, LANES), 1)

    c_gt = count(lambda kv, base: jnp.where(kv > tw, 1.0, 0.0))
    c_ge = count(lambda kv, base: jnp.where(kv >= tw, 1.0, 0.0))
    need = n_sel - c_gt
    c_eq = c_ge - c_gt

    def tie_search():
        def index_bit(b, jcur):
            cand = jcur + jnp.left_shift(jnp.int32(1), idx_bits - 1 - b)
            cw = wide(cand)
            f = count(lambda kv, base: jnp.where(kv == tw, jnp.where(base + lane < cw, 1.0, 0.0), 0.0))
            return jnp.where(f <= need, cand, jcur)
        return lax.fori_loop(0, idx_bits, index_bit, jnp.zeros((tq, 1), jnp.int32))

    excess = jnp.max(c_eq - need) > 0.0
    jc = lax.cond(excess, tie_search, lambda: jnp.full((tq, 1), IDX_BIG, jnp.int32))
    jc = jnp.where(c_eq > need, jc, IDX_BIG)
    jc = jnp.where(thr == KEY_NEG_INF, 0, jc)

    m_ref[...] = jnp.full_like(m_ref, DSA_NEG)
    acc_ref[...] = jnp.zeros_like(acc_ref)

    def attend_tile(jt, near_idx):
        k_t = k_ref[ktile(jt), :]
        v_t = v_ref[ktile(jt), :]
        keyt = key_ref[jt]
        tie_mb = jnp.where(keyt == thr, jnp.where(jt * tk + coli < jc, 0.0, DSA_NEG), DSA_NEG)
        mb = jnp.where(keyt > thr, 0.0, tie_mb)
        for g0 in range(0, nh, hg):
            s_g = _dot_nt(qn_ref[g0:g0 + hg].reshape(hg * tq, dh), k_t)
            ps, alphas = [], []
            for h in range(hg):
                rows = slice((g0 + h) * tq, (g0 + h + 1) * tq)
                s = s_g[h * tq:(h + 1) * tq] + (mb if near_idx is None else nb_ref[near_idx, g0 + h] + mb)
                m_old = m_ref[rows, :]
                m_new = jnp.maximum(m_old, jnp.max(s, axis=-1, keepdims=True))
                m_ref[rows, :] = m_new
                ps.append(jnp.exp2(s - twice(m_new)).astype(BF16))
                alphas.append(jnp.exp2(m_old - m_new))
            pv = jnp.dot(jnp.concatenate(ps, axis=0), v_t, preferred_element_type=F32)
            rows_g = slice(g0 * tq, (g0 + hg) * tq)
            acc_ref[rows_g, :] = acc_ref[rows_g, :] * twice(jnp.concatenate(alphas, axis=0)) + pv

    on_tile_edge = q0 == jd * tk
    has_prev_near = jnp.logical_and(on_tile_edge, jd >= 1)
    n_far = jnp.where(has_prev_near, jd - 1, jd)

    for_each_tile(n_far, lambda jt: attend_tile(jt, None))

    @pl.when(has_prev_near)
    def _():
        attend_tile(jd - 1, 2)
        attend_tile(jd, 0)

    @pl.when(jnp.logical_not(has_prev_near))
    def _():
        attend_tile(jd, jnp.where(on_tile_edge, 0, 1))

    for h in range(nh):
        rows = slice(h * tq, (h + 1) * tq)
        o_ref[:, h * dh:(h + 1) * dh] = (acc_ref[rows, :dh] / acc_ref[rows, dh:]).astype(o_ref.dtype)


def _t5_bucket(rel):
    n = jnp.maximum(rel, 0)
    exact = T5_BUCKETS // 2
    nf = jnp.maximum(n, exact).astype(F32)
    large = exact + (jnp.log(nf / exact) / math.log(T5_MAX_DISTANCE / exact)
                     * (T5_BUCKETS - exact)).astype(jnp.int32)
    return jnp.where(n < exact, n, jnp.minimum(large, T5_BUCKETS - 1))


def _dsa_layer(x, mix_g, w_in, q_norm_g, k_norm_g, t5_table, w_o, bsz, seq):
    t, d = x.shape
    nh, dh, nih, di = DSA_HEADS, DSA_HEAD_DIM, IDX_HEADS, IDX_DIM
    tq, tk = DSA_TQ, DSA_TK
    assert seq % (tk * DSA_TILES_PER_TRIP) == 0 and tk == 2 * tq and dh == LANES and di <= LANES and nih <= LANES
    assert T5_MAX_DISTANCE <= tq
    n_sel = min(DSA_TOPK_MAX, seq // 4)
    o1, o2, o3, o4, o5 = nh * dh, nh * dh + dh, nh * dh + 2 * dh, nh * dh + 2 * dh + nih * di, nh * dh + 2 * dh + nih * di + di
    w_qi = jnp.pad(w_in[:, o3:o4].reshape(d, nih, di), ((0, 0), (0, 0), (0, LANES - di))).reshape(d, nih * LANES)
    w_ki = jnp.pad(w_in[:, o4:o5], ((0, 0), (0, LANES - di)))
    w_wi = jnp.pad(w_in[:, o5:o5 + nih] * (nih ** -0.5 * di ** -0.5), ((0, 0), (0, LANES - nih)))
    w = jnp.concatenate([w_in[:, :o3], w_qi, w_ki, w_wi], axis=1).astype(BF16)
    nq, nqi = nh // 2, nih // 2
    assert w.shape[1] == 2 * dh * (nq + nqi + 2) and nh % 2 == 0 and nih % 2 == 0
    tm = min(1024, seq)
    assert seq % tm == 0
    tok = lambda width: pl.BlockSpec((None, tm, width), lambda b, i, j: (b, i, 0))
    qn, qi, kn, vb, kib, wis = pl.pallas_call(
        functools.partial(_dsa_proj_kernel, nh=nh, nih=nih),
        grid=(bsz, seq // tm, nq + nqi + 2),
        in_specs=[pl.BlockSpec((None, tm, d), lambda b, i, j: (b, i, 0)),
                  pl.BlockSpec((1, d), lambda b, i, j: (0, 0)),
                  pl.BlockSpec((d, 2 * dh), lambda b, i, j: (0, j)),
                  pl.BlockSpec((1, dh), lambda b, i, j: (0, 0)),
                  pl.BlockSpec((1, dh), lambda b, i, j: (0, 0))],
        out_specs=[pl.BlockSpec((None, 2, tm, dh), lambda b, i, j: (b, jnp.minimum(j, nq - 1), i, 0)),
                   pl.BlockSpec((None, 2, tm, LANES), lambda b, i, j: (b, jnp.clip(j - nq - 1, 0, nqi - 1), i, 0)),
                   tok(dh), tok(dh + LANES), tok(LANES), tok(LANES)],
        out_shape=[jax.ShapeDtypeStruct((bsz, nh, seq, dh), BF16),
                   jax.ShapeDtypeStruct((bsz, nih, seq, LANES), BF16),
                   jax.ShapeDtypeStruct((bsz, seq, dh), BF16),
                   jax.ShapeDtypeStruct((bsz, seq, dh + LANES), BF16),
                   jax.ShapeDtypeStruct((bsz, seq, LANES), BF16),
                   jax.ShapeDtypeStruct((bsz, seq, LANES), F32)],
        scratch_shapes=[pltpu.VMEM((tm, d), BF16)],
        compiler_params=_cparams("parallel", "parallel", "arbitrary"), name="dsa_proj")(
            x.reshape(bsz, seq, d), mix_g.reshape(1, d), w, q_norm_g.reshape(1, dh), k_norm_g.reshape(1, dh))

    ii = jnp.arange(tq, dtype=jnp.int32)[:, None]
    jj = jnp.arange(tk, dtype=jnp.int32)[None, :]
    buckets = jnp.stack([_t5_bucket(off + ii - jj) for off in (0, tq, 2 * tq)])
    rel_table = (t5_table - t5_table[T5_BUCKETS - 1]).astype(F32)
    near = jnp.einsum("otkb,bh->ohtk", jax.nn.one_hot(buckets, T5_BUCKETS, dtype=F32), rel_table,
                      precision=lax.Precision.HIGHEST) * LOG2E

    seqblk = lambda: pl.BlockSpec((None, seq, LANES), lambda b, i: (b, 0, 0))
    out = pl.pallas_call(
        functools.partial(_dsa_kernel, nh=nh, nih=nih, n_sel=n_sel, idx_bits=int(seq).bit_length()),
        grid=(bsz, seq // tq),
        in_specs=[pl.BlockSpec((None, nh, tq, dh), lambda b, i: (b, 0, i, 0)),
                  pl.BlockSpec((None, nih, tq, LANES), lambda b, i: (b, 0, i, 0)),
                  pl.BlockSpec((None, tq, LANES), lambda b, i: (b, i, 0)),
                  seqblk(), pl.BlockSpec((None, seq, dh + LANES), lambda b, i: (b, 0, 0)), seqblk(),
                  pl.BlockSpec((3, nh, tq, tk), lambda b, i: (0, 0, 0, 0))],
        out_specs=pl.BlockSpec((None, tq, nh * dh), lambda b, i: (b, i, 0)),
        out_shape=jax.ShapeDtypeStruct((bsz, seq, nh * dh), BF16),
        scratch_shapes=[pltpu.VMEM((seq // tk, tq, tk), jnp.int32),
                        pltpu.VMEM((seq // tk, tk, tq), jnp.int16),
                        pltpu.VMEM((seq // tk, tk, tq), jnp.int16),
                        pltpu.VMEM((seq // tk, tk, tq), jnp.int16),
                        pltpu.VMEM((nih, tq, LANES), F32),
                        pltpu.VMEM((nh * tq, dh + LANES), F32),
                        pltpu.VMEM((nh * tq, LANES), F32)],
        compiler_params=_cparams("parallel", "arbitrary"), name="dsa_attn")(
            qn, qi, wis, kn, vb, kib, near)
    return out.reshape(t, nh * dh), w_o.astype(BF16)


def kernel(x, rwkv_mu, rwkv_w_rkv, rwkv_w0, rwkv_w1, rwkv_w2, rwkv_a0, rwkv_a1, rwkv_a2, rwkv_v0, rwkv_v1,
           rwkv_v2, rwkv_g1, rwkv_g2, rwkv_k_k, rwkv_k_a, rwkv_r_k, rwkv_lnx_g, rwkv_lnx_b, rwkv_w_o,
           mlstm_w_in, mlstm_b_if, mlstm_norm_g, mlstm_w_o, dsa_w_in, dsa_q_norm_g, dsa_k_norm_g, dsa_w_o,
           t5_bias, mix_norm_g, ffn_norm_g, ffn_w_gate, ffn_w_up, ffn_w_down):
    bsz, seq, d = x.shape
    depth = mix_norm_g.shape[0]
    h = x.reshape(bsz * seq, d)
    v_first = None
    for i in range(depth):
        kind, j = i % 3, i // 3
        if kind == 0:
            vres = None if j == 0 else (rwkv_v0[j - 1], rwkv_v1[j - 1], rwkv_v2[j - 1])
            p = (rwkv_mu[j], rwkv_w_rkv[j], rwkv_w0[j], rwkv_w1[j], rwkv_w2[j], rwkv_a0[j], rwkv_a1[j],
                 rwkv_a2[j], rwkv_g1[j], rwkv_g2[j], rwkv_k_k[j], rwkv_k_a[j], rwkv_r_k[j],
                 rwkv_lnx_g[j], rwkv_lnx_b[j], rwkv_w_o[j])
            y, w_o, v_first = _rwkv_layer(h, mix_norm_g[i], p, vres, v_first, bsz, seq)
        elif kind == 1:
            y, w_o = _mlstm_layer(h, mix_norm_g[i], mlstm_w_in[j], mlstm_b_if[j], mlstm_norm_g[j],
                                  mlstm_w_o[j], bsz, seq)
        else:
            y, w_o = _dsa_layer(h, mix_norm_g[i], dsa_w_in[j], dsa_q_norm_g[j], dsa_k_norm_g[j], t5_bias,
                                dsa_w_o[j], bsz, seq)
        h = _proj_ffn(h, y, w_o, ffn_norm_g[i], ffn_w_gate[i].astype(BF16), ffn_w_up[i].astype(BF16),
                      ffn_w_down[i].astype(BF16))
    return h.reshape(bsz, seq, d)
```

```python
import functools
import math

import jax
import jax.numpy as jnp
from jax import lax
from jax.experimental import pallas as pl
from jax.experimental.pallas import tpu as pltpu

F32 = jnp.float32
BF16 = jnp.bfloat16

V7X_VMEM_LIMIT_BYTES = 56 * 1024 * 1024
LANES = 128

RMS_EPS = 1e-6
RWKV_HEAD = 64
RWKV_DECAY_SCALE = math.exp(-0.5)
RWKV_GN_EPS = 64e-5
RWKV_CHUNK = 64
RWKV_CHAINS = 16
MLSTM_HEADS = 4
MLSTM_HEADS_PER_STEP = 2
MLSTM_GATE_CAP = 15.0
DSA_HEADS = 16
DSA_HEAD_DIM = 128
IDX_HEADS = 16
IDX_DIM = 64
DSA_TOPK_MAX = 256
T5_BUCKETS = 32
T5_MAX_DISTANCE = 128


def _cparams(*sem):
    return pltpu.CompilerParams(dimension_semantics=sem, vmem_limit_bytes=V7X_VMEM_LIMIT_BYTES)


def _dot(a, b):
    return jnp.dot(a.astype(BF16), b.astype(BF16), preferred_element_type=F32)


def _dot_nt(a, b):
    return lax.dot_general(a.astype(BF16), b.astype(BF16), (((1,), (1,)), ((), ())),
                           preferred_element_type=F32)


def _dot_tn(a, b):
    return lax.dot_general(a.astype(BF16), b.astype(BF16), (((0,), (0,)), ((), ())),
                           preferred_element_type=F32)


def _split3(x):
    hi = x.astype(BF16)
    r1 = x - hi.astype(F32)
    mid = r1.astype(BF16)
    lo = (r1 - mid.astype(F32)).astype(BF16)
    return hi, mid, lo


def _dot_exact_lhs(a_bf16, x):
    hi, mid, lo = _split3(x)
    d = lambda p: jnp.dot(a_bf16, p, preferred_element_type=F32)
    return d(hi) + (d(mid) + d(lo))


def _rms(x, g):
    ms = jnp.mean(x * x, axis=-1, keepdims=True)
    return x * lax.rsqrt(ms + RMS_EPS) * g


def _sigmoid(x):
    return 1.0 / (1.0 + jnp.exp(-x))


def _norm_mm_kernel(x_ref, g_ref, w_ref, *rest):
    h_ref = rest[-1]
    has_side = len(rest) == 4
    o_ref = rest[1] if has_side else rest[0]

    @pl.when(pl.program_id(1) == 0)
    def _():
        h_ref[...] = _rms(x_ref[...], g_ref[...]).astype(BF16)
        if has_side:
            rest[2][...] = jnp.dot(h_ref[...], rest[0][...], preferred_element_type=F32)

    o_ref[...] = jnp.dot(h_ref[...], w_ref[...], preferred_element_type=F32).astype(o_ref.dtype)


def _norm_mm(x, g, w, out_dtype=F32, w_side=None, tm=1024, tn=512):
    m, k = x.shape
    n = w.shape[1]
    tm, tn = min(tm, m), min(tn, n)
    assert m % tm == 0 and n % tn == 0
    in_specs = [pl.BlockSpec((tm, k), lambda i, j: (i, 0)),
                pl.BlockSpec((1, k), lambda i, j: (0, 0)),
                pl.BlockSpec((k, tn), lambda i, j: (0, j))]
    out_specs = [pl.BlockSpec((tm, tn), lambda i, j: (i, j))]
    out_shape = [jax.ShapeDtypeStruct((m, n), out_dtype)]
    args = [x, g.reshape(1, k), w]
    if w_side is not None:
        ns = w_side.shape[1]
        in_specs.append(pl.BlockSpec((k, ns), lambda i, j: (0, 0)))
        out_specs.append(pl.BlockSpec((tm, ns), lambda i, j: (i, 0)))
        out_shape.append(jax.ShapeDtypeStruct((m, ns), F32))
        args.append(w_side)
    out = pl.pallas_call(
        _norm_mm_kernel, grid=(m // tm, n // tn), in_specs=in_specs, out_specs=out_specs, out_shape=out_shape,
        scratch_shapes=[pltpu.VMEM((tm, k), BF16)],
        compiler_params=_cparams("parallel", "arbitrary"), name="norm_mm")(*args)
    return out if w_side is not None else out[0]


def _ffn_kernel(x_ref, y_ref, wo_ref, g_ref, wg_ref, wu_ref, wd_ref, o_ref, h_ref):
    @pl.when(pl.program_id(1) == 0)
    def _():
        x1 = x_ref[...] + jnp.dot(y_ref[...], wo_ref[...], preferred_element_type=F32)
        h_ref[...] = _rms(x1, g_ref[...]).astype(BF16)
        o_ref[...] = x1

    h = h_ref[...]
    gate = jnp.dot(h, wg_ref[...], preferred_element_type=F32)
    up = jnp.dot(h, wu_ref[...], preferred_element_type=F32)
    act = (gate * _sigmoid(gate) * up).astype(BF16)
    o_ref[...] += jnp.dot(act, wd_ref[...], preferred_element_type=F32)


def _proj_ffn(x, y, wo, g, wg, wu, wd, tm=512, tf=512):
    m, d = x.shape
    dy = y.shape[1]
    f = wg.shape[1]
    tm, tf = min(tm, m), min(tf, f)
    assert m % tm == 0 and f % tf == 0
    return pl.pallas_call(
        _ffn_kernel, grid=(m // tm, f // tf),
        in_specs=[pl.BlockSpec((tm, d), lambda i, j: (i, 0)),
                  pl.BlockSpec((tm, dy), lambda i, j: (i, 0)),
                  pl.BlockSpec((dy, d), lambda i, j: (0, 0), pipeline_mode=pl.Buffered(1)),
                  pl.BlockSpec((1, d), lambda i, j: (0, 0)),
                  pl.BlockSpec((d, tf), lambda i, j: (0, j)),
                  pl.BlockSpec((d, tf), lambda i, j: (0, j)),
                  pl.BlockSpec((tf, d), lambda i, j: (j, 0))],
        out_specs=pl.BlockSpec((tm, d), lambda i, j: (i, 0)),
        out_shape=jax.ShapeDtypeStruct((m, d), F32),
        scratch_shapes=[pltpu.VMEM((tm, d), BF16)],
        compiler_params=_cparams("parallel", "arbitrary"), name="proj_ffn")(
            x, y, wo, g.reshape(1, d), wg, wu, wd)


def _rwkv_prep_kernel(*refs, seq, tm, has_v):
    if has_v:
        (x_ref, xp_ref, g_ref, mu_ref, w0_ref, w1_ref, w2_ref, a0_ref, a1_ref, a2_ref,
         g1_ref, g2_ref, v0_ref, v1_ref, v2_ref, xs_ref, lw_ref, a_ref, gate_ref, vg_ref) = refs
    else:
        (x_ref, xp_ref, g_ref, mu_ref, w0_ref, w1_ref, w2_ref, a0_ref, a1_ref, a2_ref,
         g1_ref, g2_ref, xs_ref, lw_ref, a_ref, gate_ref) = refs
    i = pl.program_id(0)
    gn = g_ref[...]
    h = _rms(x_ref[...], gn)
    hp = _rms(xp_ref[...], gn)
    seq_start = (i * tm) % seq == 0
    hp_row = jnp.where(seq_start, 0.0, hp[7:8, :])
    row = lax.broadcasted_iota(jnp.int32, (tm, 1), 0)
    h_prev = jnp.where(row == 0, hp_row, pltpu.roll(h, 1, 0))
    xx = h_prev - h
    mix = lambda n: h + xx * mu_ref[n:n + 1, :]
    xs_ref[0] = mix(0).astype(BF16)
    xs_ref[1] = mix(2).astype(BF16)
    xv = mix(3).astype(BF16)
    xs_ref[2] = xv
    lw_ref[...] = -RWKV_DECAY_SCALE * _sigmoid(
        w0_ref[...] + _dot(jnp.tanh(_dot(mix(1), w1_ref[...])), w2_ref[...]))
    a_ref[...] = _sigmoid(a0_ref[...] + _dot(_dot(mix(4), a1_ref[...]), a2_ref[...])).astype(a_ref.dtype)
    gate_ref[...] = _dot(_sigmoid(_dot(mix(5), g1_ref[...])), g2_ref[...]).astype(gate_ref.dtype)
    if has_v:
        vg_ref[...] = _sigmoid(v0_ref[...] + _dot(_dot(xv, v1_ref[...]), v2_ref[...])).astype(vg_ref.dtype)


def _pad_lora(w_in, w_out):
    r = w_in.shape[1]
    rp = -(-r // LANES) * LANES
    return (jnp.pad(w_in, ((0, 0), (0, rp - r))).astype(BF16),
            jnp.pad(w_out, ((0, rp - r), (0, 0))).astype(BF16))


def _rwkv_prep(x, norm_g, mu, w0, w1, w2, a0, a1, a2, g1, g2, vres, seq, tm=256):
    t, d = x.shape
    tm = min(tm, seq)
    assert t % tm == 0 and seq % tm == 0 and tm % 8 == 0
    has_v = vres is not None
    row = lambda v: v.reshape(1, d)
    full = lambda a: pl.BlockSpec(a.shape, lambda i: (0,) * a.ndim)
    w1p, w2p = _pad_lora(w1, w2)
    a1p, a2p = _pad_lora(a1, a2)
    g1p, g2p = _pad_lora(g1, g2)
    mu8 = jnp.pad(mu, ((0, 2), (0, 0)))
    params = [row(norm_g), mu8, row(w0), w1p, w2p, row(a0), a1p, a2p, g1p, g2p]
    if has_v:
        v1p, v2p = _pad_lora(vres[1], vres[2])
        params += [row(vres[0]), v1p, v2p]
    tile = pl.BlockSpec((tm, d), lambda i: (i, 0))
    in_specs = [tile, pl.BlockSpec((8, d), lambda i: (jnp.maximum(i * (tm // 8) - 1, 0), 0))]
    in_specs += [full(p) for p in params]
    n_gates = 3 if has_v else 2
    out_shape = ([jax.ShapeDtypeStruct((3, t, d), BF16), jax.ShapeDtypeStruct((t, d), F32)]
                 + [jax.ShapeDtypeStruct((t, d), BF16)] * n_gates)
    out_specs = [pl.BlockSpec((3, tm, d), lambda i: (0, i, 0))] + [tile] * (1 + n_gates)
    return pl.pallas_call(
        functools.partial(_rwkv_prep_kernel, seq=seq, tm=tm, has_v=has_v),
        grid=(t // tm,), in_specs=in_specs, out_specs=out_specs, out_shape=out_shape,
        compiler_params=_cparams("parallel"), name="rwkv_prep")(x, x, *params)


def _seg_sum(x, seg):
    w = x.shape[1]
    hi = x.astype(BF16)
    lo = (x - hi.astype(F32)).astype(BF16)
    d = lambda p, q: jnp.dot(p[:, q:q + 256], seg, preferred_element_type=F32)
    return jnp.concatenate([d(hi, q) + d(lo, q) for q in range(0, w, 256)], axis=1)


def _rwkv_scan_kernel(*refs, L, tc, has_v):
    (s_ref, a2_ref, r2_ref, b2_ref, k2_ref, v2_ref, pl_ref, phi_ref, psi_ref, theta_ref, yloc_ref,
     y_ref, bonus_ref, mab_ref, tinv_ref, mak_ref, mrb_ref, mrk_ref, av_ref) = refs[-19:]
    refs = refs[:-19]
    if has_v:
        (xs_ref, w_ref, lw_ref, a_ref, gate_ref, vf_ref, vg_ref,
         kk_ref, ka_ref, rk_ref, lng_ref, lnb_ref, o_ref) = refs
    else:
        (xs_ref, w_ref, lw_ref, a_ref, gate_ref,
         kk_ref, ka_ref, rk_ref, lng_ref, lnb_ref, o_ref, vout_ref) = refs
    W = lw_ref.shape[-1]
    P2 = 2 * L
    npair = W // P2
    N = RWKV_HEAD

    @pl.when(pl.program_id(2) == 0)
    def _():
        s_ref[...] = jnp.zeros_like(s_ref)

    ri = lax.broadcasted_iota(jnp.int32, (P2, P2), 0)
    ci = lax.broadcasted_iota(jnp.int32, (P2, P2), 1)
    strict = ri > ci
    incl = ri >= ci
    eye = (ri == ci).astype(F32)
    levels = []
    s = 1
    while s < L:
        levels.append(((ri // s) % 2 == 1) & ((ci // s) == (ri // s) - 1))
        s *= 2
    head0 = lax.broadcasted_iota(jnp.int32, (L, P2), 1) < N
    sr = lax.broadcasted_iota(jnp.int32, (256, 256), 0) // N
    sc = lax.broadcasted_iota(jnp.int32, (256, 256), 1) // N
    seg = (sr == sc).astype(BF16)

    nc = tc // L
    tr = lax.broadcasted_iota(jnp.int32, (tc, tc), 0)
    tcol = lax.broadcasted_iota(jnp.int32, (tc, tc), 1)
    tri = jnp.where(tr >= tcol, jnp.where(tr // L == tcol // L, 1.0, 0.0), 0.0).astype(BF16)
    del tr, tcol

    r, k, v = (jnp.dot(xs_ref[n], w_ref[n], preferred_element_type=F32) for n in range(3))
    lw = lw_ref[...]
    a = a_ref[...].astype(F32)
    if has_v:
        v = v + (vf_ref[...] - v) * vg_ref[...].astype(F32)
    else:
        vout_ref[...] = v
    kk = k * kk_ref[...]
    kk = kk / jnp.maximum(jnp.sqrt(_seg_sum(kk * kk, seg)), 1e-12)
    kmod = k * (1.0 + (a - 1.0) * ka_ref[...])
    c = _dot_exact_lhs(tri, lw)
    enc = jnp.exp(-c)
    bonus_ref[...] = _seg_sum(r * kmod * rk_ref[...], seg) * v
    operands = (-kk * jnp.exp(c - lw), r * jnp.exp(c), kk * a * enc, kmod * enc, v)
    for z, z_ref in zip(operands, (a2_ref, r2_ref, b2_ref, k2_ref, v2_ref)):
        for ch in range(nc):
            for p in range(npair):
                zz = z[ch * L:(ch + 1) * L, P2 * p:P2 * (p + 1)]
                z_ref[ch, p, :L] = jnp.where(head0, zz, 0.0).astype(BF16)
                z_ref[ch, p, L:] = jnp.where(head0, 0.0, zz).astype(BF16)
    for ch in range(nc):
        pl_ref[ch] = jnp.broadcast_to(jnp.exp(c[(ch + 1) * L - 1:(ch + 1) * L, :]), (8, W))
    del r, k, v, lw, a, kk, kmod, c, enc, operands

    chains = [(ch, p) for ch in range(nc) for p in range(npair)]
    for g0 in range(0, len(chains), RWKV_CHAINS):
        group = chains[g0:g0 + RWKV_CHAINS]
        for c in group:
            g = _dot_nt(jnp.concatenate([a2_ref[c], r2_ref[c]], axis=0),
                        jnp.concatenate([b2_ref[c], k2_ref[c]], axis=0))
            m_ab = jnp.where(strict, g[:P2, :P2], 0.0)
            mab_ref[c] = m_ab
            tinv_ref[c] = eye + jnp.where(levels[0], m_ab, 0.0)
            mak_ref[c] = jnp.where(strict, g[:P2, P2:], 0.0).astype(BF16)
            mrb_ref[c] = jnp.where(incl, g[P2:, :P2], 0.0).astype(BF16)
            mrk_ref[c] = jnp.where(incl, g[P2:, P2:], 0.0).astype(BF16)
        for lv in levels[1:]:
            steps = [_dot(jnp.where(lv, mab_ref[c], 0.0), tinv_ref[c]).astype(BF16) for c in group]
            for c, step in zip(group, steps):
                t_cur = tinv_ref[c]
                tinv_ref[c] = t_cur + _dot(t_cur, step)
        mvs = [_dot(mak_ref[c], v2_ref[c]).astype(BF16) for c in group]
        for c, mv in zip(group, mvs):
            av_ref[c] = _dot(tinv_ref[c], jnp.concatenate([a2_ref[c], mv], axis=1)).astype(BF16)
        ths = [_dot(mrb_ref[c], av_ref[c]) for c in group]
        yls = [_dot(mrk_ref[c], v2_ref[c]) for c in group]
        for c, th, yl in zip(group, ths, yls):
            theta_ref[c] = (r2_ref[c].astype(F32) + th[:, :P2]).astype(BF16)
            yloc_ref[c] = th[:, P2:] + yl
        for c in group:
            av = av_ref[c]
            pp = _dot_tn(jnp.concatenate([av[:, P2:], av[:, :P2]], axis=1), b2_ref[c])
            p_last = pl_ref[c[0]][0:1, P2 * c[1]:P2 * (c[1] + 1)]
            phi_ref[c] = ((eye + pp[P2:]) * p_last).astype(BF16)
            psi_ref[c] = (pp[:P2] + _dot_tn(v2_ref[c], k2_ref[c])) * p_last

    for ch in range(nc):
        for p in range(npair):
            s0 = s_ref[p]
            yo = _dot_nt(theta_ref[ch, p], s0) + yloc_ref[ch, p]
            y_ref[ch * L:(ch + 1) * L, P2 * p:P2 * (p + 1)] = yo[:L] + yo[L:]
            s_ref[p] = _dot(s0, phi_ref[ch, p]) + psi_ref[ch, p]

    y = y_ref[...]
    mean = _seg_sum(y, seg) * (1.0 / N)
    yc = y - mean
    var = _seg_sum(yc * yc, seg) * (1.0 / N)
    yn = yc * lax.rsqrt(var + RWKV_GN_EPS) * lng_ref[...] + lnb_ref[...]
    o_ref[...] = ((yn + bonus_ref[...]) * gate_ref[...].astype(F32)).astype(o_ref.dtype)


def _rwkv_scan(xs, w_rkv, lw, a, gate, v_first, vgate, k_k, k_a, r_k, lnx_g, lnx_b, bsz, seq, tc=256, wb=512):
    _, t, d = xs.shape
    L = RWKV_CHUNK
    tc, wb = min(tc, seq), min(wb, d)
    assert seq % tc == 0 and tc % L == 0 and d % wb == 0 and wb % 256 == 0
    has_v = v_first is not None
    nc, p2, npair = tc // L, 2 * L, wb // (2 * L)
    b3 = lambda z: z.reshape(bsz, seq, d)
    blk = pl.BlockSpec((None, tc, wb), lambda b, h, c: (b, c, h))
    prow = pl.BlockSpec((1, wb), lambda b, h, c: (0, h))
    args = [xs.reshape(3, bsz, seq, d), w_rkv, b3(lw), b3(a), b3(gate)]
    in_specs = [pl.BlockSpec((3, None, tc, d), lambda b, h, c: (0, b, c, 0)),
                pl.BlockSpec((3, d, wb), lambda b, h, c: (0, 0, h)),
                blk, blk, blk]
    if has_v:
        args += [b3(v_first), b3(vgate)]
        in_specs += [blk, blk]
    args += [z.reshape(1, d) for z in (k_k, k_a, r_k, lnx_g, lnx_b)]
    in_specs += [prow] * 5
    out_shape = [jax.ShapeDtypeStruct((bsz, seq, d), BF16)]
    if not has_v:
        out_shape.append(jax.ShapeDtypeStruct((bsz, seq, d), F32))
    out = pl.pallas_call(
        functools.partial(_rwkv_scan_kernel, L=L, tc=tc, has_v=has_v),
        grid=(bsz, d // wb, seq // tc), in_specs=in_specs, out_specs=[blk] * len(out_shape),
        out_shape=out_shape,
        scratch_shapes=[pltpu.VMEM((npair, p2, p2), F32)]
        + [pltpu.VMEM((nc, npair, p2, p2), BF16)] * 5
        + [pltpu.VMEM((nc, 8, wb), F32),
           pltpu.VMEM((nc, npair, p2, p2), BF16), pltpu.VMEM((nc, npair, p2, p2), F32),
           pltpu.VMEM((nc, npair, p2, p2), BF16), pltpu.VMEM((nc, npair, p2, p2), F32),
           pltpu.VMEM((tc, wb), F32), pltpu.VMEM((tc, wb), F32),
           pltpu.VMEM((nc, npair, p2, p2), F32), pltpu.VMEM((nc, npair, p2, p2), F32),
           pltpu.VMEM((nc, npair, p2, p2), BF16), pltpu.VMEM((nc, npair, p2, p2), BF16),
           pltpu.VMEM((nc, npair, p2, p2), BF16),
           pltpu.VMEM((nc, npair, p2, 2 * p2), BF16)],
        compiler_params=_cparams("parallel", "parallel", "arbitrary"), name="rwkv_scan")(*args)
    return out[0].reshape(t, d), (None if has_v else out[1].reshape(t, d))


def _rwkv_layer(x, norm_g, p, vres, v_first, bsz, seq):
    (mu, w_rkv, w0, w1, w2, a0, a1, a2, g1, g2, k_k, k_a, r_k, lnx_g, lnx_b, w_o) = p
    outs = _rwkv_prep(x, norm_g, mu, w0, w1, w2, a0, a1, a2, g1, g2, vres, seq)
    xs, lw, a, gate = outs[:4]
    vgate = outs[4] if vres is not None else None
    o, v_own = _rwkv_scan(xs, w_rkv.astype(BF16), lw, a, gate, v_first if vres is not None else None, vgate,
                          k_k, k_a, r_k, lnx_g, lnx_b, bsz, seq)
    return o, w_o.astype(BF16), (v_own if vres is None else v_first)


def _pick_tile(n, cap):
    best = LANES
    for tile in range(LANES, cap + 1, LANES):
        if n % tile == 0:
            best = tile
    return best


def _mlstm_kernel(q_ref, k_ref, v_ref, o_ref, gt_ref, bias_ref, ng_ref, out_ref, c_ref, m_ref, *, nh, hps):
    lc = q_ref.shape[0]
    dk = q_ref.shape[1] // hps
    dv = v_ref.shape[1] // hps

    @pl.when(pl.program_id(2) == 0)
    def _():
        c_ref[...] = jnp.zeros_like(c_ref)
        m_ref[...] = jnp.zeros_like(m_ref)

    lane = lax.broadcasted_iota(jnp.int32, (lc, LANES), 1)
    z = gt_ref[...] + bias_ref[...]
    zc = MLSTM_GATE_CAP * jnp.tanh(z / MLSTM_GATE_CAP)
    lf_all = jnp.minimum(zc, 0.0) - jnp.log1p(jnp.exp(-jnp.abs(zc)))
    rr = lax.broadcasted_iota(jnp.int32, (lc, lc), 0)
    cc = lax.broadcasted_iota(jnp.int32, (lc, lc), 1)
    causal = rr >= cc
    bcum_all = _dot_exact_lhs(causal.astype(BF16), lf_all)
    comb = jnp.where(lane < nh, zc, bcum_all)
    er = lax.broadcasted_iota(jnp.int32, (8, LANES), 0)
    ec = lax.broadcasted_iota(jnp.int32, (8, LANES), 1)
    head_of_row = pl.program_id(1) * hps + er // 2
    sel = jnp.where(er < 2 * hps, jnp.where(ec == head_of_row + nh * (er % 2), 1.0, 0.0), 0.0).astype(BF16)
    hi, mid, lo = _split3(comb)
    tr = lambda p: lax.dot_general(sel, p, (((1,), (1,)), ((), ())), preferred_element_type=F32)
    rows = tr(hi) + (tr(mid) + tr(lo))
    ones_blk = (lane == 0).astype(BF16)

    for s in range(hps):
        h = pl.program_id(1) * hps + s
        li_col = jnp.sum(jnp.where(lane == h, comb, 0.0), axis=-1, keepdims=True)
        bc_col = jnp.sum(jnp.where(lane == h + nh, comb, 0.0), axis=-1, keepdims=True)
        li_row, bc_row = rows[2 * s:2 * s + 1], rows[2 * s + 1:2 * s + 2]
        m_st = m_ref[s, 0:1, 0:1]
        dmat = jnp.where(causal, bc_col - bc_row + li_row, -jnp.inf)
        inter = bc_col + m_st
        m_t = jnp.maximum(inter, jnp.max(dmat, axis=-1, keepdims=True))
        q = q_ref[:, s * dk:(s + 1) * dk] * (dk ** -0.5)
        k = k_ref[:, s * dk:(s + 1) * dk]
        sc = _dot_nt(q, k) * jnp.exp(dmat - m_t)
        w_inter = jnp.exp(inter - m_t)
        v_ext = jnp.concatenate([v_ref[:, s * dv:(s + 1) * dv].astype(BF16), ones_blk], axis=1)
        c_st = c_ref[s]
        nd = _dot(sc, v_ext) + w_inter * _dot(q, c_st)
        den = nd[:, dv:dv + 1]
        hc = nd[:, :dv] / jnp.maximum(jnp.abs(den), jnp.exp(-m_t))

        b_tot = bc_col[lc - 1:lc]
        log_wk = b_tot - bc_col + li_col
        m_new = jnp.maximum(b_tot + m_st, jnp.max(log_wk, axis=0, keepdims=True))
        c_ref[s] = jnp.exp(b_tot + m_st - m_new) * c_st + _dot_tn(k * jnp.exp(log_wk - m_new), v_ext)
        m_ref[s] = jnp.broadcast_to(m_new, m_ref.shape[1:])

        hn = hc * lax.rsqrt(jnp.mean(hc * hc, axis=-1, keepdims=True) + RMS_EPS) * ng_ref[:, s * dv:(s + 1) * dv]
        o_gate = _sigmoid(o_ref[:, s * dv:(s + 1) * dv].astype(F32))
        out_ref[:, s * dv:(s + 1) * dv] = (hn * o_gate).astype(out_ref.dtype)


def _mlstm_layer(x, mix_g, w_in, b_if, norm_g, w_o, bsz, seq, lc=256):
    t, d = x.shape
    nh = MLSTM_HEADS
    dk, dv = d // 2 // nh, d // nh
    nq = 2 * nh * dk + 2 * nh * dv
    assert w_in.shape[1] == nq + 2 * nh and dk % LANES == 0
    lc = min(lc, seq)
    assert seq % lc == 0
    w_gates = jnp.pad(w_in[:, nq:], ((0, 0), (0, LANES - 2 * nh))).astype(BF16)
    proj, gates = _norm_mm(x, mix_g, w_in[:, :nq].astype(BF16), out_dtype=BF16, w_side=w_gates,
                           tn=_pick_tile(nq, 1024))
    proj = proj.reshape(bsz, seq, nq)
    gates = gates.reshape(bsz, seq, LANES)
    bias = jnp.zeros((1, LANES), F32).at[0, :nh].set(b_if[0]).at[0, nh:2 * nh].set(b_if[1])
    hps = MLSTM_HEADS_PER_STEP
    assert nh % hps == 0
    ng = nh // hps
    out = pl.pallas_call(
        functools.partial(_mlstm_kernel, nh=nh, hps=hps),
        grid=(bsz, ng, seq // lc),
        in_specs=[pl.BlockSpec((None, lc, hps * dk), lambda b, h, c: (b, c, h)),
                  pl.BlockSpec((None, lc, hps * dk), lambda b, h, c: (b, c, ng + h)),
                  pl.BlockSpec((None, lc, hps * dv), lambda b, h, c: (b, c, ng + h)),
                  pl.BlockSpec((None, lc, hps * dv), lambda b, h, c: (b, c, 2 * ng + h)),
                  pl.BlockSpec((None, lc, LANES), lambda b, h, c: (b, c, 0)),
                  pl.BlockSpec((1, LANES), lambda b, h, c: (0, 0)),
                  pl.BlockSpec((1, hps * dv), lambda b, h, c: (0, h))],
        out_specs=pl.BlockSpec((None, lc, hps * dv), lambda b, h, c: (b, c, h)),
        out_shape=jax.ShapeDtypeStruct((bsz, seq, nh * dv), BF16),
        scratch_shapes=[pltpu.VMEM((hps, dk, dv + LANES), F32), pltpu.VMEM((hps, 8, LANES), F32)],
        compiler_params=_cparams("parallel", "parallel", "arbitrary"), name="mlstm")(
            proj, proj, proj, proj, gates, bias, norm_g.reshape(1, nh * dv))
    return out.reshape(t, nh * dv), w_o.astype(BF16)


DSA_TQ = 128
DSA_TK = 256
DSA_HEAD_GROUP = 4
DSA_TILES_PER_TRIP = 4
DSA_NEG = -1e30
KEY_NEG_INF = -2139095041
IDX_BIG = 2 ** 30
LOG2E = math.log2(math.e)


def _dsa_proj_kernel(x_ref, g_ref, w_ref, qg_ref, kg_ref, qn_ref, qi_ref, k_ref, v_ref, ki_ref, wi_ref, h_ref,
                     *, nh, nih):
    dh = DSA_HEAD_DIM
    j = pl.program_id(2)
    nq, nqi = nh // 2, nih // 2

    @pl.when(j == 0)
    def _():
        h_ref[...] = _rms(x_ref[...], g_ref[...]).astype(BF16)

    y = jnp.dot(h_ref[...], w_ref[...], preferred_element_type=F32)
    first, second = y[:, :dh], y[:, dh:]

    @pl.when(j < nq)
    def _():
        qscale = dh ** -0.5 * LOG2E
        qn_ref[0] = (_rms(first, qg_ref[...]) * qscale).astype(BF16)
        qn_ref[1] = (_rms(second, qg_ref[...]) * qscale).astype(BF16)

    @pl.when(j == nq)
    def _():
        k_ref[...] = _rms(first, kg_ref[...]).astype(BF16)
        v_ref[:, :dh] = second.astype(BF16)
        v_ref[:, dh:] = jnp.ones((v_ref.shape[0], LANES), BF16)

    @pl.when(jnp.logical_and(j > nq, j <= nq + nqi))
    def _():
        qi_ref[0] = first.astype(BF16)
        qi_ref[1] = second.astype(BF16)

    @pl.when(j == nq + nqi + 1)
    def _():
        ki_ref[...] = first.astype(BF16)
        wi_ref[...] = second


def _dsa_kernel(qn_ref, qi_ref, wi_ref, k_ref, v_ref, ki_ref, nb_ref, o_ref,
                key_ref, hi_ref, lo_ref, lo2_ref, w_ref, acc_ref, m_ref, *, nh, nih, n_sel, idx_bits):
    tq, tk, dh = DSA_TQ, DSA_TK, DSA_HEAD_DIM
    hg = DSA_HEAD_GROUP
    q0 = pl.program_id(1) * tq
    jd = (q0 + tq - 1) // tk
    nt = jd + 1
    rowi = lax.broadcasted_iota(jnp.int32, (tq, tk), 0)
    coli = lax.broadcasted_iota(jnp.int32, (tq, tk), 1)
    ktile = lambda jt: pl.ds(pl.multiple_of(jt * tk, tk), tk)
    twice = lambda z: jnp.concatenate([z] * (tk // LANES), axis=1)

    wi = wi_ref[...]
    for h in range(nih):
        w_ref[h] = jnp.broadcast_to(wi[:, h:h + 1], (tq, LANES))

    def for_each_tile(n, body, per_trip=DSA_TILES_PER_TRIP):
        def trip(i, c):
            for u in range(per_trip):
                body(per_trip * i + u)
            return c
        lax.fori_loop(0, n // per_trip, trip, 0)
        done = (n // per_trip) * per_trip
        for u in range(per_trip - 1):
            @pl.when(n - done > u)
            def _():
                body(done + u)

    def score_tile(jt):
        ki_t = ki_ref[ktile(jt), :]
        score = jnp.zeros((tq, tk), F32)
        for g0 in range(0, nih, hg):
            lg = _dot_nt(qi_ref[g0:g0 + hg].reshape(hg * tq, LANES), ki_t)
            for h in range(hg):
                score = score + jnp.maximum(lg[h * tq:(h + 1) * tq], 0.0) * twice(w_ref[g0 + h])
        score = jnp.where(jt * tk + coli <= q0 + rowi, score, -jnp.inf)
        bits = pltpu.bitcast(score, jnp.int32)
        key = bits ^ ((bits >> 31) & 0x7FFFFFFF)
        key_ref[jt] = key
        key_t = pltpu.bitcast(jnp.transpose(pltpu.bitcast(key, F32)), jnp.int32)
        hi_ref[jt] = (key_t >> 16).astype(jnp.int16)
        lo_ref[jt] = ((key_t & 0xFFFF) - 32768).astype(jnp.int16)

    for_each_tile(nt, score_tile)

    unroll = DSA_TILES_PER_TRIP
    n_trips = (nt + unroll - 1) // unroll
    floor16 = jnp.full((tk, tq), -32768, jnp.int16)

    def pad_tiles(ref):
        for u in range(unroll - 1):
            @pl.when(n_trips * unroll - nt > u)
            def _():
                ref[nt + u] = floor16

    def count16(ref, cand):
        cb = jnp.broadcast_to(cand.astype(jnp.int16), (32, tq))

        def body(i, acc):
            for u in range(unroll):
                tile = ref[unroll * i + u]
                for r0 in range(0, tk, 32):
                    acc = acc + jnp.where(tile[r0:r0 + 32] >= cb, jnp.int16(1), jnp.int16(0))
            return acc
        acc = lax.fori_loop(0, n_trips, body, jnp.zeros((32, tq), jnp.int16))
        return jnp.sum(acc.astype(F32), axis=0, keepdims=True)

    pad_tiles(hi_ref)

    def kth_largest16(ref, kth):
        def bit(b, prefix):
            cand = prefix + jnp.left_shift(jnp.int32(1), 15 - b)
            return jnp.where(count16(ref, cand) >= kth, cand, prefix)
        return lax.fori_loop(0, 16, bit, jnp.full((1, tq), -32768, jnp.int32))

    hi_k = kth_largest16(hi_ref, n_sel)
    above = jnp.where(hi_k >= 32767, 0.0, count16(hi_ref, jnp.minimum(hi_k + 1, 32767)))
    hi_kb = jnp.broadcast_to(hi_k.astype(jnp.int16), (tk, tq))

    def bucket_tile(jt, c):
        lo2_ref[jt] = jnp.where(hi_ref[jt] == hi_kb, lo_ref[jt], floor16)
        return c

    lax.fori_loop(0, nt, bucket_tile, 0)
    pad_tiles(lo2_ref)
    lo_k = kth_largest16(lo2_ref, n_sel - above)
    thr_row = hi_k * 65536 + (lo_k + 32768)
    tw = pltpu.bitcast(jnp.transpose(pltpu.bitcast(jnp.broadcast_to(thr_row, (tq, tq)), F32)), jnp.int32)
    thr = tw[:, :1]

    def count(hit):
        def body(jt, acc):
            keyt = key_ref[jt]
            for c0 in range(0, tk, LANES):
                acc = acc + hit(keyt[:, c0:c0 + LANES], jt * tk + c0)
            return acc
        acc = lax.fori_loop(0, nt, body, jnp.zeros((tq, LANES), F32))
        return jnp.sum(acc, axis=-1, keepdims=True)

    wide = lambda col: jnp.broadcast_to(col, (tq, LANES))
    lane = lax.broadcasted_iota(jnp.int32, (tq, LANES), 1)

    c_gt = count(lambda kv, base: jnp.where(kv > tw, 1.0, 0.0))
    c_ge = count(lambda kv, base: jnp.where(kv >= tw, 1.0, 0.0))
    need = n_sel - c_gt
    c_eq = c_ge - c_gt

    def tie_search():
        def index_bit(b, jcur):
            cand = jcur + jnp.left_shift(jnp.int32(1), idx_bits - 1 - b)
            cw = wide(cand)
            f = count(lambda kv, base: jnp.where(kv == tw, jnp.where(base + lane < cw, 1.0, 0.0), 0.0))
            return jnp.where(f <= need, cand, jcur)
        return lax.fori_loop(0, idx_bits, index_bit, jnp.zeros((tq, 1), jnp.int32))

    excess = jnp.max(c_eq - need) > 0.0
    jc = lax.cond(excess, tie_search, lambda: jnp.full((tq, 1), IDX_BIG, jnp.int32))
    jc = jnp.where(c_eq > need, jc, IDX_BIG)
    jc = jnp.where(thr == KEY_NEG_INF, 0, jc)

    m_ref[...] = jnp.full_like(m_ref, DSA_NEG)
    acc_ref[...] = jnp.zeros_like(acc_ref)

    def attend_tile(jt, near_idx):
        k_t = k_ref[ktile(jt), :]
        v_t = v_ref[ktile(jt), :]
        keyt = key_ref[jt]
        tie_mb = jnp.where(keyt == thr, jnp.where(jt * tk + coli < jc, 0.0, DSA_NEG), DSA_NEG)
        mb = jnp.where(keyt > thr, 0.0, tie_mb)
        for g0 in range(0, nh, hg):
            s_g = _dot_nt(qn_ref[g0:g0 + hg].reshape(hg * tq, dh), k_t)
            ps, alphas = [], []
            for h in range(hg):
                rows = slice((g0 + h) * tq, (g0 + h + 1) * tq)
                s = s_g[h * tq:(h + 1) * tq] + (mb if near_idx is None else nb_ref[near_idx, g0 + h] + mb)
                m_old = m_ref[rows, :]
                m_new = jnp.maximum(m_old, jnp.max(s, axis=-1, keepdims=True))
                m_ref[rows, :] = m_new
                ps.append(jnp.exp2(s - twice(m_new)).astype(BF16))
                alphas.append(jnp.exp2(m_old - m_new))
            pv = jnp.dot(jnp.concatenate(ps, axis=0), v_t, preferred_element_type=F32)
            rows_g = slice(g0 * tq, (g0 + hg) * tq)
            acc_ref[rows_g, :] = acc_ref[rows_g, :] * twice(jnp.concatenate(alphas, axis=0)) + pv

    on_tile_edge = q0 == jd * tk
    has_prev_near = jnp.logical_and(on_tile_edge, jd >= 1)
    n_far = jnp.where(has_prev_near, jd - 1, jd)

    for_each_tile(n_far, lambda jt: attend_tile(jt, None))

    @pl.when(has_prev_near)
    def _():
        attend_tile(jd - 1, 2)
        attend_tile(jd, 0)

    @pl.when(jnp.logical_not(has_prev_near))
    def _():
        attend_tile(jd, jnp.where(on_tile_edge, 0, 1))

    for h in range(nh):
        rows = slice(h * tq, (h + 1) * tq)
        o_ref[:, h * dh:(h + 1) * dh] = (acc_ref[rows, :dh] / acc_ref[rows, dh:]).astype(o_ref.dtype)


def _t5_bucket(rel):
    n = jnp.maximum(rel, 0)
    exact = T5_BUCKETS // 2
    nf = jnp.maximum(n, exact).astype(F32)
    large = exact + (jnp.log(nf / exact) / math.log(T5_MAX_DISTANCE / exact)
                     * (T5_BUCKETS - exact)).astype(jnp.int32)
    return jnp.where(n < exact, n, jnp.minimum(large, T5_BUCKETS - 1))


def _dsa_layer(x, mix_g, w_in, q_norm_g, k_norm_g, t5_table, w_o, bsz, seq):
    t, d = x.shape
    nh, dh, nih, di = DSA_HEADS, DSA_HEAD_DIM, IDX_HEADS, IDX_DIM
    tq, tk = DSA_TQ, DSA_TK
    assert seq % (tk * DSA_TILES_PER_TRIP) == 0 and tk == 2 * tq and dh == LANES and di <= LANES and nih <= LANES
    assert T5_MAX_DISTANCE <= tq
    n_sel = min(DSA_TOPK_MAX, seq // 4)
    o1, o2, o3, o4, o5 = nh * dh, nh * dh + dh, nh * dh + 2 * dh, nh * dh + 2 * dh + nih * di, nh * dh + 2 * dh + nih * di + di
    w_qi = jnp.pad(w_in[:, o3:o4].reshape(d, nih, di), ((0, 0), (0, 0), (0, LANES - di))).reshape(d, nih * LANES)
    w_ki = jnp.pad(w_in[:, o4:o5], ((0, 0), (0, LANES - di)))
    w_wi = jnp.pad(w_in[:, o5:o5 + nih] * (nih ** -0.5 * di ** -0.5), ((0, 0), (0, LANES - nih)))
    w = jnp.concatenate([w_in[:, :o3], w_qi, w_ki, w_wi], axis=1).astype(BF16)
    nq, nqi = nh // 2, nih // 2
    assert w.shape[1] == 2 * dh * (nq + nqi + 2) and nh % 2 == 0 and nih % 2 == 0
    tm = min(1024, seq)
    assert seq % tm == 0
    tok = lambda width: pl.BlockSpec((None, tm, width), lambda b, i, j: (b, i, 0))
    qn, qi, kn, vb, kib, wis = pl.pallas_call(
        functools.partial(_dsa_proj_kernel, nh=nh, nih=nih),
        grid=(bsz, seq // tm, nq + nqi + 2),
        in_specs=[pl.BlockSpec((None, tm, d), lambda b, i, j: (b, i, 0)),
                  pl.BlockSpec((1, d), lambda b, i, j: (0, 0)),
                  pl.BlockSpec((d, 2 * dh), lambda b, i, j: (0, j)),
                  pl.BlockSpec((1, dh), lambda b, i, j: (0, 0)),
                  pl.BlockSpec((1, dh), lambda b, i, j: (0, 0))],
        out_specs=[pl.BlockSpec((None, 2, tm, dh), lambda b, i, j: (b, jnp.minimum(j, nq - 1), i, 0)),
                   pl.BlockSpec((None, 2, tm, LANES), lambda b, i, j: (b, jnp.clip(j - nq - 1, 0, nqi - 1), i, 0)),
                   tok(dh), tok(dh + LANES), tok(LANES), tok(LANES)],
        out_shape=[jax.ShapeDtypeStruct((bsz, nh, seq, dh), BF16),
                   jax.ShapeDtypeStruct((bsz, nih, seq, LANES), BF16),
                   jax.ShapeDtypeStruct((bsz, seq, dh), BF16),
                   jax.ShapeDtypeStruct((bsz, seq, dh + LANES), BF16),
                   jax.ShapeDtypeStruct((bsz, seq, LANES), BF16),
                   jax.ShapeDtypeStruct((bsz, seq, LANES), F32)],
        scratch_shapes=[pltpu.VMEM((tm, d), BF16)],
        compiler_params=_cparams("parallel", "parallel", "arbitrary"), name="dsa_proj")(
            x.reshape(bsz, seq, d), mix_g.reshape(1, d), w, q_norm_g.reshape(1, dh), k_norm_g.reshape(1, dh))

    ii = jnp.arange(tq, dtype=jnp.int32)[:, None]
    jj = jnp.arange(tk, dtype=jnp.int32)[None, :]
    buckets = jnp.stack([_t5_bucket(off + ii - jj) for off in (0, tq, 2 * tq)])
    rel_table = (t5_table - t5_table[T5_BUCKETS - 1]).astype(F32)
    near = jnp.einsum("otkb,bh->ohtk", jax.nn.one_hot(buckets, T5_BUCKETS, dtype=F32), rel_table,
                      precision=lax.Precision.HIGHEST) * LOG2E

    seqblk = lambda: pl.BlockSpec((None, seq, LANES), lambda b, i: (b, 0, 0))
    out = pl.pallas_call(
        functools.partial(_dsa_kernel, nh=nh, nih=nih, n_sel=n_sel, idx_bits=int(seq).bit_length()),
        grid=(bsz, seq // tq),
        in_specs=[pl.BlockSpec((None, nh, tq, dh), lambda b, i: (b, 0, i, 0)),
                  pl.BlockSpec((None, nih, tq, LANES), lambda b, i: (b, 0, i, 0)),
                  pl.BlockSpec((None, tq, LANES), lambda b, i: (b, i, 0)),
                  seqblk(), pl.BlockSpec((None, seq, dh + LANES), lambda b, i: (b, 0, 0)), seqblk(),
                  pl.BlockSpec((3, nh, tq, tk), lambda b, i: (0, 0, 0, 0))],
        out_specs=pl.BlockSpec((None, tq, nh * dh), lambda b, i: (b, i, 0)),
        out_shape=jax.ShapeDtypeStruct((bsz, seq, nh * dh), BF16),
        scratch_shapes=[pltpu.VMEM((seq // tk, tq, tk), jnp.int32),
                        pltpu.VMEM((seq // tk, tk, tq), jnp.int16),
                        pltpu.VMEM((seq // tk, tk, tq), jnp.int16),
                        pltpu.VMEM((seq // tk, tk, tq), jnp.int16),
                        pltpu.VMEM((nih, tq, LANES), F32),
                        pltpu.VMEM((nh * tq, dh + LANES), F32),
                        pltpu.VMEM((nh * tq, LANES), F32)],
        compiler_params=_cparams("parallel", "arbitrary"), name="dsa_attn")(
            qn, qi, wis, kn, vb, kib, near)
    return out.reshape(t, nh * dh), w_o.astype(BF16)


def kernel(x, rwkv_mu, rwkv_w_rkv, rwkv_w0, rwkv_w1, rwkv_w2, rwkv_a0, rwkv_a1, rwkv_a2, rwkv_v0, rwkv_v1,
           rwkv_v2, rwkv_g1, rwkv_g2, rwkv_k_k, rwkv_k_a, rwkv_r_k, rwkv_lnx_g, rwkv_lnx_b, rwkv_w_o,
           mlstm_w_in, mlstm_b_if, mlstm_norm_g, mlstm_w_o, dsa_w_in, dsa_q_norm_g, dsa_k_norm_g, dsa_w_o,
           t5_bias, mix_norm_g, ffn_norm_g, ffn_w_gate, ffn_w_up, ffn_w_down):
    bsz, seq, d = x.shape
    depth = mix_norm_g.shape[0]
    h = x.reshape(bsz * seq, d)
    v_first = None
    for i in range(depth):
        kind, j = i % 3, i // 3
        if kind == 0:
            vres = None if j == 0 else (rwkv_v0[j - 1], rwkv_v1[j - 1], rwkv_v2[j - 1])
            p = (rwkv_mu[j], rwkv_w_rkv[j], rwkv_w0[j], rwkv_w1[j], rwkv_w2[j], rwkv_a0[j], rwkv_a1[j],
                 rwkv_a2[j], rwkv_g1[j], rwkv_g2[j], rwkv_k_k[j], rwkv_k_a[j], rwkv_r_k[j],
                 rwkv_lnx_g[j], rwkv_lnx_b[j], rwkv_w_o[j])
            y, w_o, v_first = _rwkv_layer(h, mix_norm_g[i], p, vres, v_first, bsz, seq)
        elif kind == 1:
            y, w_o = _mlstm_layer(h, mix_norm_g[i], mlstm_w_in[j], mlstm_b_if[j], mlstm_norm_g[j],
                                  mlstm_w_o[j], bsz, seq)
        else:
            y, w_o = _dsa_layer(h, mix_norm_g[i], dsa_w_in[j], dsa_q_norm_g[j], dsa_k_norm_g[j], t5_bias,
                                dsa_w_o[j], bsz, seq)
        h = _proj_ffn(h, y, w_o, ffn_norm_g[i], ffn_w_gate[i].astype(BF16), ffn_w_up[i].astype(BF16),
                      ffn_w_down[i].astype(BF16))
    return h.reshape(bsz, seq, d)
```

```python
import functools
import math

import jax
import jax.numpy as jnp
from jax import lax
from jax.experimental import pallas as pl
from jax.experimental.pallas import tpu as pltpu

F32 = jnp.float32
BF16 = jnp.bfloat16

V7X_VMEM_LIMIT_BYTES = 56 * 1024 * 1024
LANES = 128

RMS_EPS = 1e-6
RWKV_HEAD = 64
RWKV_DECAY_SCALE = math.exp(-0.5)
RWKV_GN_EPS = 64e-5
RWKV_CHUNK = 64
RWKV_CHAINS = 16
MLSTM_HEADS = 4
MLSTM_HEADS_PER_STEP = 2
MLSTM_GATE_CAP = 15.0
DSA_HEADS = 16
DSA_HEAD_DIM = 128
IDX_HEADS = 16
IDX_DIM = 64
DSA_TOPK_MAX = 256
T5_BUCKETS = 32
T5_MAX_DISTANCE = 128


def _cparams(*sem):
    return pltpu.CompilerParams(dimension_semantics=sem, vmem_limit_bytes=V7X_VMEM_LIMIT_BYTES)


def _dot(a, b):
    return jnp.dot(a.astype(BF16), b.astype(BF16), preferred_element_type=F32)


def _dot_nt(a, b):
    return lax.dot_general(a.astype(BF16), b.astype(BF16), (((1,), (1,)), ((), ())),
                           preferred_element_type=F32)


def _dot_tn(a, b):
    return lax.dot_general(a.astype(BF16), b.astype(BF16), (((0,), (0,)), ((), ())),
                           preferred_element_type=F32)


def _split3(x):
    hi = x.astype(BF16)
    r1 = x - hi.astype(F32)
    mid = r1.astype(BF16)
    lo = (r1 - mid.astype(F32)).astype(BF16)
    return hi, mid, lo


def _dot_exact_lhs(a_bf16, x):
    hi, mid, lo = _split3(x)
    d = lambda p: jnp.dot(a_bf16, p, preferred_element_type=F32)
    return d(hi) + (d(mid) + d(lo))


def _rms(x, g):
    ms = jnp.mean(x * x, axis=-1, keepdims=True)
    return x * lax.rsqrt(ms + RMS_EPS) * g


def _sigmoid(x):
    return 1.0 / (1.0 + jnp.exp(-x))


def _norm_mm_kernel(x_ref, g_ref, w_ref, *rest):
    h_ref = rest[-1]
    has_side = len(rest) == 4
    o_ref = rest[1] if has_side else rest[0]

    @pl.when(pl.program_id(1) == 0)
    def _():
        h_ref[...] = _rms(x_ref[...], g_ref[...]).astype(BF16)
        if has_side:
            rest[2][...] = jnp.dot(h_ref[...], rest[0][...], preferred_element_type=F32)

    o_ref[...] = jnp.dot(h_ref[...], w_ref[...], preferred_element_type=F32).astype(o_ref.dtype)


def _norm_mm(x, g, w, out_dtype=F32, w_side=None, tm=1024, tn=512):
    m, k = x.shape
    n = w.shape[1]
    tm, tn = min(tm, m), min(tn, n)
    assert m % tm == 0 and n % tn == 0
    in_specs = [pl.BlockSpec((tm, k), lambda i, j: (i, 0)),
                pl.BlockSpec((1, k), lambda i, j: (0, 0)),
                pl.BlockSpec((k, tn), lambda i, j: (0, j))]
    out_specs = [pl.BlockSpec((tm, tn), lambda i, j: (i, j))]
    out_shape = [jax.ShapeDtypeStruct((m, n), out_dtype)]
    args = [x, g.reshape(1, k), w]
    if w_side is not None:
        ns = w_side.shape[1]
        in_specs.append(pl.BlockSpec((k, ns), lambda i, j: (0, 0)))
        out_specs.append(pl.BlockSpec((tm, ns), lambda i, j: (i, 0)))
        out_shape.append(jax.ShapeDtypeStruct((m, ns), F32))
        args.append(w_side)
    out = pl.pallas_call(
        _norm_mm_kernel, grid=(m // tm, n // tn), in_specs=in_specs, out_specs=out_specs, out_shape=out_shape,
        scratch_shapes=[pltpu.VMEM((tm, k), BF16)],
        compiler_params=_cparams("parallel", "arbitrary"), name="norm_mm")(*args)
    return out if w_side is not None else out[0]


def _ffn_kernel(x_ref, y_ref, wo_ref, g_ref, wg_ref, wu_ref, wd_ref, o_ref, h_ref):
    @pl.when(pl.program_id(1) == 0)
    def _():
        x1 = x_ref[...] + jnp.dot(y_ref[...], wo_ref[...], preferred_element_type=F32)
        h_ref[...] = _rms(x1, g_ref[...]).astype(BF16)
        o_ref[...] = x1

    h = h_ref[...]
    gate = jnp.dot(h, wg_ref[...], preferred_element_type=F32)
    up = jnp.dot(h, wu_ref[...], preferred_element_type=F32)
    act = (gate * _sigmoid(gate) * up).astype(BF16)
    o_ref[...] += jnp.dot(act, wd_ref[...], preferred_element_type=F32)


def _proj_ffn(x, y, wo, g, wg, wu, wd, tm=512, tf=512):
    m, d = x.shape
    dy = y.shape[1]
    f = wg.shape[1]
    tm, tf = min(tm, m), min(tf, f)
    assert m % tm == 0 and f % tf == 0
    return pl.pallas_call(
        _ffn_kernel, grid=(m // tm, f // tf),
        in_specs=[pl.BlockSpec((tm, d), lambda i, j: (i, 0)),
                  pl.BlockSpec((tm, dy), lambda i, j: (i, 0)),
                  pl.BlockSpec((dy, d), lambda i, j: (0, 0), pipeline_mode=pl.Buffered(1)),
                  pl.BlockSpec((1, d), lambda i, j: (0, 0)),
                  pl.BlockSpec((d, tf), lambda i, j: (0, j)),
                  pl.BlockSpec((d, tf), lambda i, j: (0, j)),
                  pl.BlockSpec((tf, d), lambda i, j: (j, 0))],
        out_specs=pl.BlockSpec((tm, d), lambda i, j: (i, 0)),
        out_shape=jax.ShapeDtypeStruct((m, d), F32),
        scratch_shapes=[pltpu.VMEM((tm, d), BF16)],
        compiler_params=_cparams("parallel", "arbitrary"), name="proj_ffn")(
            x, y, wo, g.reshape(1, d), wg, wu, wd)


def _rwkv_prep_kernel(*refs, seq, tm, has_v):
    if has_v:
        (x_ref, xp_ref, g_ref, mu_ref, w0_ref, w1_ref, w2_ref, a0_ref, a1_ref, a2_ref,
         g1_ref, g2_ref, v0_ref, v1_ref, v2_ref, xs_ref, lw_ref, a_ref, gate_ref, vg_ref) = refs
    else:
        (x_ref, xp_ref, g_ref, mu_ref, w0_ref, w1_ref, w2_ref, a0_ref, a1_ref, a2_ref,
         g1_ref, g2_ref, xs_ref, lw_ref, a_ref, gate_ref) = refs
    i = pl.program_id(0)
    gn = g_ref[...]
    h = _rms(x_ref[...], gn)
    hp = _rms(xp_ref[...], gn)
    seq_start = (i * tm) % seq == 0
    hp_row = jnp.where(seq_start, 0.0, hp[7:8, :])
    row = lax.broadcasted_iota(jnp.int32, (tm, 1), 0)
    h_prev = jnp.where(row == 0, hp_row, pltpu.roll(h, 1, 0))
    xx = h_prev - h
    mix = lambda n: h + xx * mu_ref[n:n + 1, :]
    xs_ref[0] = mix(0).astype(BF16)
    xs_ref[1] = mix(2).astype(BF16)
    xv = mix(3).astype(BF16)
    xs_ref[2] = xv
    lw_ref[...] = -RWKV_DECAY_SCALE * _sigmoid(
        w0_ref[...] + _dot(jnp.tanh(_dot(mix(1), w1_ref[...])), w2_ref[...]))
    a_ref[...] = _sigmoid(a0_ref[...] + _dot(_dot(mix(4), a1_ref[...]), a2_ref[...])).astype(a_ref.dtype)
    gate_ref[...] = _dot(_sigmoid(_dot(mix(5), g1_ref[...])), g2_ref[...]).astype(gate_ref.dtype)
    if has_v:
        vg_ref[...] = _sigmoid(v0_ref[...] + _dot(_dot(xv, v1_ref[...]), v2_ref[...])).astype(vg_ref.dtype)


def _pad_lora(w_in, w_out):
    r = w_in.shape[1]
    rp = -(-r // LANES) * LANES
    return (jnp.pad(w_in, ((0, 0), (0, rp - r))).astype(BF16),
            jnp.pad(w_out, ((0, rp - r), (0, 0))).astype(BF16))


def _rwkv_prep(x, norm_g, mu, w0, w1, w2, a0, a1, a2, g1, g2, vres, seq, tm=256):
    t, d = x.shape
    tm = min(tm, seq)
    assert t % tm == 0 and seq % tm == 0 and tm % 8 == 0
    has_v = vres is not None
    row = lambda v: v.reshape(1, d)
    full = lambda a: pl.BlockSpec(a.shape, lambda i: (0,) * a.ndim)
    w1p, w2p = _pad_lora(w1, w2)
    a1p, a2p = _pad_lora(a1, a2)
    g1p, g2p = _pad_lora(g1, g2)
    mu8 = jnp.pad(mu, ((0, 2), (0, 0)))
    params = [row(norm_g), mu8, row(w0), w1p, w2p, row(a0), a1p, a2p, g1p, g2p]
    if has_v:
        v1p, v2p = _pad_lora(vres[1], vres[2])
        params += [row(vres[0]), v1p, v2p]
    tile = pl.BlockSpec((tm, d), lambda i: (i, 0))
    in_specs = [tile, pl.BlockSpec((8, d), lambda i: (jnp.maximum(i * (tm // 8) - 1, 0), 0))]
    in_specs += [full(p) for p in params]
    n_gates = 3 if has_v else 2
    out_shape = ([jax.ShapeDtypeStruct((3, t, d), BF16), jax.ShapeDtypeStruct((t, d), F32)]
                 + [jax.ShapeDtypeStruct((t, d), BF16)] * n_gates)
    out_specs = [pl.BlockSpec((3, tm, d), lambda i: (0, i, 0))] + [tile] * (1 + n_gates)
    return pl.pallas_call(
        functools.partial(_rwkv_prep_kernel, seq=seq, tm=tm, has_v=has_v),
        grid=(t // tm,), in_specs=in_specs, out_specs=out_specs, out_shape=out_shape,
        compiler_params=_cparams("parallel"), name="rwkv_prep")(x, x, *params)


def _seg_sum(x, seg):
    w = x.shape[1]
    hi = x.astype(BF16)
    lo = (x - hi.astype(F32)).astype(BF16)
    d = lambda p, q: jnp.dot(p[:, q:q + 256], seg, preferred_element_type=F32)
    return jnp.concatenate([d(hi, q) + d(lo, q) for q in range(0, w, 256)], axis=1)


def _rwkv_scan_kernel(*refs, L, tc, has_v):
    (s_ref, a2_ref, r2_ref, b2_ref, k2_ref, v2_ref, pl_ref, phi_ref, psi_ref, theta_ref, yloc_ref,
     y_ref, bonus_ref, mab_ref, tinv_ref, mak_ref, mrb_ref, mrk_ref, av_ref) = refs[-19:]
    refs = refs[:-19]
    if has_v:
        (xs_ref, w_ref, lw_ref, a_ref, gate_ref, vf_ref, vg_ref,
         kk_ref, ka_ref, rk_ref, lng_ref, lnb_ref, o_ref) = refs
    else:
        (xs_ref, w_ref, lw_ref, a_ref, gate_ref,
         kk_ref, ka_ref, rk_ref, lng_ref, lnb_ref, o_ref, vout_ref) = refs
    W = lw_ref.shape[-1]
    P2 = 2 * L
    npair = W // P2
    N = RWKV_HEAD

    @pl.when(pl.program_id(2) == 0)
    def _():
        s_ref[...] = jnp.zeros_like(s_ref)

    ri = lax.broadcasted_iota(jnp.int32, (P2, P2), 0)
    ci = lax.broadcasted_iota(jnp.int32, (P2, P2), 1)
    strict = ri > ci
    incl = ri >= ci
    eye = (ri == ci).astype(F32)
    levels = []
    s = 1
    while s < L:
        levels.append(((ri // s) % 2 == 1) & ((ci // s) == (ri // s) - 1))
        s *= 2
    head0 = lax.broadcasted_iota(jnp.int32, (L, P2), 1) < N
    sr = lax.broadcasted_iota(jnp.int32, (256, 256), 0) // N
    sc = lax.broadcasted_iota(jnp.int32, (256, 256), 1) // N
    seg = (sr == sc).astype(BF16)

    nc = tc // L
    tr = lax.broadcasted_iota(jnp.int32, (tc, tc), 0)
    tcol = lax.broadcasted_iota(jnp.int32, (tc, tc), 1)
    tri = jnp.where(tr >= tcol, jnp.where(tr // L == tcol // L, 1.0, 0.0), 0.0).astype(BF16)
    del tr, tcol

    r, k, v = (jnp.dot(xs_ref[n], w_ref[n], preferred_element_type=F32) for n in range(3))
    lw = lw_ref[...]
    a = a_ref[...].astype(F32)
    if has_v:
        v = v + (vf_ref[...] - v) * vg_ref[...].astype(F32)
    else:
        vout_ref[...] = v
    kk = k * kk_ref[...]
    kk = kk / jnp.maximum(jnp.sqrt(_seg_sum(kk * kk, seg)), 1e-12)
    kmod = k * (1.0 + (a - 1.0) * ka_ref[...])
    c = _dot_exact_lhs(tri, lw)
    enc = jnp.exp(-c)
    bonus_ref[...] = _seg_sum(r * kmod * rk_ref[...], seg) * v
    operands = (-kk * jnp.exp(c - lw), r * jnp.exp(c), kk * a * enc, kmod * enc, v)
    for z, z_ref in zip(operands, (a2_ref, r2_ref, b2_ref, k2_ref, v2_ref)):
        for ch in range(nc):
            for p in range(npair):
                zz = z[ch * L:(ch + 1) * L, P2 * p:P2 * (p + 1)]
                z_ref[ch, p, :L] = jnp.where(head0, zz, 0.0).astype(BF16)
                z_ref[ch, p, L:] = jnp.where(head0, 0.0, zz).astype(BF16)
    for ch in range(nc):
        pl_ref[ch] = jnp.broadcast_to(jnp.exp(c[(ch + 1) * L - 1:(ch + 1) * L, :]), (8, W))
    del r, k, v, lw, a, kk, kmod, c, enc, operands

    chains = [(ch, p) for ch in range(nc) for p in range(npair)]
    for g0 in range(0, len(chains), RWKV_CHAINS):
        group = chains[g0:g0 + RWKV_CHAINS]
        for c in group:
            g = _dot_nt(jnp.concatenate([a2_ref[c], r2_ref[c]], axis=0),
                        jnp.concatenate([b2_ref[c], k2_ref[c]], axis=0))
            m_ab = jnp.where(strict, g[:P2, :P2], 0.0)
            mab_ref[c] = m_ab
            tinv_ref[c] = eye + jnp.where(levels[0], m_ab, 0.0)
            mak_ref[c] = jnp.where(strict, g[:P2, P2:], 0.0).astype(BF16)
            mrb_ref[c] = jnp.where(incl, g[P2:, :P2], 0.0).astype(BF16)
            mrk_ref[c] = jnp.where(incl, g[P2:, P2:], 0.0).astype(BF16)
        for lv in levels[1:]:
            steps = [_dot(jnp.where(lv, mab_ref[c], 0.0), tinv_ref[c]).astype(BF16) for c in group]
            for c, step in zip(group, steps):
                t_cur = tinv_ref[c]
                tinv_ref[c] = t_cur + _dot(t_cur, step)
        mvs = [_dot(mak_ref[c], v2_ref[c]).astype(BF16) for c in group]
        for c, mv in zip(group, mvs):
            av_ref[c] = _dot(tinv_ref[c], jnp.concatenate([a2_ref[c], mv], axis=1)).astype(BF16)
        ths = [_dot(mrb_ref[c], av_ref[c]) for c in group]
        yls = [_dot(mrk_ref[c], v2_ref[c]) for c in group]
        for c, th, yl in zip(group, ths, yls):
            theta_ref[c] = (r2_ref[c].astype(F32) + th[:, :P2]).astype(BF16)
            yloc_ref[c] = th[:, P2:] + yl
        for c in group:
            av = av_ref[c]
            pp = _dot_tn(jnp.concatenate([av[:, P2:], av[:, :P2]], axis=1), b2_ref[c])
            p_last = pl_ref[c[0]][0:1, P2 * c[1]:P2 * (c[1] + 1)]
            phi_ref[c] = ((eye + pp[P2:]) * p_last).astype(BF16)
            psi_ref[c] = (pp[:P2] + _dot_tn(v2_ref[c], k2_ref[c])) * p_last

    for ch in range(nc):
        for p in range(npair):
            s0 = s_ref[p]
            yo = _dot_nt(theta_ref[ch, p], s0) + yloc_ref[ch, p]
            y_ref[ch * L:(ch + 1) * L, P2 * p:P2 * (p + 1)] = yo[:L] + yo[L:]
            s_ref[p] = _dot(s0, phi_ref[ch, p]) + psi_ref[ch, p]

    y = y_ref[...]
    mean = _seg_sum(y, seg) * (1.0 / N)
    yc = y - mean
    var = _seg_sum(yc * yc, seg) * (1.0 / N)
    yn = yc * lax.rsqrt(var + RWKV_GN_EPS) * lng_ref[...] + lnb_ref[...]
    o_ref[...] = ((yn + bonus_ref[...]) * gate_ref[...].astype(F32)).astype(o_ref.dtype)


def _rwkv_scan(xs, w_rkv, lw, a, gate, v_first, vgate, k_k, k_a, r_k, lnx_g, lnx_b, bsz, seq, tc=256, wb=512):
    _, t, d = xs.shape
    L = RWKV_CHUNK
    tc, wb = min(tc, seq), min(wb, d)
    assert seq % tc == 0 and tc % L == 0 and d % wb == 0 and wb % 256 == 0
    has_v = v_first is not None
    nc, p2, npair = tc // L, 2 * L, wb // (2 * L)
    b3 = lambda z: z.reshape(bsz, seq, d)
    blk = pl.BlockSpec((None, tc, wb), lambda b, h, c: (b, c, h))
    prow = pl.BlockSpec((1, wb), lambda b, h, c: (0, h))
    args = [xs.reshape(3, bsz, seq, d), w_rkv, b3(lw), b3(a), b3(gate)]
    in_specs = [pl.BlockSpec((3, None, tc, d), lambda b, h, c: (0, b, c, 0)),
                pl.BlockSpec((3, d, wb), lambda b, h, c: (0, 0, h)),
                blk, blk, blk]
    if has_v:
        args += [b3(v_first), b3(vgate)]
        in_specs += [blk, blk]
    args += [z.reshape(1, d) for z in (k_k, k_a, r_k, lnx_g, lnx_b)]
    in_specs += [prow] * 5
    out_shape = [jax.ShapeDtypeStruct((bsz, seq, d), BF16)]
    if not has_v:
        out_shape.append(jax.ShapeDtypeStruct((bsz, seq, d), F32))
    out = pl.pallas_call(
        functools.partial(_rwkv_scan_kernel, L=L, tc=tc, has_v=has_v),
        grid=(bsz, d // wb, seq // tc), in_specs=in_specs, out_specs=[blk] * len(out_shape),
        out_shape=out_shape,
        scratch_shapes=[pltpu.VMEM((npair, p2, p2), F32)]
        + [pltpu.VMEM((nc, npair, p2, p2), BF16)] * 5
        + [pltpu.VMEM((nc, 8, wb), F32),
           pltpu.VMEM((nc, npair, p2, p2), BF16), pltpu.VMEM((nc, npair, p2, p2), F32),
           pltpu.VMEM((nc, npair, p2, p2), BF16), pltpu.VMEM((nc, npair, p2, p2), F32),
           pltpu.VMEM((tc, wb), F32), pltpu.VMEM((tc, wb), F32),
           pltpu.VMEM((nc, npair, p2, p2), F32), pltpu.VMEM((nc, npair, p2, p2), F32),
           pltpu.VMEM((nc, npair, p2, p2), BF16), pltpu.VMEM((nc, npair, p2, p2), BF16),
           pltpu.VMEM((nc, npair, p2, p2), BF16),
           pltpu.VMEM((nc, npair, p2, 2 * p2), BF16)],
        compiler_params=_cparams("parallel", "parallel", "arbitrary"), name="rwkv_scan")(*args)
    return out[0].reshape(t, d), (None if has_v else out[1].reshape(t, d))


def _rwkv_layer(x, norm_g, p, vres, v_first, bsz, seq):
    (mu, w_rkv, w0, w1, w2, a0, a1, a2, g1, g2, k_k, k_a, r_k, lnx_g, lnx_b, w_o) = p
    outs = _rwkv_prep(x, norm_g, mu, w0, w1, w2, a0, a1, a2, g1, g2, vres, seq)
    xs, lw, a, gate = outs[:4]
    vgate = outs[4] if vres is not None else None
    o, v_own = _rwkv_scan(xs, w_rkv.astype(BF16), lw, a, gate, v_first if vres is not None else None, vgate,
                          k_k, k_a, r_k, lnx_g, lnx_b, bsz, seq)
    return o, w_o.astype(BF16), (v_own if vres is None else v_first)


def _pick_tile(n, cap):
    best = LANES
    for tile in range(LANES, cap + 1, LANES):
        if n % tile == 0:
            best = tile
    return best


def _mlstm_kernel(q_ref, k_ref, v_ref, o_ref, gt_ref, bias_ref, ng_ref, out_ref, c_ref, m_ref, *, nh, hps):
    lc = q_ref.shape[0]
    dk = q_ref.shape[1] // hps
    dv = v_ref.shape[1] // hps

    @pl.when(pl.program_id(2) == 0)
    def _():
        c_ref[...] = jnp.zeros_like(c_ref)
        m_ref[...] = jnp.zeros_like(m_ref)

    lane = lax.broadcasted_iota(jnp.int32, (lc, LANES), 1)
    z = gt_ref[...] + bias_ref[...]
    zc = MLSTM_GATE_CAP * jnp.tanh(z / MLSTM_GATE_CAP)
    lf_all = jnp.minimum(zc, 0.0) - jnp.log1p(jnp.exp(-jnp.abs(zc)))
    rr = lax.broadcasted_iota(jnp.int32, (lc, lc), 0)
    cc = lax.broadcasted_iota(jnp.int32, (lc, lc), 1)
    causal = rr >= cc
    bcum_all = _dot_exact_lhs(causal.astype(BF16), lf_all)
    comb = jnp.where(lane < nh, zc, bcum_all)
    er = lax.broadcasted_iota(jnp.int32, (8, LANES), 0)
    ec = lax.broadcasted_iota(jnp.int32, (8, LANES), 1)
    head_of_row = pl.program_id(1) * hps + er // 2
    sel = jnp.where(er < 2 * hps, jnp.where(ec == head_of_row + nh * (er % 2), 1.0, 0.0), 0.0).astype(BF16)
    hi, mid, lo = _split3(comb)
    tr = lambda p: lax.dot_general(sel, p, (((1,), (1,)), ((), ())), preferred_element_type=F32)
    rows = tr(hi) + (tr(mid) + tr(lo))
    ones_blk = (lane == 0).astype(BF16)

    for s in range(hps):
        h = pl.program_id(1) * hps + s
        li_col = jnp.sum(jnp.where(lane == h, comb, 0.0), axis=-1, keepdims=True)
        bc_col = jnp.sum(jnp.where(lane == h + nh, comb, 0.0), axis=-1, keepdims=True)
        li_row, bc_row = rows[2 * s:2 * s + 1], rows[2 * s + 1:2 * s + 2]
        m_st = m_ref[s, 0:1, 0:1]
        dmat = jnp.where(causal, bc_col - bc_row + li_row, -jnp.inf)
        inter = bc_col + m_st
        m_t = jnp.maximum(inter, jnp.max(dmat, axis=-1, keepdims=True))
        q = q_ref[:, s * dk:(s + 1) * dk] * (dk ** -0.5)
        k = k_ref[:, s * dk:(s + 1) * dk]
        sc = _dot_nt(q, k) * jnp.exp(dmat - m_t)
        w_inter = jnp.exp(inter - m_t)
        v_ext = jnp.concatenate([v_ref[:, s * dv:(s + 1) * dv].astype(BF16), ones_blk], axis=1)
        c_st = c_ref[s]
        nd = _dot(sc, v_ext) + w_inter * _dot(q, c_st)
        den = nd[:, dv:dv + 1]
        hc = nd[:, :dv] / jnp.maximum(jnp.abs(den), jnp.exp(-m_t))

        b_tot = bc_col[lc - 1:lc]
        log_wk = b_tot - bc_col + li_col
        m_new = jnp.maximum(b_tot + m_st, jnp.max(log_wk, axis=0, keepdims=True))
        c_ref[s] = jnp.exp(b_tot + m_st - m_new) * c_st + _dot_tn(k * jnp.exp(log_wk - m_new), v_ext)
        m_ref[s] = jnp.broadcast_to(m_new, m_ref.shape[1:])

        hn = hc * lax.rsqrt(jnp.mean(hc * hc, axis=-1, keepdims=True) + RMS_EPS) * ng_ref[:, s * dv:(s + 1) * dv]
        o_gate = _sigmoid(o_ref[:, s * dv:(s + 1) * dv].astype(F32))
        out_ref[:, s * dv:(s + 1) * dv] = (hn * o_gate).astype(out_ref.dtype)


def _mlstm_layer(x, mix_g, w_in, b_if, norm_g, w_o, bsz, seq, lc=256):
    t, d = x.shape
    nh = MLSTM_HEADS
    dk, dv = d // 2 // nh, d // nh
    nq = 2 * nh * dk + 2 * nh * dv
    assert w_in.shape[1] == nq + 2 * nh and dk % LANES == 0
    lc = min(lc, seq)
    assert seq % lc == 0
    w_gates = jnp.pad(w_in[:, nq:], ((0, 0), (0, LANES - 2 * nh))).astype(BF16)
    proj, gates = _norm_mm(x, mix_g, w_in[:, :nq].astype(BF16), out_dtype=BF16, w_side=w_gates,
                           tn=_pick_tile(nq, 1024))
    proj = proj.reshape(bsz, seq, nq)
    gates = gates.reshape(bsz, seq, LANES)
    bias = jnp.zeros((1, LANES), F32).at[0, :nh].set(b_if[0]).at[0, nh:2 * nh].set(b_if[1])
    hps = MLSTM_HEADS_PER_STEP
    assert nh % hps == 0
    ng = nh // hps
    out = pl.pallas_call(
        functools.partial(_mlstm_kernel, nh=nh, hps=hps),
        grid=(bsz, ng, seq // lc),
        in_specs=[pl.BlockSpec((None, lc, hps * dk), lambda b, h, c: (b, c, h)),
                  pl.BlockSpec((None, lc, hps * dk), lambda b, h, c: (b, c, ng + h)),
                  pl.BlockSpec((None, lc, hps * dv), lambda b, h, c: (b, c, ng + h)),
                  pl.BlockSpec((None, lc, hps * dv), lambda b, h, c: (b, c, 2 * ng + h)),
                  pl.BlockSpec((None, lc, LANES), lambda b, h, c: (b, c, 0)),
                  pl.BlockSpec((1, LANES), lambda b, h, c: (0, 0)),
                  pl.BlockSpec((1, hps * dv), lambda b, h, c: (0, h))],
        out_specs=pl.BlockSpec((None, lc, hps * dv), lambda b, h, c: (b, c, h)),
        out_shape=jax.ShapeDtypeStruct((bsz, seq, nh * dv), BF16),
        scratch_shapes=[pltpu.VMEM((hps, dk, dv + LANES), F32), pltpu.VMEM((hps, 8, LANES), F32)],
        compiler_params=_cparams("parallel", "parallel", "arbitrary"), name="mlstm")(
            proj, proj, proj, proj, gates, bias, norm_g.reshape(1, nh * dv))
    return out.reshape(t, nh * dv), w_o.astype(BF16)


DSA_TQ = 128
DSA_TK = 256
DSA_HEAD_GROUP = 4
DSA_TILES_PER_TRIP = 4
DSA_PROJ_GROUPS = 4
DSA_NEG = -1e30
KEY_NEG_INF = -2139095041
IDX_BIG = 2 ** 30
LOG2E = math.log2(math.e)


def _dsa_proj_kernel(x_ref, g_ref, w_ref, qg_ref, kg_ref, qn_ref, qi_ref, k_ref, v_ref, ki_ref, wi_ref, h_ref,
                     *, nh, nih):
    dh = DSA_HEAD_DIM
    gs = DSA_PROJ_GROUPS
    j = pl.program_id(2)
    nq, nqi = nh // gs, nih // gs

    @pl.when(j == 0)
    def _():
        h_ref[...] = _rms(x_ref[...], g_ref[...]).astype(BF16)

    y = jnp.dot(h_ref[...], w_ref[...], preferred_element_type=F32)
    part = lambda n: y[:, n * dh:(n + 1) * dh]

    @pl.when(j < nq)
    def _():
        qscale = dh ** -0.5 * LOG2E
        for n in range(gs):
            qn_ref[n] = (_rms(part(n), qg_ref[...]) * qscale).astype(BF16)

    @pl.when(jnp.logical_and(j >= nq, j < nq + nqi))
    def _():
        for n in range(gs):
            qi_ref[n] = part(n).astype(BF16)

    @pl.when(j == nq + nqi)
    def _():
        k_ref[...] = _rms(part(0), kg_ref[...]).astype(BF16)
        v_ref[:, :dh] = part(1).astype(BF16)
        v_ref[:, dh:] = jnp.ones((v_ref.shape[0], LANES), BF16)
        ki_ref[...] = part(2).astype(BF16)
        wi_ref[...] = part(3)


def _dsa_kernel(qn_ref, qi_ref, wi_ref, k_ref, v_ref, ki_ref, nb_ref, o_ref,
                key_ref, hi_ref, lo_ref, lo2_ref, w_ref, acc_ref, m_ref, *, nh, nih, n_sel, idx_bits):
    tq, tk, dh = DSA_TQ, DSA_TK, DSA_HEAD_DIM
    hg = DSA_HEAD_GROUP
    q0 = pl.program_id(1) * tq
    jd = (q0 + tq - 1) // tk
    nt = jd + 1
    rowi = lax.broadcasted_iota(jnp.int32, (tq, tk), 0)
    coli = lax.broadcasted_iota(jnp.int32, (tq, tk), 1)
    ktile = lambda jt: pl.ds(pl.multiple_of(jt * tk, tk), tk)
    twice = lambda z: jnp.concatenate([z] * (tk // LANES), axis=1)

    wi = wi_ref[...]
    for h in range(nih):
        w_ref[h] = jnp.broadcast_to(wi[:, h:h + 1], (tq, LANES))

    def for_each_tile(n, body, per_trip=DSA_TILES_PER_TRIP):
        def trip(i, c):
            for u in range(per_trip):
                body(per_trip * i + u)
            return c
        lax.fori_loop(0, n // per_trip, trip, 0)
        done = (n // per_trip) * per_trip
        for u in range(per_trip - 1):
            @pl.when(n - done > u)
            def _():
                body(done + u)

    def score_tile(jt):
        ki_t = ki_ref[ktile(jt), :]
        score = jnp.zeros((tq, tk), F32)
        for g0 in range(0, nih, hg):
            lg = _dot_nt(qi_ref[g0:g0 + hg].reshape(hg * tq, LANES), ki_t)
            for h in range(hg):
                score = score + jnp.maximum(lg[h * tq:(h + 1) * tq], 0.0) * twice(w_ref[g0 + h])
        score = jnp.where(jt * tk + coli <= q0 + rowi, score, -jnp.inf)
        bits = pltpu.bitcast(score, jnp.int32)
        key = bits ^ ((bits >> 31) & 0x7FFFFFFF)
        key_ref[jt] = key
        key_t = pltpu.bitcast(jnp.transpose(pltpu.bitcast(key, F32)), jnp.int32)
        hi_ref[jt] = (key_t >> 16).astype(jnp.int16)
        lo_ref[jt] = ((key_t & 0xFFFF) - 32768).astype(jnp.int16)

    for_each_tile(nt, score_tile)

    unroll = DSA_TILES_PER_TRIP
    n_trips = (nt + unroll - 1) // unroll
    floor16 = jnp.full((tk, tq), -32768, jnp.int16)

    def pad_tiles(ref):
        for u in range(unroll - 1):
            @pl.when(n_trips * unroll - nt > u)
            def _():
                ref[nt + u] = floor16

    def count16(ref, cand):
        cb = jnp.broadcast_to(cand.astype(jnp.int16), (32, tq))

        def body(i, acc):
            for u in range(unroll):
                tile = ref[unroll * i + u]
                for r0 in range(0, tk, 32):
                    acc = acc + jnp.where(tile[r0:r0 + 32] >= cb, jnp.int16(1), jnp.int16(0))
            return acc
        acc = lax.fori_loop(0, n_trips, body, jnp.zeros((32, tq), jnp.int16))
        return jnp.sum(acc.astype(F32), axis=0, keepdims=True)

    pad_tiles(hi_ref)

    def kth_largest16(ref, kth):
        def bit(b, prefix):
            cand = prefix + jnp.left_shift(jnp.int32(1), 15 - b)
            return jnp.where(count16(ref, cand) >= kth, cand, prefix)
        return lax.fori_loop(0, 16, bit, jnp.full((1, tq), -32768, jnp.int32))

    hi_k = kth_largest16(hi_ref, n_sel)
    above = jnp.where(hi_k >= 32767, 0.0, count16(hi_ref, jnp.minimum(hi_k + 1, 32767)))
    hi_kb = jnp.broadcast_to(hi_k.astype(jnp.int16), (tk, tq))

    def bucket_tile(jt, c):
        lo2_ref[jt] = jnp.where(hi_ref[jt] == hi_kb, lo_ref[jt], floor16)
        return c

    lax.fori_loop(0, nt, bucket_tile, 0)
    pad_tiles(lo2_ref)
    lo_k = kth_largest16(lo2_ref, n_sel - above)
    thr_row = hi_k * 65536 + (lo_k + 32768)
    tw = pltpu.bitcast(jnp.transpose(pltpu.bitcast(jnp.broadcast_to(thr_row, (tq, tq)), F32)), jnp.int32)
    thr = tw[:, :1]

    def count(hit):
        def body(jt, acc):
            keyt = key_ref[jt]
            for c0 in range(0, tk, LANES):
                acc = acc + hit(keyt[:, c0:c0 + LANES], jt * tk + c0)
            return acc
        acc = lax.fori_loop(0, nt, body, jnp.zeros((tq, LANES), F32))
        return jnp.sum(acc, axis=-1, keepdims=True)

    wide = lambda col: jnp.broadcast_to(col, (tq, LANES))
    lane = lax.broadcasted_iota(jnp.int32, (tq, LANES), 1)

    c_gt = count(lambda kv, base: jnp.where(kv > tw, 1.0, 0.0))
    c_ge = count(lambda kv, base: jnp.where(kv >= tw, 1.0, 0.0))
    need = n_sel - c_gt
    c_eq = c_ge - c_gt

    def tie_search():
        def index_bit(b, jcur):
            cand = jcur + jnp.left_shift(jnp.int32(1), idx_bits - 1 - b)
            cw = wide(cand)
            f = count(lambda kv, base: jnp.where(kv == tw, jnp.where(base + lane < cw, 1.0, 0.0), 0.0))
            return jnp.where(f <= need, cand, jcur)
        return lax.fori_loop(0, idx_bits, index_bit, jnp.zeros((tq, 1), jnp.int32))

    excess = jnp.max(c_eq - need) > 0.0
    jc = lax.cond(excess, tie_search, lambda: jnp.full((tq, 1), IDX_BIG, jnp.int32))
    jc = jnp.where(c_eq > need, jc, IDX_BIG)
    jc = jnp.where(thr == KEY_NEG_INF, 0, jc)

    m_ref[...] = jnp.full_like(m_ref, DSA_NEG)
    acc_ref[...] = jnp.zeros_like(acc_ref)

    def attend_tile(jt, near_idx):
        k_t = k_ref[ktile(jt), :]
        v_t = v_ref[ktile(jt), :]
        keyt = key_ref[jt]
        tie_mb = jnp.where(keyt == thr, jnp.where(jt * tk + coli < jc, 0.0, DSA_NEG), DSA_NEG)
        mb = jnp.where(keyt > thr, 0.0, tie_mb)
        for g0 in range(0, nh, hg):
            s_g = _dot_nt(qn_ref[g0:g0 + hg].reshape(hg * tq, dh), k_t)
            ps, alphas = [], []
            for h in range(hg):
                rows = slice((g0 + h) * tq, (g0 + h + 1) * tq)
                s = s_g[h * tq:(h + 1) * tq] + (mb if near_idx is None else nb_ref[near_idx, g0 + h] + mb)
                m_old = m_ref[rows, :]
                m_new = jnp.maximum(m_old, jnp.max(s, axis=-1, keepdims=True))
                m_ref[rows, :] = m_new
                ps.append(jnp.exp2(s - twice(m_new)).astype(BF16))
                alphas.append(jnp.exp2(m_old - m_new))
            pv = jnp.dot(jnp.concatenate(ps, axis=0), v_t, preferred_element_type=F32)
            rows_g = slice(g0 * tq, (g0 + hg) * tq)
            acc_ref[rows_g, :] = acc_ref[rows_g, :] * twice(jnp.concatenate(alphas, axis=0)) + pv

    on_tile_edge = q0 == jd * tk
    has_prev_near = jnp.logical_and(on_tile_edge, jd >= 1)
    n_far = jnp.where(has_prev_near, jd - 1, jd)

    for_each_tile(n_far, lambda jt: attend_tile(jt, None))

    @pl.when(has_prev_near)
    def _():
        attend_tile(jd - 1, 2)
        attend_tile(jd, 0)

    @pl.when(jnp.logical_not(has_prev_near))
    def _():
        attend_tile(jd, jnp.where(on_tile_edge, 0, 1))

    for h in range(nh):
        rows = slice(h * tq, (h + 1) * tq)
        o_ref[:, h * dh:(h + 1) * dh] = (acc_ref[rows, :dh] / acc_ref[rows, dh:]).astype(o_ref.dtype)


def _t5_bucket(rel):
    n = jnp.maximum(rel, 0)
    exact = T5_BUCKETS // 2
    nf = jnp.maximum(n, exact).astype(F32)
    large = exact + (jnp.log(nf / exact) / math.log(T5_MAX_DISTANCE / exact)
                     * (T5_BUCKETS - exact)).astype(jnp.int32)
    return jnp.where(n < exact, n, jnp.minimum(large, T5_BUCKETS - 1))


def _dsa_layer(x, mix_g, w_in, q_norm_g, k_norm_g, t5_table, w_o, bsz, seq):
    t, d = x.shape
    nh, dh, nih, di = DSA_HEADS, DSA_HEAD_DIM, IDX_HEADS, IDX_DIM
    tq, tk = DSA_TQ, DSA_TK
    assert seq % (tk * DSA_TILES_PER_TRIP) == 0 and tk == 2 * tq and dh == LANES and di <= LANES and nih <= LANES
    assert T5_MAX_DISTANCE <= tq
    n_sel = min(DSA_TOPK_MAX, seq // 4)
    o1, o2, o3, o4, o5 = nh * dh, nh * dh + dh, nh * dh + 2 * dh, nh * dh + 2 * dh + nih * di, nh * dh + 2 * dh + nih * di + di
    w_qi = jnp.pad(w_in[:, o3:o4].reshape(d, nih, di), ((0, 0), (0, 0), (0, LANES - di))).reshape(d, nih * LANES)
    w_ki = jnp.pad(w_in[:, o4:o5], ((0, 0), (0, LANES - di)))
    w_wi = jnp.pad(w_in[:, o5:o5 + nih] * (nih ** -0.5 * di ** -0.5), ((0, 0), (0, LANES - nih)))
    w = jnp.concatenate([w_in[:, :o1], w_qi, w_in[:, o1:o3], w_ki, w_wi], axis=1).astype(BF16)
    gs = DSA_PROJ_GROUPS
    nq, nqi = nh // gs, nih // gs
    assert gs == 4 and w.shape[1] == gs * dh * (nq + nqi + 1) and nh % gs == 0 and nih % gs == 0
    tm = min(1024, seq)
    assert seq % tm == 0
    tok = lambda width: pl.BlockSpec((None, tm, width), lambda b, i, j: (b, i, 0))
    qn, qi, kn, vb, kib, wis = pl.pallas_call(
        functools.partial(_dsa_proj_kernel, nh=nh, nih=nih),
        grid=(bsz, seq // tm, nq + nqi + 1),
        in_specs=[pl.BlockSpec((None, tm, d), lambda b, i, j: (b, i, 0)),
                  pl.BlockSpec((1, d), lambda b, i, j: (0, 0)),
                  pl.BlockSpec((d, gs * dh), lambda b, i, j: (0, j)),
                  pl.BlockSpec((1, dh), lambda b, i, j: (0, 0)),
                  pl.BlockSpec((1, dh), lambda b, i, j: (0, 0))],
        out_specs=[pl.BlockSpec((None, gs, tm, dh), lambda b, i, j: (b, jnp.minimum(j, nq - 1), i, 0)),
                   pl.BlockSpec((None, gs, tm, LANES), lambda b, i, j: (b, jnp.clip(j - nq, 0, nqi - 1), i, 0)),
                   tok(dh), tok(dh + LANES), tok(LANES), tok(LANES)],
        out_shape=[jax.ShapeDtypeStruct((bsz, nh, seq, dh), BF16),
                   jax.ShapeDtypeStruct((bsz, nih, seq, LANES), BF16),
                   jax.ShapeDtypeStruct((bsz, seq, dh), BF16),
                   jax.ShapeDtypeStruct((bsz, seq, dh + LANES), BF16),
                   jax.ShapeDtypeStruct((bsz, seq, LANES), BF16),
                   jax.ShapeDtypeStruct((bsz, seq, LANES), F32)],
        scratch_shapes=[pltpu.VMEM((tm, d), BF16)],
        compiler_params=_cparams("parallel", "parallel", "arbitrary"), name="dsa_proj")(
            x.reshape(bsz, seq, d), mix_g.reshape(1, d), w, q_norm_g.reshape(1, dh), k_norm_g.reshape(1, dh))

    ii = jnp.arange(tq, dtype=jnp.int32)[:, None]
    jj = jnp.arange(tk, dtype=jnp.int32)[None, :]
    buckets = jnp.stack([_t5_bucket(off + ii - jj) for off in (0, tq, 2 * tq)])
    rel_table = (t5_table - t5_table[T5_BUCKETS - 1]).astype(F32)
    near = jnp.einsum("otkb,bh->ohtk", jax.nn.one_hot(buckets, T5_BUCKETS, dtype=F32), rel_table,
                      precision=lax.Precision.HIGHEST) * LOG2E

    seqblk = lambda: pl.BlockSpec((None, seq, LANES), lambda b, i: (b, 0, 0))
    out = pl.pallas_call(
        functools.partial(_dsa_kernel, nh=nh, nih=nih, n_sel=n_sel, idx_bits=int(seq).bit_length()),
        grid=(bsz, seq // tq),
        in_specs=[pl.BlockSpec((None, nh, tq, dh), lambda b, i: (b, 0, i, 0)),
                  pl.BlockSpec((None, nih, tq, LANES), lambda b, i: (b, 0, i, 0)),
                  pl.BlockSpec((None, tq, LANES), lambda b, i: (b, i, 0)),
                  seqblk(), pl.BlockSpec((None, seq, dh + LANES), lambda b, i: (b, 0, 0)), seqblk(),
                  pl.BlockSpec((3, nh, tq, tk), lambda b, i: (0, 0, 0, 0))],
        out_specs=pl.BlockSpec((None, tq, nh * dh), lambda b, i: (b, i, 0)),
        out_shape=jax.ShapeDtypeStruct((bsz, seq, nh * dh), BF16),
        scratch_shapes=[pltpu.VMEM((seq // tk, tq, tk), jnp.int32),
                        pltpu.VMEM((seq // tk, tk, tq), jnp.int16),
                        pltpu.VMEM((seq // tk, tk, tq), jnp.int16),
                        pltpu.VMEM((seq // tk, tk, tq), jnp.int16),
                        pltpu.VMEM((nih, tq, LANES), F32),
                        pltpu.VMEM((nh * tq, dh + LANES), F32),
                        pltpu.VMEM((nh * tq, LANES), F32)],
        compiler_params=_cparams("parallel", "arbitrary"), name="dsa_attn")(
            qn, qi, wis, kn, vb, kib, near)
    return out.reshape(t, nh * dh), w_o.astype(BF16)


def kernel(x, rwkv_mu, rwkv_w_rkv, rwkv_w0, rwkv_w1, rwkv_w2, rwkv_a0, rwkv_a1, rwkv_a2, rwkv_v0, rwkv_v1,
           rwkv_v2, rwkv_g1, rwkv_g2, rwkv_k_k, rwkv_k_a, rwkv_r_k, rwkv_lnx_g, rwkv_lnx_b, rwkv_w_o,
           mlstm_w_in, mlstm_b_if, mlstm_norm_g, mlstm_w_o, dsa_w_in, dsa_q_norm_g, dsa_k_norm_g, dsa_w_o,
           t5_bias, mix_norm_g, ffn_norm_g, ffn_w_gate, ffn_w_up, ffn_w_down):
    bsz, seq, d = x.shape
    depth = mix_norm_g.shape[0]
    h = x.reshape(bsz * seq, d)
    v_first = None
    for i in range(depth):
        kind, j = i % 3, i // 3
        if kind == 0:
            vres = None if j == 0 else (rwkv_v0[j - 1], rwkv_v1[j - 1], rwkv_v2[j - 1])
            p = (rwkv_mu[j], rwkv_w_rkv[j], rwkv_w0[j], rwkv_w1[j], rwkv_w2[j], rwkv_a0[j], rwkv_a1[j],
                 rwkv_a2[j], rwkv_g1[j], rwkv_g2[j], rwkv_k_k[j], rwkv_k_a[j], rwkv_r_k[j],
                 rwkv_lnx_g[j], rwkv_lnx_b[j], rwkv_w_o[j])
            y, w_o, v_first = _rwkv_layer(h, mix_norm_g[i], p, vres, v_first, bsz, seq)
        elif kind == 1:
            y, w_o = _mlstm_layer(h, mix_norm_g[i], mlstm_w_in[j], mlstm_b_if[j], mlstm_norm_g[j],
                                  mlstm_w_o[j], bsz, seq)
        else:
            y, w_o = _dsa_layer(h, mix_norm_g[i], dsa_w_in[j], dsa_q_norm_g[j], dsa_k_norm_g[j], t5_bias,
                                dsa_w_o[j], bsz, seq)
        h = _proj_ffn(h, y, w_o, ffn_norm_g[i], ffn_w_gate[i].astype(BF16), ffn_w_up[i].astype(BF16),
                      ffn_w_down[i].astype(BF16))
    return h.reshape(bsz, seq, d)
```

```python
import functools
import math

import jax
import jax.numpy as jnp
from jax import lax
from jax.experimental import pallas as pl
from jax.experimental.pallas import tpu as pltpu

F32 = jnp.float32
BF16 = jnp.bfloat16

V7X_VMEM_LIMIT_BYTES = 56 * 1024 * 1024
LANES = 128

RMS_EPS = 1e-6
RWKV_HEAD = 64
RWKV_DECAY_SCALE = math.exp(-0.5)
RWKV_GN_EPS = 64e-5
RWKV_CHUNK = 64
RWKV_CHAINS = 16
MLSTM_HEADS = 4
MLSTM_HEADS_PER_STEP = 2
MLSTM_GATE_CAP = 15.0
DSA_HEADS = 16
DSA_HEAD_DIM = 128
IDX_HEADS = 16
IDX_DIM = 64
DSA_TOPK_MAX = 256
T5_BUCKETS = 32
T5_MAX_DISTANCE = 128


def _cparams(*sem):
    return pltpu.CompilerParams(dimension_semantics=sem, vmem_limit_bytes=V7X_VMEM_LIMIT_BYTES)


def _dot(a, b):
    return jnp.dot(a.astype(BF16), b.astype(BF16), preferred_element_type=F32)


def _dot_nt(a, b):
    return lax.dot_general(a.astype(BF16), b.astype(BF16), (((1,), (1,)), ((), ())),
                           preferred_element_type=F32)


def _dot_tn(a, b):
    return lax.dot_general(a.astype(BF16), b.astype(BF16), (((0,), (0,)), ((), ())),
                           preferred_element_type=F32)


def _split3(x):
    hi = x.astype(BF16)
    r1 = x - hi.astype(F32)
    mid = r1.astype(BF16)
    lo = (r1 - mid.astype(F32)).astype(BF16)
    return hi, mid, lo


def _dot_exact_lhs(a_bf16, x):
    hi, mid, lo = _split3(x)
    d = lambda p: jnp.dot(a_bf16, p, preferred_element_type=F32)
    return d(hi) + (d(mid) + d(lo))


def _rms(x, g):
    ms = jnp.mean(x * x, axis=-1, keepdims=True)
    return x * lax.rsqrt(ms + RMS_EPS) * g


def _sigmoid(x):
    return 1.0 / (1.0 + jnp.exp(-x))


def _ffn_kernel(x_ref, y_ref, wo_ref, g_ref, wg_ref, wu_ref, wd_ref, o_ref, h_ref):
    @pl.when(pl.program_id(1) == 0)
    def _():
        x1 = x_ref[...] + jnp.dot(y_ref[...], wo_ref[...], preferred_element_type=F32)
        h_ref[...] = _rms(x1, g_ref[...]).astype(BF16)
        o_ref[...] = x1

    h = h_ref[...]
    gate = jnp.dot(h, wg_ref[...], preferred_element_type=F32)
    up = jnp.dot(h, wu_ref[...], preferred_element_type=F32)
    act = (gate * _sigmoid(gate) * up).astype(BF16)
    o_ref[...] += jnp.dot(act, wd_ref[...], preferred_element_type=F32)


def _proj_ffn(x, y, wo, g, wg, wu, wd, tm=512, tf=512):
    m, d = x.shape
    dy = y.shape[1]
    f = wg.shape[1]
    tm, tf = min(tm, m), min(tf, f)
    assert m % tm == 0 and f % tf == 0
    return pl.pallas_call(
        _ffn_kernel, grid=(m // tm, f // tf),
        in_specs=[pl.BlockSpec((tm, d), lambda i, j: (i, 0)),
                  pl.BlockSpec((tm, dy), lambda i, j: (i, 0)),
                  pl.BlockSpec((dy, d), lambda i, j: (0, 0), pipeline_mode=pl.Buffered(1)),
                  pl.BlockSpec((1, d), lambda i, j: (0, 0)),
                  pl.BlockSpec((d, tf), lambda i, j: (0, j)),
                  pl.BlockSpec((d, tf), lambda i, j: (0, j)),
                  pl.BlockSpec((tf, d), lambda i, j: (j, 0))],
        out_specs=pl.BlockSpec((tm, d), lambda i, j: (i, 0)),
        out_shape=jax.ShapeDtypeStruct((m, d), F32),
        scratch_shapes=[pltpu.VMEM((tm, d), BF16)],
        compiler_params=_cparams("parallel", "arbitrary"), name="proj_ffn")(
            x, y, wo, g.reshape(1, d), wg, wu, wd)


def _rwkv_prep_kernel(*refs, seq, tm, has_v):
    if has_v:
        (x_ref, xp_ref, g_ref, mu_ref, w0_ref, w1_ref, w2_ref, a0_ref, a1_ref, a2_ref,
         g1_ref, g2_ref, v0_ref, v1_ref, v2_ref, xs_ref, lw_ref, a_ref, gate_ref, vg_ref) = refs
    else:
        (x_ref, xp_ref, g_ref, mu_ref, w0_ref, w1_ref, w2_ref, a0_ref, a1_ref, a2_ref,
         g1_ref, g2_ref, xs_ref, lw_ref, a_ref, gate_ref) = refs
    i = pl.program_id(0)
    gn = g_ref[...]
    h = _rms(x_ref[...], gn)
    hp = _rms(xp_ref[...], gn)
    seq_start = (i * tm) % seq == 0
    hp_row = jnp.where(seq_start, 0.0, hp[7:8, :])
    row = lax.broadcasted_iota(jnp.int32, (tm, 1), 0)
    h_prev = jnp.where(row == 0, hp_row, pltpu.roll(h, 1, 0))
    xx = h_prev - h
    mix = lambda n: h + xx * mu_ref[n:n + 1, :]
    xs_ref[0] = mix(0).astype(BF16)
    xs_ref[1] = mix(2).astype(BF16)
    xv = mix(3).astype(BF16)
    xs_ref[2] = xv
    lw_ref[...] = -RWKV_DECAY_SCALE * _sigmoid(
        w0_ref[...] + _dot(jnp.tanh(_dot(mix(1), w1_ref[...])), w2_ref[...]))
    a_ref[...] = _sigmoid(a0_ref[...] + _dot(_dot(mix(4), a1_ref[...]), a2_ref[...])).astype(a_ref.dtype)
    gate_ref[...] = _dot(_sigmoid(_dot(mix(5), g1_ref[...])), g2_ref[...]).astype(gate_ref.dtype)
    if has_v:
        vg_ref[...] = _sigmoid(v0_ref[...] + _dot(_dot(xv, v1_ref[...]), v2_ref[...])).astype(vg_ref.dtype)


def _pad_lora(w_in, w_out):
    r = w_in.shape[1]
    rp = -(-r // LANES) * LANES
    return (jnp.pad(w_in, ((0, 0), (0, rp - r))).astype(BF16),
            jnp.pad(w_out, ((0, rp - r), (0, 0))).astype(BF16))


def _rwkv_prep(x, norm_g, mu, w0, w1, w2, a0, a1, a2, g1, g2, vres, seq, tm=256):
    t, d = x.shape
    tm = min(tm, seq)
    assert t % tm == 0 and seq % tm == 0 and tm % 8 == 0
    has_v = vres is not None
    row = lambda v: v.reshape(1, d)
    full = lambda a: pl.BlockSpec(a.shape, lambda i: (0,) * a.ndim)
    w1p, w2p = _pad_lora(w1, w2)
    a1p, a2p = _pad_lora(a1, a2)
    g1p, g2p = _pad_lora(g1, g2)
    mu8 = jnp.pad(mu, ((0, 2), (0, 0)))
    params = [row(norm_g), mu8, row(w0), w1p, w2p, row(a0), a1p, a2p, g1p, g2p]
    if has_v:
        v1p, v2p = _pad_lora(vres[1], vres[2])
        params += [row(vres[0]), v1p, v2p]
    tile = pl.BlockSpec((tm, d), lambda i: (i, 0))
    in_specs = [tile, pl.BlockSpec((8, d), lambda i: (jnp.maximum(i * (tm // 8) - 1, 0), 0))]
    in_specs += [full(p) for p in params]
    n_gates = 3 if has_v else 2
    out_shape = ([jax.ShapeDtypeStruct((3, t, d), BF16), jax.ShapeDtypeStruct((t, d), F32)]
                 + [jax.ShapeDtypeStruct((t, d), BF16)] * n_gates)
    out_specs = [pl.BlockSpec((3, tm, d), lambda i: (0, i, 0))] + [tile] * (1 + n_gates)
    return pl.pallas_call(
        functools.partial(_rwkv_prep_kernel, seq=seq, tm=tm, has_v=has_v),
        grid=(t // tm,), in_specs=in_specs, out_specs=out_specs, out_shape=out_shape,
        compiler_params=_cparams("parallel"), name="rwkv_prep")(x, x, *params)


def _seg_sum(x, seg):
    w = x.shape[1]
    hi = x.astype(BF16)
    lo = (x - hi.astype(F32)).astype(BF16)
    d = lambda p, q: jnp.dot(p[:, q:q + 256], seg, preferred_element_type=F32)
    return jnp.concatenate([d(hi, q) + d(lo, q) for q in range(0, w, 256)], axis=1)


def _rwkv_scan_kernel(*refs, L, tc, has_v):
    (s_ref, a2_ref, r2_ref, b2_ref, k2_ref, v2_ref, pl_ref, phi_ref, psi_ref, theta_ref, yloc_ref,
     y_ref, bonus_ref, mab_ref, tinv_ref, mak_ref, mrb_ref, mrk_ref, av_ref) = refs[-19:]
    refs = refs[:-19]
    if has_v:
        (xs_ref, w_ref, lw_ref, a_ref, gate_ref, vf_ref, vg_ref,
         kk_ref, ka_ref, rk_ref, lng_ref, lnb_ref, o_ref) = refs
    else:
        (xs_ref, w_ref, lw_ref, a_ref, gate_ref,
         kk_ref, ka_ref, rk_ref, lng_ref, lnb_ref, o_ref, vout_ref) = refs
    W = lw_ref.shape[-1]
    P2 = 2 * L
    npair = W // P2
    N = RWKV_HEAD

    @pl.when(pl.program_id(2) == 0)
    def _():
        s_ref[...] = jnp.zeros_like(s_ref)

    ri = lax.broadcasted_iota(jnp.int32, (P2, P2), 0)
    ci = lax.broadcasted_iota(jnp.int32, (P2, P2), 1)
    strict = ri > ci
    incl = ri >= ci
    eye = (ri == ci).astype(F32)
    levels = []
    s = 1
    while s < L:
        levels.append(((ri // s) % 2 == 1) & ((ci // s) == (ri // s) - 1))
        s *= 2
    head0 = lax.broadcasted_iota(jnp.int32, (L, P2), 1) < N
    sr = lax.broadcasted_iota(jnp.int32, (256, 256), 0) // N
    sc = lax.broadcasted_iota(jnp.int32, (256, 256), 1) // N
    seg = (sr == sc).astype(BF16)

    nc = tc // L
    tr = lax.broadcasted_iota(jnp.int32, (tc, tc), 0)
    tcol = lax.broadcasted_iota(jnp.int32, (tc, tc), 1)
    tri = jnp.where(tr >= tcol, jnp.where(tr // L == tcol // L, 1.0, 0.0), 0.0).astype(BF16)
    del tr, tcol

    r, k, v = (jnp.dot(xs_ref[n], w_ref[n], preferred_element_type=F32) for n in range(3))
    lw = lw_ref[...]
    a = a_ref[...].astype(F32)
    if has_v:
        v = v + (vf_ref[...] - v) * vg_ref[...].astype(F32)
    else:
        vout_ref[...] = v
    kk = k * kk_ref[...]
    kk = kk / jnp.maximum(jnp.sqrt(_seg_sum(kk * kk, seg)), 1e-12)
    kmod = k * (1.0 + (a - 1.0) * ka_ref[...])
    c = _dot_exact_lhs(tri, lw)
    enc = jnp.exp(-c)
    bonus_ref[...] = _seg_sum(r * kmod * rk_ref[...], seg) * v
    operands = (-kk * jnp.exp(c - lw), r * jnp.exp(c), kk * a * enc, kmod * enc, v)
    for z, z_ref in zip(operands, (a2_ref, r2_ref, b2_ref, k2_ref, v2_ref)):
        for ch in range(nc):
            for p in range(npair):
                zz = z[ch * L:(ch + 1) * L, P2 * p:P2 * (p + 1)]
                z_ref[ch, p, :L] = jnp.where(head0, zz, 0.0).astype(BF16)
                z_ref[ch, p, L:] = jnp.where(head0, 0.0, zz).astype(BF16)
    for ch in range(nc):
        pl_ref[ch] = jnp.broadcast_to(jnp.exp(c[(ch + 1) * L - 1:(ch + 1) * L, :]), (8, W))
    del r, k, v, lw, a, kk, kmod, c, enc, operands

    chains = [(ch, p) for ch in range(nc) for p in range(npair)]
    for g0 in range(0, len(chains), RWKV_CHAINS):
        group = chains[g0:g0 + RWKV_CHAINS]
        for c in group:
            g = _dot_nt(jnp.concatenate([a2_ref[c], r2_ref[c]], axis=0),
                        jnp.concatenate([b2_ref[c], k2_ref[c]], axis=0))
            m_ab = jnp.where(strict, g[:P2, :P2], 0.0)
            mab_ref[c] = m_ab
            tinv_ref[c] = eye + jnp.where(levels[0], m_ab, 0.0)
            mak_ref[c] = jnp.where(strict, g[:P2, P2:], 0.0).astype(BF16)
            mrb_ref[c] = jnp.where(incl, g[P2:, :P2], 0.0).astype(BF16)
            mrk_ref[c] = jnp.where(incl, g[P2:, P2:], 0.0).astype(BF16)
        for lv in levels[1:]:
            steps = [_dot(jnp.where(lv, mab_ref[c], 0.0), tinv_ref[c]).astype(BF16) for c in group]
            for c, step in zip(group, steps):
                t_cur = tinv_ref[c]
                tinv_ref[c] = t_cur + _dot(t_cur, step)
        mvs = [_dot(mak_ref[c], v2_ref[c]).astype(BF16) for c in group]
        for c, mv in zip(group, mvs):
            av_ref[c] = _dot(tinv_ref[c], jnp.concatenate([a2_ref[c], mv], axis=1)).astype(BF16)
        ths = [_dot(mrb_ref[c], av_ref[c]) for c in group]
        yls = [_dot(mrk_ref[c], v2_ref[c]) for c in group]
        for c, th, yl in zip(group, ths, yls):
            theta_ref[c] = (r2_ref[c].astype(F32) + th[:, :P2]).astype(BF16)
            yloc_ref[c] = th[:, P2:] + yl
        for c in group:
            av = av_ref[c]
            pp = _dot_tn(jnp.concatenate([av[:, P2:], av[:, :P2]], axis=1), b2_ref[c])
            p_last = pl_ref[c[0]][0:1, P2 * c[1]:P2 * (c[1] + 1)]
            phi_ref[c] = ((eye + pp[P2:]) * p_last).astype(BF16)
            psi_ref[c] = (pp[:P2] + _dot_tn(v2_ref[c], k2_ref[c])) * p_last

    for ch in range(nc):
        for p in range(npair):
            s0 = s_ref[p]
            yo = _dot_nt(theta_ref[ch, p], s0) + yloc_ref[ch, p]
            y_ref[ch * L:(ch + 1) * L, P2 * p:P2 * (p + 1)] = yo[:L] + yo[L:]
            s_ref[p] = _dot(s0, phi_ref[ch, p]) + psi_ref[ch, p]

    y = y_ref[...]
    mean = _seg_sum(y, seg) * (1.0 / N)
    yc = y - mean
    var = _seg_sum(yc * yc, seg) * (1.0 / N)
    yn = yc * lax.rsqrt(var + RWKV_GN_EPS) * lng_ref[...] + lnb_ref[...]
    o_ref[...] = ((yn + bonus_ref[...]) * gate_ref[...].astype(F32)).astype(o_ref.dtype)


def _rwkv_scan(xs, w_rkv, lw, a, gate, v_first, vgate, k_k, k_a, r_k, lnx_g, lnx_b, bsz, seq, tc=256, wb=512):
    _, t, d = xs.shape
    L = RWKV_CHUNK
    tc, wb = min(tc, seq), min(wb, d)
    assert seq % tc == 0 and tc % L == 0 and d % wb == 0 and wb % 256 == 0
    has_v = v_first is not None
    nc, p2, npair = tc // L, 2 * L, wb // (2 * L)
    b3 = lambda z: z.reshape(bsz, seq, d)
    blk = pl.BlockSpec((None, tc, wb), lambda b, h, c: (b, c, h))
    prow = pl.BlockSpec((1, wb), lambda b, h, c: (0, h))
    args = [xs.reshape(3, bsz, seq, d), w_rkv, b3(lw), b3(a), b3(gate)]
    in_specs = [pl.BlockSpec((3, None, tc, d), lambda b, h, c: (0, b, c, 0)),
                pl.BlockSpec((3, d, wb), lambda b, h, c: (0, 0, h)),
                blk, blk, blk]
    if has_v:
        args += [b3(v_first), b3(vgate)]
        in_specs += [blk, blk]
    args += [z.reshape(1, d) for z in (k_k, k_a, r_k, lnx_g, lnx_b)]
    in_specs += [prow] * 5
    out_shape = [jax.ShapeDtypeStruct((bsz, seq, d), BF16)]
    if not has_v:
        out_shape.append(jax.ShapeDtypeStruct((bsz, seq, d), F32))
    out = pl.pallas_call(
        functools.partial(_rwkv_scan_kernel, L=L, tc=tc, has_v=has_v),
        grid=(bsz, d // wb, seq // tc), in_specs=in_specs, out_specs=[blk] * len(out_shape),
        out_shape=out_shape,
        scratch_shapes=[pltpu.VMEM((npair, p2, p2), F32)]
        + [pltpu.VMEM((nc, npair, p2, p2), BF16)] * 5
        + [pltpu.VMEM((nc, 8, wb), F32),
           pltpu.VMEM((nc, npair, p2, p2), BF16), pltpu.VMEM((nc, npair, p2, p2), F32),
           pltpu.VMEM((nc, npair, p2, p2), BF16), pltpu.VMEM((nc, npair, p2, p2), F32),
           pltpu.VMEM((tc, wb), F32), pltpu.VMEM((tc, wb), F32),
           pltpu.VMEM((nc, npair, p2, p2), F32), pltpu.VMEM((nc, npair, p2, p2), F32),
           pltpu.VMEM((nc, npair, p2, p2), BF16), pltpu.VMEM((nc, npair, p2, p2), BF16),
           pltpu.VMEM((nc, npair, p2, p2), BF16),
           pltpu.VMEM((nc, npair, p2, 2 * p2), BF16)],
        compiler_params=_cparams("parallel", "parallel", "arbitrary"), name="rwkv_scan")(*args)
    return out[0].reshape(t, d), (None if has_v else out[1].reshape(t, d))


def _rwkv_layer(x, norm_g, p, vres, v_first, bsz, seq):
    (mu, w_rkv, w0, w1, w2, a0, a1, a2, g1, g2, k_k, k_a, r_k, lnx_g, lnx_b, w_o) = p
    outs = _rwkv_prep(x, norm_g, mu, w0, w1, w2, a0, a1, a2, g1, g2, vres, seq)
    xs, lw, a, gate = outs[:4]
    vgate = outs[4] if vres is not None else None
    o, v_own = _rwkv_scan(xs, w_rkv.astype(BF16), lw, a, gate, v_first if vres is not None else None, vgate,
                          k_k, k_a, r_k, lnx_g, lnx_b, bsz, seq)
    return o, w_o.astype(BF16), (v_own if vres is None else v_first)


def _mlstm_kernel(x_ref, mg_ref, w_ref, wg_ref, bias_ref, ng_ref, out_ref, c_ref, m_ref, *, nh, hps):
    lc = x_ref.shape[0]
    dv = ng_ref.shape[1] // hps
    dk = dv // 2

    @pl.when(pl.program_id(2) == 0)
    def _():
        c_ref[...] = jnp.zeros_like(c_ref)
        m_ref[...] = jnp.zeros_like(m_ref)

    hb = _rms(x_ref[...], mg_ref[...]).astype(BF16)
    proj = jnp.dot(hb, w_ref[...], preferred_element_type=F32).astype(BF16)
    q_of = lambda s: proj[:, s * dk:(s + 1) * dk]
    k_of = lambda s: proj[:, (hps + s) * dk:(hps + s + 1) * dk]
    v_of = lambda s: proj[:, 2 * hps * dk + s * dv:2 * hps * dk + (s + 1) * dv]
    o_of = lambda s: proj[:, 2 * hps * dk + (hps + s) * dv:2 * hps * dk + (hps + s + 1) * dv]

    lane = lax.broadcasted_iota(jnp.int32, (lc, LANES), 1)
    z = jnp.dot(hb, wg_ref[...], preferred_element_type=F32) + bias_ref[...]
    zc = MLSTM_GATE_CAP * jnp.tanh(z / MLSTM_GATE_CAP)
    lf_all = jnp.minimum(zc, 0.0) - jnp.log1p(jnp.exp(-jnp.abs(zc)))
    rr = lax.broadcasted_iota(jnp.int32, (lc, lc), 0)
    cc = lax.broadcasted_iota(jnp.int32, (lc, lc), 1)
    causal = rr >= cc
    bcum_all = _dot_exact_lhs(causal.astype(BF16), lf_all)
    comb = jnp.where(lane < nh, zc, bcum_all)
    er = lax.broadcasted_iota(jnp.int32, (8, LANES), 0)
    ec = lax.broadcasted_iota(jnp.int32, (8, LANES), 1)
    head_of_row = pl.program_id(1) * hps + er // 2
    sel = jnp.where(er < 2 * hps, jnp.where(ec == head_of_row + nh * (er % 2), 1.0, 0.0), 0.0).astype(BF16)
    hi, mid, lo = _split3(comb)
    tr = lambda p: lax.dot_general(sel, p, (((1,), (1,)), ((), ())), preferred_element_type=F32)
    rows = tr(hi) + (tr(mid) + tr(lo))
    ones_blk = (lane == 0).astype(BF16)

    for s in range(hps):
        h = pl.program_id(1) * hps + s
        li_col = jnp.sum(jnp.where(lane == h, comb, 0.0), axis=-1, keepdims=True)
        bc_col = jnp.sum(jnp.where(lane == h + nh, comb, 0.0), axis=-1, keepdims=True)
        li_row, bc_row = rows[2 * s:2 * s + 1], rows[2 * s + 1:2 * s + 2]
        m_st = m_ref[s, 0:1, 0:1]
        dmat = jnp.where(causal, bc_col - bc_row + li_row, -jnp.inf)
        inter = bc_col + m_st
        m_t = jnp.maximum(inter, jnp.max(dmat, axis=-1, keepdims=True))
        q = q_of(s) * (dk ** -0.5)
        k = k_of(s)
        sc = _dot_nt(q, k) * jnp.exp(dmat - m_t)
        w_inter = jnp.exp(inter - m_t)
        v_ext = jnp.concatenate([v_of(s), ones_blk], axis=1)
        c_st = c_ref[s]
        nd = _dot(sc, v_ext) + w_inter * _dot(q, c_st)
        den = nd[:, dv:dv + 1]
        hc = nd[:, :dv] / jnp.maximum(jnp.abs(den), jnp.exp(-m_t))

        b_tot = bc_col[lc - 1:lc]
        log_wk = b_tot - bc_col + li_col
        m_new = jnp.maximum(b_tot + m_st, jnp.max(log_wk, axis=0, keepdims=True))
        c_ref[s] = jnp.exp(b_tot + m_st - m_new) * c_st + _dot_tn(k * jnp.exp(log_wk - m_new), v_ext)
        m_ref[s] = jnp.broadcast_to(m_new, m_ref.shape[1:])

        hn = hc * lax.rsqrt(jnp.mean(hc * hc, axis=-1, keepdims=True) + RMS_EPS) * ng_ref[:, s * dv:(s + 1) * dv]
        o_gate = _sigmoid(o_of(s).astype(F32))
        out_ref[:, s * dv:(s + 1) * dv] = (hn * o_gate).astype(out_ref.dtype)


def _mlstm_layer(x, mix_g, w_in, b_if, norm_g, w_o, bsz, seq, lc=256):
    t, d = x.shape
    nh = MLSTM_HEADS
    dk, dv = d // 2 // nh, d // nh
    nq = 2 * nh * dk + 2 * nh * dv
    assert w_in.shape[1] == nq + 2 * nh and dk % LANES == 0
    lc = min(lc, seq)
    assert seq % lc == 0
    w_gates = jnp.pad(w_in[:, nq:], ((0, 0), (0, LANES - 2 * nh))).astype(BF16)
    bias = jnp.zeros((1, LANES), F32).at[0, :nh].set(b_if[0]).at[0, nh:2 * nh].set(b_if[1])
    hps = MLSTM_HEADS_PER_STEP
    assert nh % hps == 0
    ng = nh // hps
    heads = lambda base, width, g: w_in[:, base + g * hps * width:base + (g + 1) * hps * width]
    w_groups = jnp.stack([jnp.concatenate([heads(0, dk, g), heads(nh * dk, dk, g), heads(2 * nh * dk, dv, g),
                                           heads(2 * nh * dk + nh * dv, dv, g)], axis=1)
                          for g in range(ng)]).astype(BF16)
    out = pl.pallas_call(
        functools.partial(_mlstm_kernel, nh=nh, hps=hps),
        grid=(bsz, ng, seq // lc),
        in_specs=[pl.BlockSpec((None, lc, d), lambda b, h, c: (b, c, 0)),
                  pl.BlockSpec((1, d), lambda b, h, c: (0, 0)),
                  pl.BlockSpec((None, d, w_groups.shape[2]), lambda b, h, c: (h, 0, 0)),
                  pl.BlockSpec((d, LANES), lambda b, h, c: (0, 0)),
                  pl.BlockSpec((1, LANES), lambda b, h, c: (0, 0)),
                  pl.BlockSpec((1, hps * dv), lambda b, h, c: (0, h))],
        out_specs=pl.BlockSpec((None, lc, hps * dv), lambda b, h, c: (b, c, h)),
        out_shape=jax.ShapeDtypeStruct((bsz, seq, nh * dv), BF16),
        scratch_shapes=[pltpu.VMEM((hps, dk, dv + LANES), F32), pltpu.VMEM((hps, 8, LANES), F32)],
        compiler_params=_cparams("parallel", "parallel", "arbitrary"), name="mlstm")(
            x.reshape(bsz, seq, d), mix_g.reshape(1, d), w_groups, w_gates, bias, norm_g.reshape(1, nh * dv))
    return out.reshape(t, nh * dv), w_o.astype(BF16)


DSA_TQ = 128
DSA_TK = 256
DSA_HEAD_GROUP = 4
DSA_TILES_PER_TRIP = 4
DSA_PROJ_GROUPS = 4
DSA_NEG = -1e30
KEY_NEG_INF = -2139095041
IDX_BIG = 2 ** 30
LOG2E = math.log2(math.e)


def _dsa_proj_kernel(x_ref, g_ref, w_ref, qg_ref, kg_ref, qn_ref, qi_ref, k_ref, v_ref, ki_ref, wi_ref, h_ref,
                     *, nh, nih):
    dh = DSA_HEAD_DIM
    gs = DSA_PROJ_GROUPS
    j = pl.program_id(2)
    nq, nqi = nh // gs, nih // gs

    @pl.when(j == 0)
    def _():
        h_ref[...] = _rms(x_ref[...], g_ref[...]).astype(BF16)

    y = jnp.dot(h_ref[...], w_ref[...], preferred_element_type=F32)
    part = lambda n: y[:, n * dh:(n + 1) * dh]

    @pl.when(j < nq)
    def _():
        qscale = dh ** -0.5 * LOG2E
        for n in range(gs):
            qn_ref[n] = (_rms(part(n), qg_ref[...]) * qscale).astype(BF16)

    @pl.when(jnp.logical_and(j >= nq, j < nq + nqi))
    def _():
        for n in range(gs):
            qi_ref[n] = part(n).astype(BF16)

    @pl.when(j == nq + nqi)
    def _():
        k_ref[...] = _rms(part(0), kg_ref[...]).astype(BF16)
        v_ref[:, :dh] = part(1).astype(BF16)
        v_ref[:, dh:] = jnp.ones((v_ref.shape[0], LANES), BF16)
        ki_ref[...] = part(2).astype(BF16)
        wi_ref[...] = part(3)


def _dsa_kernel(qn_ref, qi_ref, wi_ref, k_ref, v_ref, ki_ref, nb_ref, o_ref,
                key_ref, hi_ref, lo_ref, lo2_ref, w_ref, acc_ref, m_ref, *, nh, nih, n_sel, idx_bits):
    tq, tk, dh = DSA_TQ, DSA_TK, DSA_HEAD_DIM
    hg = DSA_HEAD_GROUP
    q0 = pl.program_id(1) * tq
    jd = (q0 + tq - 1) // tk
    nt = jd + 1
    rowi = lax.broadcasted_iota(jnp.int32, (tq, tk), 0)
    coli = lax.broadcasted_iota(jnp.int32, (tq, tk), 1)
    ktile = lambda jt: pl.ds(pl.multiple_of(jt * tk, tk), tk)
    twice = lambda z: jnp.concatenate([z] * (tk // LANES), axis=1)

    wi = wi_ref[...]
    for h in range(nih):
        w_ref[h] = jnp.broadcast_to(wi[:, h:h + 1], (tq, LANES))

    def for_each_tile(n, body, per_trip=DSA_TILES_PER_TRIP):
        def trip(i, c):
            for u in range(per_trip):
                body(per_trip * i + u)
            return c
        lax.fori_loop(0, n // per_trip, trip, 0)
        done = (n // per_trip) * per_trip
        for u in range(per_trip - 1):
            @pl.when(n - done > u)
            def _():
                body(done + u)

    def score_tile(jt):
        ki_t = ki_ref[ktile(jt), :]
        score = jnp.zeros((tq, tk), F32)
        for g0 in range(0, nih, hg):
            lg = _dot_nt(qi_ref[g0:g0 + hg].reshape(hg * tq, LANES), ki_t)
            for h in range(hg):
                score = score + jnp.maximum(lg[h * tq:(h + 1) * tq], 0.0) * twice(w_ref[g0 + h])
        score = jnp.where(jt * tk + coli <= q0 + rowi, score, -jnp.inf)
        bits = pltpu.bitcast(score, jnp.int32)
        key = bits ^ ((bits >> 31) & 0x7FFFFFFF)
        key_ref[jt] = key
        key_t = pltpu.bitcast(jnp.transpose(pltpu.bitcast(key, F32)), jnp.int32)
        hi_ref[jt] = (key_t >> 16).astype(jnp.int16)
        lo_ref[jt] = ((key_t & 0xFFFF) - 32768).astype(jnp.int16)

    for_each_tile(nt, score_tile)

    unroll = DSA_TILES_PER_TRIP
    n_trips = (nt + unroll - 1) // unroll
    floor16 = jnp.full((tk, tq), -32768, jnp.int16)

    def pad_tiles(ref):
        for u in range(unroll - 1):
            @pl.when(n_trips * unroll - nt > u)
            def _():
                ref[nt + u] = floor16

    def count16(ref, cand):
        cb = jnp.broadcast_to(cand.astype(jnp.int16), (32, tq))

        def body(i, acc):
            for u in range(unroll):
                tile = ref[unroll * i + u]
                for r0 in range(0, tk, 32):
                    acc = acc + jnp.where(tile[r0:r0 + 32] >= cb, jnp.int16(1), jnp.int16(0))
            return acc
        acc = lax.fori_loop(0, n_trips, body, jnp.zeros((32, tq), jnp.int16))
        return jnp.sum(acc.astype(F32), axis=0, keepdims=True)

    pad_tiles(hi_ref)

    def kth_largest16(ref, kth):
        def bit(b, prefix):
            cand = prefix + jnp.left_shift(jnp.int32(1), 15 - b)
            return jnp.where(count16(ref, cand) >= kth, cand, prefix)
        return lax.fori_loop(0, 16, bit, jnp.full((1, tq), -32768, jnp.int32))

    hi_k = kth_largest16(hi_ref, n_sel)
    above = jnp.where(hi_k >= 32767, 0.0, count16(hi_ref, jnp.minimum(hi_k + 1, 32767)))
    hi_kb = jnp.broadcast_to(hi_k.astype(jnp.int16), (tk, tq))

    def bucket_tile(jt, c):
        lo2_ref[jt] = jnp.where(hi_ref[jt] == hi_kb, lo_ref[jt], floor16)
        return c

    lax.fori_loop(0, nt, bucket_tile, 0)
    pad_tiles(lo2_ref)
    lo_k = kth_largest16(lo2_ref, n_sel - above)
    thr_row = hi_k * 65536 + (lo_k + 32768)
    tw = pltpu.bitcast(jnp.transpose(pltpu.bitcast(jnp.broadcast_to(thr_row, (tq, tq)), F32)), jnp.int32)
    thr = tw[:, :1]

    def count(hit):
        def body(jt, acc):
            keyt = key_ref[jt]
            for c0 in range(0, tk, LANES):
                acc = acc + hit(keyt[:, c0:c0 + LANES], jt * tk + c0)
            return acc
        acc = lax.fori_loop(0, nt, body, jnp.zeros((tq, LANES), F32))
        return jnp.sum(acc, axis=-1, keepdims=True)

    wide = lambda col: jnp.broadcast_to(col, (tq, LANES))
    lane = lax.broadcasted_iota(jnp.int32, (tq, LANES), 1)

    c_gt = count(lambda kv, base: jnp.where(kv > tw, 1.0, 0.0))
    c_ge = count(lambda kv, base: jnp.where(kv >= tw, 1.0, 0.0))
    need = n_sel - c_gt
    c_eq = c_ge - c_gt

    def tie_search():
        def index_bit(b, jcur):
            cand = jcur + jnp.left_shift(jnp.int32(1), idx_bits - 1 - b)
            cw = wide(cand)
            f = count(lambda kv, base: jnp.where(kv == tw, jnp.where(base + lane < cw, 1.0, 0.0), 0.0))
            return jnp.where(f <= need, cand, jcur)
        return lax.fori_loop(0, idx_bits, index_bit, jnp.zeros((tq, 1), jnp.int32))

    excess = jnp.max(c_eq - need) > 0.0
    jc = lax.cond(excess, tie_search, lambda: jnp.full((tq, 1), IDX_BIG, jnp.int32))
    jc = jnp.where(c_eq > need, jc, IDX_BIG)
    jc = jnp.where(thr == KEY_NEG_INF, 0, jc)

    m_ref[...] = jnp.full_like(m_ref, DSA_NEG)
    acc_ref[...] = jnp.zeros_like(acc_ref)

    def attend_tile(jt, near_idx):
        k_t = k_ref[ktile(jt), :]
        v_t = v_ref[ktile(jt), :]
        keyt = key_ref[jt]
        tie_mb = jnp.where(keyt == thr, jnp.where(jt * tk + coli < jc, 0.0, DSA_NEG), DSA_NEG)
        mb = jnp.where(keyt > thr, 0.0, tie_mb)
        for g0 in range(0, nh, hg):
            s_g = _dot_nt(qn_ref[g0:g0 + hg].reshape(hg * tq, dh), k_t)
            ps, alphas = [], []
            for h in range(hg):
                rows = slice((g0 + h) * tq, (g0 + h + 1) * tq)
                s = s_g[h * tq:(h + 1) * tq] + (mb if near_idx is None else nb_ref[near_idx, g0 + h] + mb)
                m_old = m_ref[rows, :]
                m_new = jnp.maximum(m_old, jnp.max(s, axis=-1, keepdims=True))
                m_ref[rows, :] = m_new
                ps.append(jnp.exp2(s - twice(m_new)).astype(BF16))
                alphas.append(jnp.exp2(m_old - m_new))
            pv = jnp.dot(jnp.concatenate(ps, axis=0), v_t, preferred_element_type=F32)
            rows_g = slice(g0 * tq, (g0 + hg) * tq)
            acc_ref[rows_g, :] = acc_ref[rows_g, :] * twice(jnp.concatenate(alphas, axis=0)) + pv

    on_tile_edge = q0 == jd * tk
    has_prev_near = jnp.logical_and(on_tile_edge, jd >= 1)
    n_far = jnp.where(has_prev_near, jd - 1, jd)

    for_each_tile(n_far, lambda jt: attend_tile(jt, None))

    @pl.when(has_prev_near)
    def _():
        attend_tile(jd - 1, 2)
        attend_tile(jd, 0)

    @pl.when(jnp.logical_not(has_prev_near))
    def _():
        attend_tile(jd, jnp.where(on_tile_edge, 0, 1))

    for h in range(nh):
        rows = slice(h * tq, (h + 1) * tq)
        o_ref[:, h * dh:(h + 1) * dh] = (acc_ref[rows, :dh] / acc_ref[rows, dh:]).astype(o_ref.dtype)


def _t5_bucket(rel):
    n = jnp.maximum(rel, 0)
    exact = T5_BUCKETS // 2
    nf = jnp.maximum(n, exact).astype(F32)
    large = exact + (jnp.log(nf / exact) / math.log(T5_MAX_DISTANCE / exact)
                     * (T5_BUCKETS - exact)).astype(jnp.int32)
    return jnp.where(n < exact, n, jnp.minimum(large, T5_BUCKETS - 1))


def _dsa_layer(x, mix_g, w_in, q_norm_g, k_norm_g, t5_table, w_o, bsz, seq):
    t, d = x.shape
    nh, dh, nih, di = DSA_HEADS, DSA_HEAD_DIM, IDX_HEADS, IDX_DIM
    tq, tk = DSA_TQ, DSA_TK
    assert seq % (tk * DSA_TILES_PER_TRIP) == 0 and tk == 2 * tq and dh == LANES and di <= LANES and nih <= LANES
    assert T5_MAX_DISTANCE <= tq
    n_sel = min(DSA_TOPK_MAX, seq // 4)
    o1, o2, o3, o4, o5 = nh * dh, nh * dh + dh, nh * dh + 2 * dh, nh * dh + 2 * dh + nih * di, nh * dh + 2 * dh + nih * di + di
    w_qi = jnp.pad(w_in[:, o3:o4].reshape(d, nih, di), ((0, 0), (0, 0), (0, LANES - di))).reshape(d, nih * LANES)
    w_ki = jnp.pad(w_in[:, o4:o5], ((0, 0), (0, LANES - di)))
    w_wi = jnp.pad(w_in[:, o5:o5 + nih] * (nih ** -0.5 * di ** -0.5), ((0, 0), (0, LANES - nih)))
    w = jnp.concatenate([w_in[:, :o1], w_qi, w_in[:, o1:o3], w_ki, w_wi], axis=1).astype(BF16)
    gs = DSA_PROJ_GROUPS
    nq, nqi = nh // gs, nih // gs
    assert gs == 4 and w.shape[1] == gs * dh * (nq + nqi + 1) and nh % gs == 0 and nih % gs == 0
    tm = min(1024, seq)
    assert seq % tm == 0
    tok = lambda width: pl.BlockSpec((None, tm, width), lambda b, i, j: (b, i, 0))
    qn, qi, kn, vb, kib, wis = pl.pallas_call(
        functools.partial(_dsa_proj_kernel, nh=nh, nih=nih),
        grid=(bsz, seq // tm, nq + nqi + 1),
        in_specs=[pl.BlockSpec((None, tm, d), lambda b, i, j: (b, i, 0)),
                  pl.BlockSpec((1, d), lambda b, i, j: (0, 0)),
                  pl.BlockSpec((d, gs * dh), lambda b, i, j: (0, j)),
                  pl.BlockSpec((1, dh), lambda b, i, j: (0, 0)),
                  pl.BlockSpec((1, dh), lambda b, i, j: (0, 0))],
        out_specs=[pl.BlockSpec((None, gs, tm, dh), lambda b, i, j: (b, jnp.minimum(j, nq - 1), i, 0)),
                   pl.BlockSpec((None, gs, tm, LANES), lambda b, i, j: (b, jnp.clip(j - nq, 0, nqi - 1), i, 0)),
                   tok(dh), tok(dh + LANES), tok(LANES), tok(LANES)],
        out_shape=[jax.ShapeDtypeStruct((bsz, nh, seq, dh), BF16),
                   jax.ShapeDtypeStruct((bsz, nih, seq, LANES), BF16),
                   jax.ShapeDtypeStruct((bsz, seq, dh), BF16),
                   jax.ShapeDtypeStruct((bsz, seq, dh + LANES), BF16),
                   jax.ShapeDtypeStruct((bsz, seq, LANES), BF16),
                   jax.ShapeDtypeStruct((bsz, seq, LANES), F32)],
        scratch_shapes=[pltpu.VMEM((tm, d), BF16)],
        compiler_params=_cparams("parallel", "parallel", "arbitrary"), name="dsa_proj")(
            x.reshape(bsz, seq, d), mix_g.reshape(1, d), w, q_norm_g.reshape(1, dh), k_norm_g.reshape(1, dh))

    ii = jnp.arange(tq, dtype=jnp.int32)[:, None]
    jj = jnp.arange(tk, dtype=jnp.int32)[None, :]
    buckets = jnp.stack([_t5_bucket(off + ii - jj) for off in (0, tq, 2 * tq)])
    rel_table = (t5_table - t5_table[T5_BUCKETS - 1]).astype(F32)
    near = jnp.einsum("otkb,bh->ohtk", jax.nn.one_hot(buckets, T5_BUCKETS, dtype=F32), rel_table,
                      precision=lax.Precision.HIGHEST) * LOG2E

    seqblk = lambda: pl.BlockSpec((None, seq, LANES), lambda b, i: (b, 0, 0))
    out = pl.pallas_call(
        functools.partial(_dsa_kernel, nh=nh, nih=nih, n_sel=n_sel, idx_bits=int(seq).bit_length()),
        grid=(bsz, seq // tq),
        in_specs=[pl.BlockSpec((None, nh, tq, dh), lambda b, i: (b, 0, i, 0)),
                  pl.BlockSpec((None, nih, tq, LANES), lambda b, i: (b, 0, i, 0)),
                  pl.BlockSpec((None, tq, LANES), lambda b, i: (b, i, 0)),
                  seqblk(), pl.BlockSpec((None, seq, dh + LANES), lambda b, i: (b, 0, 0)), seqblk(),
                  pl.BlockSpec((3, nh, tq, tk), lambda b, i: (0, 0, 0, 0))],
        out_specs=pl.BlockSpec((None, tq, nh * dh), lambda b, i: (b, i, 0)),
        out_shape=jax.ShapeDtypeStruct((bsz, seq, nh * dh), BF16),
        scratch_shapes=[pltpu.VMEM((seq // tk, tq, tk), jnp.int32),
                        pltpu.VMEM((seq // tk, tk, tq), jnp.int16),
                        pltpu.VMEM((seq // tk, tk, tq), jnp.int16),
                        pltpu.VMEM((seq // tk, tk, tq), jnp.int16),
                        pltpu.VMEM((nih, tq, LANES), F32),
                        pltpu.VMEM((nh * tq, dh + LANES), F32),
                        pltpu.VMEM((nh * tq, LANES), F32)],
        compiler_params=_cparams("parallel", "arbitrary"), name="dsa_attn")(
            qn, qi, wis, kn, vb, kib, near)
    return out.reshape(t, nh * dh), w_o.astype(BF16)


def kernel(x, rwkv_mu, rwkv_w_rkv, rwkv_w0, rwkv_w1, rwkv_w2, rwkv_a0, rwkv_a1, rwkv_a2, rwkv_v0, rwkv_v1,
           rwkv_v2, rwkv_g1, rwkv_g2, rwkv_k_k, rwkv_k_a, rwkv_r_k, rwkv_lnx_g, rwkv_lnx_b, rwkv_w_o,
           mlstm_w_in, mlstm_b_if, mlstm_norm_g, mlstm_w_o, dsa_w_in, dsa_q_norm_g, dsa_k_norm_g, dsa_w_o,
           t5_bias, mix_norm_g, ffn_norm_g, ffn_w_gate, ffn_w_up, ffn_w_down):
    bsz, seq, d = x.shape
    depth = mix_norm_g.shape[0]
    h = x.reshape(bsz * seq, d)
    v_first = None
    for i in range(depth):
        kind, j = i % 3, i // 3
        if kind == 0:
            vres = None if j == 0 else (rwkv_v0[j - 1], rwkv_v1[j - 1], rwkv_v2[j - 1])
            p = (rwkv_mu[j], rwkv_w_rkv[j], rwkv_w0[j], rwkv_w1[j], rwkv_w2[j], rwkv_a0[j], rwkv_a1[j],
                 rwkv_a2[j], rwkv_g1[j], rwkv_g2[j], rwkv_k_k[j], rwkv_k_a[j], rwkv_r_k[j],
                 rwkv_lnx_g[j], rwkv_lnx_b[j], rwkv_w_o[j])
            y, w_o, v_first = _rwkv_layer(h, mix_norm_g[i], p, vres, v_first, bsz, seq)
        elif kind == 1:
            y, w_o = _mlstm_layer(h, mix_norm_g[i], mlstm_w_in[j], mlstm_b_if[j], mlstm_norm_g[j],
                                  mlstm_w_o[j], bsz, seq)
        else:
            y, w_o = _dsa_layer(h, mix_norm_g[i], dsa_w_in[j], dsa_q_norm_g[j], dsa_k_norm_g[j], t5_bias,
                                dsa_w_o[j], bsz, seq)
        h = _proj_ffn(h, y, w_o, ffn_norm_g[i], ffn_w_gate[i].astype(BF16), ffn_w_up[i].astype(BF16),
                      ffn_w_down[i].astype(BF16))
    return h.reshape(bsz, seq, d)
```

```python
import functools
import math

import jax
import jax.numpy as jnp
from jax import lax
from jax.experimental import pallas as pl
from jax.experimental.pallas import tpu as pltpu

F32 = jnp.float32
BF16 = jnp.bfloat16

V7X_VMEM_LIMIT_BYTES = 56 * 1024 * 1024
LANES = 128

RMS_EPS = 1e-6
RWKV_HEAD = 64
RWKV_DECAY_SCALE = math.exp(-0.5)
RWKV_GN_EPS = 64e-5
RWKV_CHUNK = 64
RWKV_CHAINS = 16
MLSTM_HEADS = 4
MLSTM_HEADS_PER_STEP = 2
MLSTM_PROJ_PIECE = 512
MLSTM_GATE_CAP = 15.0
DSA_HEADS = 16
DSA_HEAD_DIM = 128
IDX_HEADS = 16
IDX_DIM = 64
DSA_TOPK_MAX = 256
T5_BUCKETS = 32
T5_MAX_DISTANCE = 128


def _cparams(*sem):
    return pltpu.CompilerParams(dimension_semantics=sem, vmem_limit_bytes=V7X_VMEM_LIMIT_BYTES)


def _dot(a, b):
    return jnp.dot(a.astype(BF16), b.astype(BF16), preferred_element_type=F32)


def _dot_nt(a, b):
    return lax.dot_general(a.astype(BF16), b.astype(BF16), (((1,), (1,)), ((), ())),
                           preferred_element_type=F32)


def _dot_tn(a, b):
    return lax.dot_general(a.astype(BF16), b.astype(BF16), (((0,), (0,)), ((), ())),
                           preferred_element_type=F32)


def _split3(x):
    hi = x.astype(BF16)
    r1 = x - hi.astype(F32)
    mid = r1.astype(BF16)
    lo = (r1 - mid.astype(F32)).astype(BF16)
    return hi, mid, lo


def _dot_exact_lhs(a_bf16, x):
    hi, mid, lo = _split3(x)
    d = lambda p: jnp.dot(a_bf16, p, preferred_element_type=F32)
    return d(hi) + (d(mid) + d(lo))


def _rms(x, g):
    ms = jnp.mean(x * x, axis=-1, keepdims=True)
    return x * lax.rsqrt(ms + RMS_EPS) * g


def _sigmoid(x):
    return 1.0 / (1.0 + jnp.exp(-x))


def _ffn_kernel(x_ref, y_ref, wo_ref, g_ref, wg_ref, wu_ref, wd_ref, o_ref, h_ref):
    @pl.when(pl.program_id(1) == 0)
    def _():
        x1 = x_ref[...] + jnp.dot(y_ref[...], wo_ref[...], preferred_element_type=F32)
        h_ref[...] = _rms(x1, g_ref[...]).astype(BF16)
        o_ref[...] = x1

    h = h_ref[...]
    gate = jnp.dot(h, wg_ref[...], preferred_element_type=F32)
    up = jnp.dot(h, wu_ref[...], preferred_element_type=F32)
    act = (gate * _sigmoid(gate) * up).astype(BF16)
    o_ref[...] += jnp.dot(act, wd_ref[...], preferred_element_type=F32)


def _proj_ffn(x, y, wo, g, wg, wu, wd, tm=512, tf=512):
    m, d = x.shape
    dy = y.shape[1]
    f = wg.shape[1]
    tm, tf = min(tm, m), min(tf, f)
    assert m % tm == 0 and f % tf == 0
    return pl.pallas_call(
        _ffn_kernel, grid=(m // tm, f // tf),
        in_specs=[pl.BlockSpec((tm, d), lambda i, j: (i, 0)),
                  pl.BlockSpec((tm, dy), lambda i, j: (i, 0)),
                  pl.BlockSpec((dy, d), lambda i, j: (0, 0), pipeline_mode=pl.Buffered(1)),
                  pl.BlockSpec((1, d), lambda i, j: (0, 0)),
                  pl.BlockSpec((d, tf), lambda i, j: (0, j)),
                  pl.BlockSpec((d, tf), lambda i, j: (0, j)),
                  pl.BlockSpec((tf, d), lambda i, j: (j, 0))],
        out_specs=pl.BlockSpec((tm, d), lambda i, j: (i, 0)),
        out_shape=jax.ShapeDtypeStruct((m, d), F32),
        scratch_shapes=[pltpu.VMEM((tm, d), BF16)],
        compiler_params=_cparams("parallel", "arbitrary"), name="proj_ffn")(
            x, y, wo, g.reshape(1, d), wg, wu, wd)


def _rwkv_prep_kernel(*refs, seq, tm, has_v):
    if has_v:
        (x_ref, xp_ref, g_ref, mu_ref, w0_ref, w1_ref, w2_ref, a0_ref, a1_ref, a2_ref,
         g1_ref, g2_ref, v0_ref, v1_ref, v2_ref, xs_ref, lw_ref, a_ref, gate_ref, vg_ref) = refs
    else:
        (x_ref, xp_ref, g_ref, mu_ref, w0_ref, w1_ref, w2_ref, a0_ref, a1_ref, a2_ref,
         g1_ref, g2_ref, xs_ref, lw_ref, a_ref, gate_ref) = refs
    i = pl.program_id(0)
    gn = g_ref[...]
    h = _rms(x_ref[...], gn)
    hp = _rms(xp_ref[...], gn)
    seq_start = (i * tm) % seq == 0
    hp_row = jnp.where(seq_start, 0.0, hp[7:8, :])
    row = lax.broadcasted_iota(jnp.int32, (tm, 1), 0)
    h_prev = jnp.where(row == 0, hp_row, pltpu.roll(h, 1, 0))
    xx = h_prev - h
    mix = lambda n: h + xx * mu_ref[n:n + 1, :]
    xs_ref[0] = mix(0).astype(BF16)
    xs_ref[1] = mix(2).astype(BF16)
    xv = mix(3).astype(BF16)
    xs_ref[2] = xv
    lw_ref[...] = -RWKV_DECAY_SCALE * _sigmoid(
        w0_ref[...] + _dot(jnp.tanh(_dot(mix(1), w1_ref[...])), w2_ref[...]))
    a_ref[...] = _sigmoid(a0_ref[...] + _dot(_dot(mix(4), a1_ref[...]), a2_ref[...])).astype(a_ref.dtype)
    gate_ref[...] = _dot(_sigmoid(_dot(mix(5), g1_ref[...])), g2_ref[...]).astype(gate_ref.dtype)
    if has_v:
        vg_ref[...] = _sigmoid(v0_ref[...] + _dot(_dot(xv, v1_ref[...]), v2_ref[...])).astype(vg_ref.dtype)


def _pad_lora(w_in, w_out):
    r = w_in.shape[1]
    rp = -(-r // LANES) * LANES
    return (jnp.pad(w_in, ((0, 0), (0, rp - r))).astype(BF16),
            jnp.pad(w_out, ((0, rp - r), (0, 0))).astype(BF16))


def _rwkv_prep(x, norm_g, mu, w0, w1, w2, a0, a1, a2, g1, g2, vres, seq, tm=256):
    t, d = x.shape
    tm = min(tm, seq)
    assert t % tm == 0 and seq % tm == 0 and tm % 8 == 0
    has_v = vres is not None
    row = lambda v: v.reshape(1, d)
    full = lambda a: pl.BlockSpec(a.shape, lambda i: (0,) * a.ndim)
    w1p, w2p = _pad_lora(w1, w2)
    a1p, a2p = _pad_lora(a1, a2)
    g1p, g2p = _pad_lora(g1, g2)
    mu8 = jnp.pad(mu, ((0, 2), (0, 0)))
    params = [row(norm_g), mu8, row(w0), w1p, w2p, row(a0), a1p, a2p, g1p, g2p]
    if has_v:
        v1p, v2p = _pad_lora(vres[1], vres[2])
        params += [row(vres[0]), v1p, v2p]
    tile = pl.BlockSpec((tm, d), lambda i: (i, 0))
    in_specs = [tile, pl.BlockSpec((8, d), lambda i: (jnp.maximum(i * (tm // 8) - 1, 0), 0))]
    in_specs += [full(p) for p in params]
    n_gates = 3 if has_v else 2
    out_shape = ([jax.ShapeDtypeStruct((3, t, d), BF16), jax.ShapeDtypeStruct((t, d), F32)]
                 + [jax.ShapeDtypeStruct((t, d), BF16)] * n_gates)
    out_specs = [pl.BlockSpec((3, tm, d), lambda i: (0, i, 0))] + [tile] * (1 + n_gates)
    return pl.pallas_call(
        functools.partial(_rwkv_prep_kernel, seq=seq, tm=tm, has_v=has_v),
        grid=(t // tm,), in_specs=in_specs, out_specs=out_specs, out_shape=out_shape,
        compiler_params=_cparams("parallel"), name="rwkv_prep")(x, x, *params)


def _seg_sum(x, seg):
    w = x.shape[1]
    hi = x.astype(BF16)
    lo = (x - hi.astype(F32)).astype(BF16)
    d = lambda p, q: jnp.dot(p[:, q:q + 256], seg, preferred_element_type=F32)
    return jnp.concatenate([d(hi, q) + d(lo, q) for q in range(0, w, 256)], axis=1)


def _rwkv_scan_kernel(*refs, L, tc, has_v):
    (s_ref, a2_ref, r2_ref, b2_ref, k2_ref, v2_ref, pl_ref, phi_ref, psi_ref, theta_ref, yloc_ref,
     y_ref, bonus_ref, mab_ref, tinv_ref, mak_ref, mrb_ref, mrk_ref, av_ref) = refs[-19:]
    refs = refs[:-19]
    if has_v:
        (xs_ref, w_ref, lw_ref, a_ref, gate_ref, vf_ref, vg_ref,
         kk_ref, ka_ref, rk_ref, lng_ref, lnb_ref, o_ref) = refs
    else:
        (xs_ref, w_ref, lw_ref, a_ref, gate_ref,
         kk_ref, ka_ref, rk_ref, lng_ref, lnb_ref, o_ref, vout_ref) = refs
    W = lw_ref.shape[-1]
    P2 = 2 * L
    npair = W // P2
    N = RWKV_HEAD

    @pl.when(pl.program_id(2) == 0)
    def _():
        s_ref[...] = jnp.zeros_like(s_ref)

    ri = lax.broadcasted_iota(jnp.int32, (P2, P2), 0)
    ci = lax.broadcasted_iota(jnp.int32, (P2, P2), 1)
    strict = ri > ci
    incl = ri >= ci
    eye = (ri == ci).astype(F32)
    levels = []
    s = 1
    while s < L:
        levels.append(((ri // s) % 2 == 1) & ((ci // s) == (ri // s) - 1))
        s *= 2
    head0 = lax.broadcasted_iota(jnp.int32, (L, P2), 1) < N
    sr = lax.broadcasted_iota(jnp.int32, (256, 256), 0) // N
    sc = lax.broadcasted_iota(jnp.int32, (256, 256), 1) // N
    seg = (sr == sc).astype(BF16)

    nc = tc // L
    tr = lax.broadcasted_iota(jnp.int32, (tc, tc), 0)
    tcol = lax.broadcasted_iota(jnp.int32, (tc, tc), 1)
    tri = jnp.where(tr >= tcol, jnp.where(tr // L == tcol // L, 1.0, 0.0), 0.0).astype(BF16)
    del tr, tcol

    r, k, v = (jnp.dot(xs_ref[n], w_ref[n], preferred_element_type=F32) for n in range(3))
    lw = lw_ref[...]
    a = a_ref[...].astype(F32)
    if has_v:
        v = v + (vf_ref[...] - v) * vg_ref[...].astype(F32)
    else:
        vout_ref[...] = v
    kk = k * kk_ref[...]
    kk = kk / jnp.maximum(jnp.sqrt(_seg_sum(kk * kk, seg)), 1e-12)
    kmod = k * (1.0 + (a - 1.0) * ka_ref[...])
    c = _dot_exact_lhs(tri, lw)
    enc = jnp.exp(-c)
    bonus_ref[...] = _seg_sum(r * kmod * rk_ref[...], seg) * v
    operands = (-kk * jnp.exp(c - lw), r * jnp.exp(c), kk * a * enc, kmod * enc, v)
    for z, z_ref in zip(operands, (a2_ref, r2_ref, b2_ref, k2_ref, v2_ref)):
        for ch in range(nc):
            for p in range(npair):
                zz = z[ch * L:(ch + 1) * L, P2 * p:P2 * (p + 1)]
                z_ref[ch, p, :L] = jnp.where(head0, zz, 0.0).astype(BF16)
                z_ref[ch, p, L:] = jnp.where(head0, 0.0, zz).astype(BF16)
    for ch in range(nc):
        pl_ref[ch] = jnp.broadcast_to(jnp.exp(c[(ch + 1) * L - 1:(ch + 1) * L, :]), (8, W))
    del r, k, v, lw, a, kk, kmod, c, enc, operands

    chains = [(ch, p) for ch in range(nc) for p in range(npair)]
    for g0 in range(0, len(chains), RWKV_CHAINS):
        group = chains[g0:g0 + RWKV_CHAINS]
        for c in group:
            g = _dot_nt(jnp.concatenate([a2_ref[c], r2_ref[c]], axis=0),
                        jnp.concatenate([b2_ref[c], k2_ref[c]], axis=0))
            m_ab = jnp.where(strict, g[:P2, :P2], 0.0)
            mab_ref[c] = m_ab
            tinv_ref[c] = eye + jnp.where(levels[0], m_ab, 0.0)
            mak_ref[c] = jnp.where(strict, g[:P2, P2:], 0.0).astype(BF16)
            mrb_ref[c] = jnp.where(incl, g[P2:, :P2], 0.0).astype(BF16)
            mrk_ref[c] = jnp.where(incl, g[P2:, P2:], 0.0).astype(BF16)
        for lv in levels[1:]:
            steps = [_dot(jnp.where(lv, mab_ref[c], 0.0), tinv_ref[c]).astype(BF16) for c in group]
            for c, step in zip(group, steps):
                t_cur = tinv_ref[c]
                tinv_ref[c] = t_cur + _dot(t_cur, step)
        mvs = [_dot(mak_ref[c], v2_ref[c]).astype(BF16) for c in group]
        for c, mv in zip(group, mvs):
            av_ref[c] = _dot(tinv_ref[c], jnp.concatenate([a2_ref[c], mv], axis=1)).astype(BF16)
        ths = [_dot(mrb_ref[c], av_ref[c]) for c in group]
        yls = [_dot(mrk_ref[c], v2_ref[c]) for c in group]
        for c, th, yl in zip(group, ths, yls):
            theta_ref[c] = (r2_ref[c].astype(F32) + th[:, :P2]).astype(BF16)
            yloc_ref[c] = th[:, P2:] + yl
        for c in group:
            av = av_ref[c]
            pp = _dot_tn(jnp.concatenate([av[:, P2:], av[:, :P2]], axis=1), b2_ref[c])
            p_last = pl_ref[c[0]][0:1, P2 * c[1]:P2 * (c[1] + 1)]
            phi_ref[c] = ((eye + pp[P2:]) * p_last).astype(BF16)
            psi_ref[c] = (pp[:P2] + _dot_tn(v2_ref[c], k2_ref[c])) * p_last

    for ch in range(nc):
        for p in range(npair):
            s0 = s_ref[p]
            yo = _dot_nt(theta_ref[ch, p], s0) + yloc_ref[ch, p]
            y_ref[ch * L:(ch + 1) * L, P2 * p:P2 * (p + 1)] = yo[:L] + yo[L:]
            s_ref[p] = _dot(s0, phi_ref[ch, p]) + psi_ref[ch, p]

    y = y_ref[...]
    mean = _seg_sum(y, seg) * (1.0 / N)
    yc = y - mean
    var = _seg_sum(yc * yc, seg) * (1.0 / N)
    yn = yc * lax.rsqrt(var + RWKV_GN_EPS) * lng_ref[...] + lnb_ref[...]
    o_ref[...] = ((yn + bonus_ref[...]) * gate_ref[...].astype(F32)).astype(o_ref.dtype)


def _rwkv_scan(xs, w_rkv, lw, a, gate, v_first, vgate, k_k, k_a, r_k, lnx_g, lnx_b, bsz, seq, tc=256, wb=512):
    _, t, d = xs.shape
    L = RWKV_CHUNK
    tc, wb = min(tc, seq), min(wb, d)
    assert seq % tc == 0 and tc % L == 0 and d % wb == 0 and wb % 256 == 0
    has_v = v_first is not None
    nc, p2, npair = tc // L, 2 * L, wb // (2 * L)
    b3 = lambda z: z.reshape(bsz, seq, d)
    blk = pl.BlockSpec((None, tc, wb), lambda b, h, c: (b, c, h))
    prow = pl.BlockSpec((1, wb), lambda b, h, c: (0, h))
    args = [xs.reshape(3, bsz, seq, d), w_rkv, b3(lw), b3(a), b3(gate)]
    in_specs = [pl.BlockSpec((3, None, tc, d), lambda b, h, c: (0, b, c, 0)),
                pl.BlockSpec((3, d, wb), lambda b, h, c: (0, 0, h)),
                blk, blk, blk]
    if has_v:
        args += [b3(v_first), b3(vgate)]
        in_specs += [blk, blk]
    args += [z.reshape(1, d) for z in (k_k, k_a, r_k, lnx_g, lnx_b)]
    in_specs += [prow] * 5
    out_shape = [jax.ShapeDtypeStruct((bsz, seq, d), BF16)]
    if not has_v:
        out_shape.append(jax.ShapeDtypeStruct((bsz, seq, d), F32))
    out = pl.pallas_call(
        functools.partial(_rwkv_scan_kernel, L=L, tc=tc, has_v=has_v),
        grid=(bsz, d // wb, seq // tc), in_specs=in_specs, out_specs=[blk] * len(out_shape),
        out_shape=out_shape,
        scratch_shapes=[pltpu.VMEM((npair, p2, p2), F32)]
        + [pltpu.VMEM((nc, npair, p2, p2), BF16)] * 5
        + [pltpu.VMEM((nc, 8, wb), F32),
           pltpu.VMEM((nc, npair, p2, p2), BF16), pltpu.VMEM((nc, npair, p2, p2), F32),
           pltpu.VMEM((nc, npair, p2, p2), BF16), pltpu.VMEM((nc, npair, p2, p2), F32),
           pltpu.VMEM((tc, wb), F32), pltpu.VMEM((tc, wb), F32),
           pltpu.VMEM((nc, npair, p2, p2), F32), pltpu.VMEM((nc, npair, p2, p2), F32),
           pltpu.VMEM((nc, npair, p2, p2), BF16), pltpu.VMEM((nc, npair, p2, p2), BF16),
           pltpu.VMEM((nc, npair, p2, p2), BF16),
           pltpu.VMEM((nc, npair, p2, 2 * p2), BF16)],
        compiler_params=_cparams("parallel", "parallel", "arbitrary"), name="rwkv_scan")(*args)
    return out[0].reshape(t, d), (None if has_v else out[1].reshape(t, d))


def _rwkv_layer(x, norm_g, p, vres, v_first, bsz, seq):
    (mu, w_rkv, w0, w1, w2, a0, a1, a2, g1, g2, k_k, k_a, r_k, lnx_g, lnx_b, w_o) = p
    outs = _rwkv_prep(x, norm_g, mu, w0, w1, w2, a0, a1, a2, g1, g2, vres, seq)
    xs, lw, a, gate = outs[:4]
    vgate = outs[4] if vres is not None else None
    o, v_own = _rwkv_scan(xs, w_rkv.astype(BF16), lw, a, gate, v_first if vres is not None else None, vgate,
                          k_k, k_a, r_k, lnx_g, lnx_b, bsz, seq)
    return o, w_o.astype(BF16), (v_own if vres is None else v_first)


def _mlstm_kernel(x_ref, xn_ref, mg_ref, w_ref, wg_ref, bias_ref, ng_ref, out_ref, c_ref, m_ref, p_ref, z_ref,
                  *, nh, hps):
    lc = x_ref.shape[0]
    dv = ng_ref.shape[1] // hps
    dk = dv // 2
    step = pl.program_id(2)

    def project(src_ref, slot):
        hb = _rms(src_ref[...], mg_ref[...]).astype(BF16)
        p_ref[slot] = jnp.dot(hb, w_ref[...], preferred_element_type=F32).astype(BF16)
        z_ref[slot] = jnp.dot(hb, wg_ref[...], preferred_element_type=F32)

    @pl.when(step == 0)
    def _():
        c_ref[...] = jnp.zeros_like(c_ref)
        m_ref[...] = jnp.zeros_like(m_ref)
        project(x_ref, 0)

    cur = step % 2
    proj = p_ref[cur]
    z = z_ref[cur] + bias_ref[...]
    hb_next = _rms(xn_ref[...], mg_ref[...]).astype(BF16)
    z_ref[1 - cur] = jnp.dot(hb_next, wg_ref[...], preferred_element_type=F32)
    pending = list(range(0, w_ref.shape[1], MLSTM_PROJ_PIECE))

    def project_piece():
        if pending:
            c0 = pending.pop(0)
            cols = slice(c0, c0 + MLSTM_PROJ_PIECE)
            p_ref[1 - cur, :, cols] = jnp.dot(hb_next, w_ref[:, cols], preferred_element_type=F32).astype(BF16)

    q_of = lambda s: proj[:, s * dk:(s + 1) * dk]
    k_of = lambda s: proj[:, (hps + s) * dk:(hps + s + 1) * dk]
    v_of = lambda s: proj[:, 2 * hps * dk + s * dv:2 * hps * dk + (s + 1) * dv]
    o_of = lambda s: proj[:, 2 * hps * dk + (hps + s) * dv:2 * hps * dk + (hps + s + 1) * dv]

    lane = lax.broadcasted_iota(jnp.int32, (lc, LANES), 1)
    zc = MLSTM_GATE_CAP * jnp.tanh(z / MLSTM_GATE_CAP)
    lf_all = jnp.minimum(zc, 0.0) - jnp.log1p(jnp.exp(-jnp.abs(zc)))
    rr = lax.broadcasted_iota(jnp.int32, (lc, lc), 0)
    cc = lax.broadcasted_iota(jnp.int32, (lc, lc), 1)
    causal = rr >= cc
    bcum_all = _dot_exact_lhs(causal.astype(BF16), lf_all)
    project_piece()
    comb = jnp.where(lane < nh, zc, bcum_all)
    er = lax.broadcasted_iota(jnp.int32, (8, LANES), 0)
    ec = lax.broadcasted_iota(jnp.int32, (8, LANES), 1)
    head_of_row = pl.program_id(1) * hps + er // 2
    sel = jnp.where(er < 2 * hps, jnp.where(ec == head_of_row + nh * (er % 2), 1.0, 0.0), 0.0).astype(BF16)
    hi, mid, lo = _split3(comb)
    tr = lambda p: lax.dot_general(sel, p, (((1,), (1,)), ((), ())), preferred_element_type=F32)
    rows = tr(hi) + (tr(mid) + tr(lo))
    project_piece()
    ones_blk = (lane == 0).astype(BF16)

    for s in range(hps):
        h = pl.program_id(1) * hps + s
        li_col = jnp.sum(jnp.where(lane == h, comb, 0.0), axis=-1, keepdims=True)
        bc_col = jnp.sum(jnp.where(lane == h + nh, comb, 0.0), axis=-1, keepdims=True)
        li_row, bc_row = rows[2 * s:2 * s + 1], rows[2 * s + 1:2 * s + 2]
        m_st = m_ref[s, 0:1, 0:1]
        dmat = jnp.where(causal, bc_col - bc_row + li_row, -jnp.inf)
        inter = bc_col + m_st
        m_t = jnp.maximum(inter, jnp.max(dmat, axis=-1, keepdims=True))
        q = q_of(s) * (dk ** -0.5)
        k = k_of(s)
        sc = _dot_nt(q, k) * jnp.exp(dmat - m_t)
        project_piece()
        w_inter = jnp.exp(inter - m_t)
        v_ext = jnp.concatenate([v_of(s), ones_blk], axis=1)
        c_st = c_ref[s]
        nd = _dot(sc, v_ext) + w_inter * _dot(q, c_st)
        project_piece()
        den = nd[:, dv:dv + 1]
        hc = nd[:, :dv] / jnp.maximum(jnp.abs(den), jnp.exp(-m_t))

        b_tot = bc_col[lc - 1:lc]
        log_wk = b_tot - bc_col + li_col
        m_new = jnp.maximum(b_tot + m_st, jnp.max(log_wk, axis=0, keepdims=True))
        c_ref[s] = jnp.exp(b_tot + m_st - m_new) * c_st + _dot_tn(k * jnp.exp(log_wk - m_new), v_ext)
        m_ref[s] = jnp.broadcast_to(m_new, m_ref.shape[1:])
        project_piece()

        hn = hc * lax.rsqrt(jnp.mean(hc * hc, axis=-1, keepdims=True) + RMS_EPS) * ng_ref[:, s * dv:(s + 1) * dv]
        o_gate = _sigmoid(o_of(s).astype(F32))
        out_ref[:, s * dv:(s + 1) * dv] = (hn * o_gate).astype(out_ref.dtype)
    while pending:
        project_piece()


def _mlstm_layer(x, mix_g, w_in, b_if, norm_g, w_o, bsz, seq, lc=256):
    t, d = x.shape
    nh = MLSTM_HEADS
    dk, dv = d // 2 // nh, d // nh
    nq = 2 * nh * dk + 2 * nh * dv
    assert w_in.shape[1] == nq + 2 * nh and dk % LANES == 0
    lc = min(lc, seq)
    assert seq % lc == 0
    w_gates = jnp.pad(w_in[:, nq:], ((0, 0), (0, LANES - 2 * nh))).astype(BF16)
    bias = jnp.zeros((1, LANES), F32).at[0, :nh].set(b_if[0]).at[0, nh:2 * nh].set(b_if[1])
    hps = MLSTM_HEADS_PER_STEP
    assert nh % hps == 0
    ng = nh // hps
    heads = lambda base, width, g: w_in[:, base + g * hps * width:base + (g + 1) * hps * width]
    w_groups = jnp.stack([jnp.concatenate([heads(0, dk, g), heads(nh * dk, dk, g), heads(2 * nh * dk, dv, g),
                                           heads(2 * nh * dk + nh * dv, dv, g)], axis=1)
                          for g in range(ng)]).astype(BF16)
    out = pl.pallas_call(
        functools.partial(_mlstm_kernel, nh=nh, hps=hps),
        grid=(bsz, ng, seq // lc),
        in_specs=[pl.BlockSpec((None, lc, d), lambda b, h, c: (b, c, 0)),
                  pl.BlockSpec((None, lc, d), lambda b, h, c: (b, jnp.minimum(c + 1, seq // lc - 1), 0)),
                  pl.BlockSpec((1, d), lambda b, h, c: (0, 0)),
                  pl.BlockSpec((None, d, w_groups.shape[2]), lambda b, h, c: (h, 0, 0)),
                  pl.BlockSpec((d, LANES), lambda b, h, c: (0, 0)),
                  pl.BlockSpec((1, LANES), lambda b, h, c: (0, 0)),
                  pl.BlockSpec((1, hps * dv), lambda b, h, c: (0, h))],
        out_specs=pl.BlockSpec((None, lc, hps * dv), lambda b, h, c: (b, c, h)),
        out_shape=jax.ShapeDtypeStruct((bsz, seq, nh * dv), BF16),
        scratch_shapes=[pltpu.VMEM((hps, dk, dv + LANES), F32), pltpu.VMEM((hps, 8, LANES), F32),
                        pltpu.VMEM((2, lc, w_groups.shape[2]), BF16), pltpu.VMEM((2, lc, LANES), F32)],
        compiler_params=_cparams("parallel", "parallel", "arbitrary"), name="mlstm")(
            x.reshape(bsz, seq, d), x.reshape(bsz, seq, d), mix_g.reshape(1, d), w_groups, w_gates, bias,
            norm_g.reshape(1, nh * dv))
    return out.reshape(t, nh * dv), w_o.astype(BF16)


DSA_TQ = 128
DSA_TK = 256
DSA_HEAD_GROUP = 4
DSA_TILES_PER_TRIP = 4
DSA_PROJ_GROUPS = 4
DSA_NEG = -1e30
KEY_NEG_INF = -2139095041
IDX_BIG = 2 ** 30
LOG2E = math.log2(math.e)


def _dsa_proj_kernel(x_ref, g_ref, w_ref, qg_ref, kg_ref, qn_ref, qi_ref, k_ref, v_ref, ki_ref, wi_ref, h_ref,
                     *, nh, nih):
    dh = DSA_HEAD_DIM
    gs = DSA_PROJ_GROUPS
    j = pl.program_id(2)
    nq, nqi = nh // gs, nih // gs

    @pl.when(j == 0)
    def _():
        h_ref[...] = _rms(x_ref[...], g_ref[...]).astype(BF16)

    y = jnp.dot(h_ref[...], w_ref[...], preferred_element_type=F32)
    part = lambda n: y[:, n * dh:(n + 1) * dh]

    @pl.when(j < nq)
    def _():
        qscale = dh ** -0.5 * LOG2E
        for n in range(gs):
            qn_ref[n] = (_rms(part(n), qg_ref[...]) * qscale).astype(BF16)

    @pl.when(jnp.logical_and(j >= nq, j < nq + nqi))
    def _():
        for n in range(gs):
            qi_ref[n] = part(n).astype(BF16)

    @pl.when(j == nq + nqi)
    def _():
        k_ref[...] = _rms(part(0), kg_ref[...]).astype(BF16)
        v_ref[:, :dh] = part(1).astype(BF16)
        v_ref[:, dh:] = jnp.ones((v_ref.shape[0], LANES), BF16)
        ki_ref[...] = part(2).astype(BF16)
        wi_ref[...] = part(3)


def _dsa_kernel(qn_ref, qi_ref, wi_ref, k_ref, v_ref, ki_ref, nb_ref, o_ref,
                key_ref, hi_ref, lo_ref, lo2_ref, w_ref, acc_ref, m_ref, *, nh, nih, n_sel, idx_bits):
    tq, tk, dh = DSA_TQ, DSA_TK, DSA_HEAD_DIM
    hg = DSA_HEAD_GROUP
    q0 = pl.program_id(1) * tq
    jd = (q0 + tq - 1) // tk
    nt = jd + 1
    rowi = lax.broadcasted_iota(jnp.int32, (tq, tk), 0)
    coli = lax.broadcasted_iota(jnp.int32, (tq, tk), 1)
    ktile = lambda jt: pl.ds(pl.multiple_of(jt * tk, tk), tk)
    twice = lambda z: jnp.concatenate([z] * (tk // LANES), axis=1)

    wi = wi_ref[...]
    for h in range(nih):
        w_ref[h] = jnp.broadcast_to(wi[:, h:h + 1], (tq, LANES))

    def for_each_tile(n, body, per_trip=DSA_TILES_PER_TRIP):
        def trip(i, c):
            for u in range(per_trip):
                body(per_trip * i + u)
            return c
        lax.fori_loop(0, n // per_trip, trip, 0)
        done = (n // per_trip) * per_trip
        for u in range(per_trip - 1):
            @pl.when(n - done > u)
            def _():
                body(done + u)

    def score_tile(jt):
        ki_t = ki_ref[ktile(jt), :]
        score = jnp.zeros((tq, tk), F32)
        for g0 in range(0, nih, hg):
            lg = _dot_nt(qi_ref[g0:g0 + hg].reshape(hg * tq, LANES), ki_t)
            for h in range(hg):
                score = score + jnp.maximum(lg[h * tq:(h + 1) * tq], 0.0) * twice(w_ref[g0 + h])
        score = jnp.where(jt * tk + coli <= q0 + rowi, score, -jnp.inf)
        bits = pltpu.bitcast(score, jnp.int32)
        key = bits ^ ((bits >> 31) & 0x7FFFFFFF)
        key_ref[jt] = key
        key_t = pltpu.bitcast(jnp.transpose(pltpu.bitcast(key, F32)), jnp.int32)
        hi_ref[jt] = (key_t >> 16).astype(jnp.int16)
        lo_ref[jt] = ((key_t & 0xFFFF) - 32768).astype(jnp.int16)

    for_each_tile(nt, score_tile)

    unroll = DSA_TILES_PER_TRIP
    n_trips = (nt + unroll - 1) // unroll
    floor16 = jnp.full((tk, tq), -32768, jnp.int16)

    def pad_tiles(ref):
        for u in range(unroll - 1):
            @pl.when(n_trips * unroll - nt > u)
            def _():
                ref[nt + u] = floor16

    def count16(ref, cand):
        cb = jnp.broadcast_to(cand.astype(jnp.int16), (32, tq))

        def body(i, acc):
            for u in range(unroll):
                tile = ref[unroll * i + u]
                for r0 in range(0, tk, 32):
                    acc = acc + jnp.where(tile[r0:r0 + 32] >= cb, jnp.int16(1), jnp.int16(0))
            return acc
        acc = lax.fori_loop(0, n_trips, body, jnp.zeros((32, tq), jnp.int16))
        return jnp.sum(acc.astype(F32), axis=0, keepdims=True)

    pad_tiles(hi_ref)

    def kth_largest16(ref, kth):
        def bit(b, prefix):
            cand = prefix + jnp.left_shift(jnp.int32(1), 15 - b)
            return jnp.where(count16(ref, cand) >= kth, cand, prefix)
        return lax.fori_loop(0, 16, bit, jnp.full((1, tq), -32768, jnp.int32))

    hi_k = kth_largest16(hi_ref, n_sel)
    above = jnp.where(hi_k >= 32767, 0.0, count16(hi_ref, jnp.minimum(hi_k + 1, 32767)))
    hi_kb = jnp.broadcast_to(hi_k.astype(jnp.int16), (tk, tq))

    def bucket_tile(jt, c):
        lo2_ref[jt] = jnp.where(hi_ref[jt] == hi_kb, lo_ref[jt], floor16)
        return c

    lax.fori_loop(0, nt, bucket_tile, 0)
    pad_tiles(lo2_ref)
    lo_k = kth_largest16(lo2_ref, n_sel - above)
    thr_row = hi_k * 65536 + (lo_k + 32768)
    tw = pltpu.bitcast(jnp.transpose(pltpu.bitcast(jnp.broadcast_to(thr_row, (tq, tq)), F32)), jnp.int32)
    thr = tw[:, :1]

    def count(hit):
        def body(jt, acc):
            keyt = key_ref[jt]
            for c0 in range(0, tk, LANES):
                acc = acc + hit(keyt[:, c0:c0 + LANES], jt * tk + c0)
            return acc
        acc = lax.fori_loop(0, nt, body, jnp.zeros((tq, LANES), F32))
        return jnp.sum(acc, axis=-1, keepdims=True)

    wide = lambda col: jnp.broadcast_to(col, (tq, LANES))
    lane = lax.broadcasted_iota(jnp.int32, (tq, LANES), 1)

    c_gt = count(lambda kv, base: jnp.where(kv > tw, 1.0, 0.0))
    c_ge = count(lambda kv, base: jnp.where(kv >= tw, 1.0, 0.0))
    need = n_sel - c_gt
    c_eq = c_ge - c_gt

    def tie_search():
        def index_bit(b, jcur):
            cand = jcur + jnp.left_shift(jnp.int32(1), idx_bits - 1 - b)
            cw = wide(cand)
            f = count(lambda kv, base: jnp.where(kv == tw, jnp.where(base + lane < cw, 1.0, 0.0), 0.0))
            return jnp.where(f <= need, cand, jcur)
        return lax.fori_loop(0, idx_bits, index_bit, jnp.zeros((tq, 1), jnp.int32))

    excess = jnp.max(c_eq - need) > 0.0
    jc = lax.cond(excess, tie_search, lambda: jnp.full((tq, 1), IDX_BIG, jnp.int32))
    jc = jnp.where(c_eq > need, jc, IDX_BIG)
    jc = jnp.where(thr == KEY_NEG_INF, 0, jc)

    m_ref[...] = jnp.full_like(m_ref, DSA_NEG)
    acc_ref[...] = jnp.zeros_like(acc_ref)

    def attend_tile(jt, near_idx):
        k_t = k_ref[ktile(jt), :]
        v_t = v_ref[ktile(jt), :]
        keyt = key_ref[jt]
        tie_mb = jnp.where(keyt == thr, jnp.where(jt * tk + coli < jc, 0.0, DSA_NEG), DSA_NEG)
        mb = jnp.where(keyt > thr, 0.0, tie_mb)
        for g0 in range(0, nh, hg):
            s_g = _dot_nt(qn_ref[g0:g0 + hg].reshape(hg * tq, dh), k_t)
            ps, alphas = [], []
            for h in range(hg):
                rows = slice((g0 + h) * tq, (g0 + h + 1) * tq)
                s = s_g[h * tq:(h + 1) * tq] + (mb if near_idx is None else nb_ref[near_idx, g0 + h] + mb)
                m_old = m_ref[rows, :]
                m_new = jnp.maximum(m_old, jnp.max(s, axis=-1, keepdims=True))
                m_ref[rows, :] = m_new
                ps.append(jnp.exp2(s - twice(m_new)).astype(BF16))
                alphas.append(jnp.exp2(m_old - m_new))
            pv = jnp.dot(jnp.concatenate(ps, axis=0), v_t, preferred_element_type=F32)
            rows_g = slice(g0 * tq, (g0 + hg) * tq)
            acc_ref[rows_g, :] = acc_ref[rows_g, :] * twice(jnp.concatenate(alphas, axis=0)) + pv

    on_tile_edge = q0 == jd * tk
    has_prev_near = jnp.logical_and(on_tile_edge, jd >= 1)
    n_far = jnp.where(has_prev_near, jd - 1, jd)

    for_each_tile(n_far, lambda jt: attend_tile(jt, None))

    @pl.when(has_prev_near)
    def _():
        attend_tile(jd - 1, 2)
        attend_tile(jd, 0)

    @pl.when(jnp.logical_not(has_prev_near))
    def _():
        attend_tile(jd, jnp.where(on_tile_edge, 0, 1))

    for h in range(nh):
        rows = slice(h * tq, (h + 1) * tq)
        o_ref[:, h * dh:(h + 1) * dh] = (acc_ref[rows, :dh] / acc_ref[rows, dh:]).astype(o_ref.dtype)


def _t5_bucket(rel):
    n = jnp.maximum(rel, 0)
    exact = T5_BUCKETS // 2
    nf = jnp.maximum(n, exact).astype(F32)
    large = exact + (jnp.log(nf / exact) / math.log(T5_MAX_DISTANCE / exact)
                     * (T5_BUCKETS - exact)).astype(jnp.int32)
    return jnp.where(n < exact, n, jnp.minimum(large, T5_BUCKETS - 1))


def _dsa_layer(x, mix_g, w_in, q_norm_g, k_norm_g, t5_table, w_o, bsz, seq):
    t, d = x.shape
    nh, dh, nih, di = DSA_HEADS, DSA_HEAD_DIM, IDX_HEADS, IDX_DIM
    tq, tk = DSA_TQ, DSA_TK
    assert seq % (tk * DSA_TILES_PER_TRIP) == 0 and tk == 2 * tq and dh == LANES and di <= LANES and nih <= LANES
    assert T5_MAX_DISTANCE <= tq
    n_sel = min(DSA_TOPK_MAX, seq // 4)
    o1, o2, o3, o4, o5 = nh * dh, nh * dh + dh, nh * dh + 2 * dh, nh * dh + 2 * dh + nih * di, nh * dh + 2 * dh + nih * di + di
    w_qi = jnp.pad(w_in[:, o3:o4].reshape(d, nih, di), ((0, 0), (0, 0), (0, LANES - di))).reshape(d, nih * LANES)
    w_ki = jnp.pad(w_in[:, o4:o5], ((0, 0), (0, LANES - di)))
    w_wi = jnp.pad(w_in[:, o5:o5 + nih] * (nih ** -0.5 * di ** -0.5), ((0, 0), (0, LANES - nih)))
    w = jnp.concatenate([w_in[:, :o1], w_qi, w_in[:, o1:o3], w_ki, w_wi], axis=1).astype(BF16)
    gs = DSA_PROJ_GROUPS
    nq, nqi = nh // gs, nih // gs
    assert gs == 4 and w.shape[1] == gs * dh * (nq + nqi + 1) and nh % gs == 0 and nih % gs == 0
    tm = min(1024, seq)
    assert seq % tm == 0
    tok = lambda width: pl.BlockSpec((None, tm, width), lambda b, i, j: (b, i, 0))
    qn, qi, kn, vb, kib, wis = pl.pallas_call(
        functools.partial(_dsa_proj_kernel, nh=nh, nih=nih),
        grid=(bsz, seq // tm, nq + nqi + 1),
        in_specs=[pl.BlockSpec((None, tm, d), lambda b, i, j: (b, i, 0)),
                  pl.BlockSpec((1, d), lambda b, i, j: (0, 0)),
                  pl.BlockSpec((d, gs * dh), lambda b, i, j: (0, j)),
                  pl.BlockSpec((1, dh), lambda b, i, j: (0, 0)),
                  pl.BlockSpec((1, dh), lambda b, i, j: (0, 0))],
        out_specs=[pl.BlockSpec((None, gs, tm, dh), lambda b, i, j: (b, jnp.minimum(j, nq - 1), i, 0)),
                   pl.BlockSpec((None, gs, tm, LANES), lambda b, i, j: (b, jnp.clip(j - nq, 0, nqi - 1), i, 0)),
                   tok(dh), tok(dh + LANES), tok(LANES), tok(LANES)],
        out_shape=[jax.ShapeDtypeStruct((bsz, nh, seq, dh), BF16),
                   jax.ShapeDtypeStruct((bsz, nih, seq, LANES), BF16),
                   jax.ShapeDtypeStruct((bsz, seq, dh), BF16),
                   jax.ShapeDtypeStruct((bsz, seq, dh + LANES), BF16),
                   jax.ShapeDtypeStruct((bsz, seq, LANES), BF16),
                   jax.ShapeDtypeStruct((bsz, seq, LANES), F32)],
        scratch_shapes=[pltpu.VMEM((tm, d), BF16)],
        compiler_params=_cparams("parallel", "parallel", "arbitrary"), name="dsa_proj")(
            x.reshape(bsz, seq, d), mix_g.reshape(1, d), w, q_norm_g.reshape(1, dh), k_norm_g.reshape(1, dh))

    ii = jnp.arange(tq, dtype=jnp.int32)[:, None]
    jj = jnp.arange(tk, dtype=jnp.int32)[None, :]
    buckets = jnp.stack([_t5_bucket(off + ii - jj) for off in (0, tq, 2 * tq)])
    rel_table = (t5_table - t5_table[T5_BUCKETS - 1]).astype(F32)
    near = jnp.einsum("otkb,bh->ohtk", jax.nn.one_hot(buckets, T5_BUCKETS, dtype=F32), rel_table,
                      precision=lax.Precision.HIGHEST) * LOG2E

    seqblk = lambda: pl.BlockSpec((None, seq, LANES), lambda b, i: (b, 0, 0))
    out = pl.pallas_call(
        functools.partial(_dsa_kernel, nh=nh, nih=nih, n_sel=n_sel, idx_bits=int(seq).bit_length()),
        grid=(bsz, seq // tq),
        in_specs=[pl.BlockSpec((None, nh, tq, dh), lambda b, i: (b, 0, i, 0)),
                  pl.BlockSpec((None, nih, tq, LANES), lambda b, i: (b, 0, i, 0)),
                  pl.BlockSpec((None, tq, LANES), lambda b, i: (b, i, 0)),
                  seqblk(), pl.BlockSpec((None, seq, dh + LANES), lambda b, i: (b, 0, 0)), seqblk(),
                  pl.BlockSpec((3, nh, tq, tk), lambda b, i: (0, 0, 0, 0))],
        out_specs=pl.BlockSpec((None, tq, nh * dh), lambda b, i: (b, i, 0)),
        out_shape=jax.ShapeDtypeStruct((bsz, seq, nh * dh), BF16),
        scratch_shapes=[pltpu.VMEM((seq // tk, tq, tk), jnp.int32),
                        pltpu.VMEM((seq // tk, tk, tq), jnp.int16),
                        pltpu.VMEM((seq // tk, tk, tq), jnp.int16),
                        pltpu.VMEM((seq // tk, tk, tq), jnp.int16),
                        pltpu.VMEM((nih, tq, LANES), F32),
                        pltpu.VMEM((nh * tq, dh + LANES), F32),
                        pltpu.VMEM((nh * tq, LANES), F32)],
        compiler_params=_cparams("parallel", "arbitrary"), name="dsa_attn")(
            qn, qi, wis, kn, vb, kib, near)
    return out.reshape(t, nh * dh), w_o.astype(BF16)


def kernel(x, rwkv_mu, rwkv_w_rkv, rwkv_w0, rwkv_w1, rwkv_w2, rwkv_a0, rwkv_a1, rwkv_a2, rwkv_v0, rwkv_v1,
           rwkv_v2, rwkv_g1, rwkv_g2, rwkv_k_k, rwkv_k_a, rwkv_r_k, rwkv_lnx_g, rwkv_lnx_b, rwkv_w_o,
           mlstm_w_in, mlstm_b_if, mlstm_norm_g, mlstm_w_o, dsa_w_in, dsa_q_norm_g, dsa_k_norm_g, dsa_w_o,
           t5_bias, mix_norm_g, ffn_norm_g, ffn_w_gate, ffn_w_up, ffn_w_down):
    bsz, seq, d = x.shape
    depth = mix_norm_g.shape[0]
    h = x.reshape(bsz * seq, d)
    v_first = None
    for i in range(depth):
        kind, j = i % 3, i // 3
        if kind == 0:
            vres = None if j == 0 else (rwkv_v0[j - 1], rwkv_v1[j - 1], rwkv_v2[j - 1])
            p = (rwkv_mu[j], rwkv_w_rkv[j], rwkv_w0[j], rwkv_w1[j], rwkv_w2[j], rwkv_a0[j], rwkv_a1[j],
                 rwkv_a2[j], rwkv_g1[j], rwkv_g2[j], rwkv_k_k[j], rwkv_k_a[j], rwkv_r_k[j],
                 rwkv_lnx_g[j], rwkv_lnx_b[j], rwkv_w_o[j])
            y, w_o, v_first = _rwkv_layer(h, mix_norm_g[i], p, vres, v_first, bsz, seq)
        elif kind == 1:
            y, w_o = _mlstm_layer(h, mix_norm_g[i], mlstm_w_in[j], mlstm_b_if[j], mlstm_norm_g[j],
                                  mlstm_w_o[j], bsz, seq)
        else:
            y, w_o = _dsa_layer(h, mix_norm_g[i], dsa_w_in[j], dsa_q_norm_g[j], dsa_k_norm_g[j], t5_bias,
                                dsa_w_o[j], bsz, seq)
        h = _proj_ffn(h, y, w_o, ffn_norm_g[i], ffn_w_gate[i].astype(BF16), ffn_w_up[i].astype(BF16),
                      ffn_w_down[i].astype(BF16))
    return h.reshape(bsz, seq, d)
```

```python
import functools
import math

import jax
import jax.numpy as jnp
from jax import lax
from jax.experimental import pallas as pl
from jax.experimental.pallas import tpu as pltpu

F32 = jnp.float32
BF16 = jnp.bfloat16

V7X_VMEM_LIMIT_BYTES = 56 * 1024 * 1024
LANES = 128

RMS_EPS = 1e-6
RWKV_HEAD = 64
RWKV_DECAY_SCALE = math.exp(-0.5)
RWKV_GN_EPS = 64e-5
RWKV_CHUNK = 64
RWKV_CHAINS = 16
RWKV_PROJ_PIECE = 256
MLSTM_HEADS = 4
MLSTM_HEADS_PER_STEP = 2
MLSTM_PROJ_PIECE = 512
MLSTM_GATE_CAP = 15.0
DSA_HEADS = 16
DSA_HEAD_DIM = 128
IDX_HEADS = 16
IDX_DIM = 64
DSA_TOPK_MAX = 256
T5_BUCKETS = 32
T5_MAX_DISTANCE = 128


def _cparams(*sem):
    return pltpu.CompilerParams(dimension_semantics=sem, vmem_limit_bytes=V7X_VMEM_LIMIT_BYTES)


def _dot(a, b):
    return jnp.dot(a.astype(BF16), b.astype(BF16), preferred_element_type=F32)


def _dot_nt(a, b):
    return lax.dot_general(a.astype(BF16), b.astype(BF16), (((1,), (1,)), ((), ())),
                           preferred_element_type=F32)


def _dot_tn(a, b):
    return lax.dot_general(a.astype(BF16), b.astype(BF16), (((0,), (0,)), ((), ())),
                           preferred_element_type=F32)


def _split3(x):
    hi = x.astype(BF16)
    r1 = x - hi.astype(F32)
    mid = r1.astype(BF16)
    lo = (r1 - mid.astype(F32)).astype(BF16)
    return hi, mid, lo


def _dot_exact_lhs(a_bf16, x):
    hi, mid, lo = _split3(x)
    d = lambda p: jnp.dot(a_bf16, p, preferred_element_type=F32)
    return d(hi) + (d(mid) + d(lo))


def _rms(x, g):
    ms = jnp.mean(x * x, axis=-1, keepdims=True)
    return x * lax.rsqrt(ms + RMS_EPS) * g


def _sigmoid(x):
    return 1.0 / (1.0 + jnp.exp(-x))


def _ffn_kernel(x_ref, y_ref, wo_ref, g_ref, wg_ref, wu_ref, wd_ref, o_ref, h_ref):
    @pl.when(pl.program_id(1) == 0)
    def _():
        x1 = x_ref[...] + jnp.dot(y_ref[...], wo_ref[...], preferred_element_type=F32)
        h_ref[...] = _rms(x1, g_ref[...]).astype(BF16)
        o_ref[...] = x1

    h = h_ref[...]
    gate = jnp.dot(h, wg_ref[...], preferred_element_type=F32)
    up = jnp.dot(h, wu_ref[...], preferred_element_type=F32)
    act = (gate * _sigmoid(gate) * up).astype(BF16)
    o_ref[...] += jnp.dot(act, wd_ref[...], preferred_element_type=F32)


def _proj_ffn(x, y, wo, g, wg, wu, wd, tm=512, tf=512):
    m, d = x.shape
    dy = y.shape[1]
    f = wg.shape[1]
    tm, tf = min(tm, m), min(tf, f)
    assert m % tm == 0 and f % tf == 0
    return pl.pallas_call(
        _ffn_kernel, grid=(m // tm, f // tf),
        in_specs=[pl.BlockSpec((tm, d), lambda i, j: (i, 0)),
                  pl.BlockSpec((tm, dy), lambda i, j: (i, 0)),
                  pl.BlockSpec((dy, d), lambda i, j: (0, 0), pipeline_mode=pl.Buffered(1)),
                  pl.BlockSpec((1, d), lambda i, j: (0, 0)),
                  pl.BlockSpec((d, tf), lambda i, j: (0, j)),
                  pl.BlockSpec((d, tf), lambda i, j: (0, j)),
                  pl.BlockSpec((tf, d), lambda i, j: (j, 0))],
        out_specs=pl.BlockSpec((tm, d), lambda i, j: (i, 0)),
        out_shape=jax.ShapeDtypeStruct((m, d), F32),
        scratch_shapes=[pltpu.VMEM((tm, d), BF16)],
        compiler_params=_cparams("parallel", "arbitrary"), name="proj_ffn")(
            x, y, wo, g.reshape(1, d), wg, wu, wd)


def _rwkv_prep_kernel(*refs, seq, tm, has_v):
    if has_v:
        (x_ref, xp_ref, g_ref, mu_ref, w0_ref, w1_ref, w2_ref, a0_ref, a1_ref, a2_ref,
         g1_ref, g2_ref, v0_ref, v1_ref, v2_ref, xs_ref, lw_ref, a_ref, gate_ref, vg_ref) = refs
    else:
        (x_ref, xp_ref, g_ref, mu_ref, w0_ref, w1_ref, w2_ref, a0_ref, a1_ref, a2_ref,
         g1_ref, g2_ref, xs_ref, lw_ref, a_ref, gate_ref) = refs
    i = pl.program_id(0)
    gn = g_ref[...]
    h = _rms(x_ref[...], gn)
    hp = _rms(xp_ref[...], gn)
    seq_start = (i * tm) % seq == 0
    hp_row = jnp.where(seq_start, 0.0, hp[7:8, :])
    row = lax.broadcasted_iota(jnp.int32, (tm, 1), 0)
    h_prev = jnp.where(row == 0, hp_row, pltpu.roll(h, 1, 0))
    xx = h_prev - h
    mix = lambda n: h + xx * mu_ref[n:n + 1, :]
    xs_ref[0] = mix(0).astype(BF16)
    xs_ref[1] = mix(2).astype(BF16)
    xv = mix(3).astype(BF16)
    xs_ref[2] = xv
    lw_ref[...] = -RWKV_DECAY_SCALE * _sigmoid(
        w0_ref[...] + _dot(jnp.tanh(_dot(mix(1), w1_ref[...])), w2_ref[...]))
    a_ref[...] = _sigmoid(a0_ref[...] + _dot(_dot(mix(4), a1_ref[...]), a2_ref[...])).astype(a_ref.dtype)
    gate_ref[...] = _dot(_sigmoid(_dot(mix(5), g1_ref[...])), g2_ref[...]).astype(gate_ref.dtype)
    if has_v:
        vg_ref[...] = _sigmoid(v0_ref[...] + _dot(_dot(xv, v1_ref[...]), v2_ref[...])).astype(vg_ref.dtype)


def _pad_lora(w_in, w_out):
    r = w_in.shape[1]
    rp = -(-r // LANES) * LANES
    return (jnp.pad(w_in, ((0, 0), (0, rp - r))).astype(BF16),
            jnp.pad(w_out, ((0, rp - r), (0, 0))).astype(BF16))


def _rwkv_prep(x, norm_g, mu, w0, w1, w2, a0, a1, a2, g1, g2, vres, seq, tm=256):
    t, d = x.shape
    tm = min(tm, seq)
    assert t % tm == 0 and seq % tm == 0 and tm % 8 == 0
    has_v = vres is not None
    row = lambda v: v.reshape(1, d)
    full = lambda a: pl.BlockSpec(a.shape, lambda i: (0,) * a.ndim)
    w1p, w2p = _pad_lora(w1, w2)
    a1p, a2p = _pad_lora(a1, a2)
    g1p, g2p = _pad_lora(g1, g2)
    mu8 = jnp.pad(mu, ((0, 2), (0, 0)))
    params = [row(norm_g), mu8, row(w0), w1p, w2p, row(a0), a1p, a2p, g1p, g2p]
    if has_v:
        v1p, v2p = _pad_lora(vres[1], vres[2])
        params += [row(vres[0]), v1p, v2p]
    tile = pl.BlockSpec((tm, d), lambda i: (i, 0))
    in_specs = [tile, pl.BlockSpec((8, d), lambda i: (jnp.maximum(i * (tm // 8) - 1, 0), 0))]
    in_specs += [full(p) for p in params]
    n_gates = 3 if has_v else 2
    out_shape = ([jax.ShapeDtypeStruct((3, t, d), BF16), jax.ShapeDtypeStruct((t, d), F32)]
                 + [jax.ShapeDtypeStruct((t, d), BF16)] * n_gates)
    out_specs = [pl.BlockSpec((3, tm, d), lambda i: (0, i, 0))] + [tile] * (1 + n_gates)
    return pl.pallas_call(
        functools.partial(_rwkv_prep_kernel, seq=seq, tm=tm, has_v=has_v),
        grid=(t // tm,), in_specs=in_specs, out_specs=out_specs, out_shape=out_shape,
        compiler_params=_cparams("parallel"), name="rwkv_prep")(x, x, *params)


def _seg_sum(x, seg):
    w = x.shape[1]
    hi = x.astype(BF16)
    lo = (x - hi.astype(F32)).astype(BF16)
    d = lambda p, q: jnp.dot(p[:, q:q + 256], seg, preferred_element_type=F32)
    return jnp.concatenate([d(hi, q) + d(lo, q) for q in range(0, w, 256)], axis=1)


def _rwkv_scan_kernel(*refs, L, tc, has_v):
    (s_ref, a2_ref, r2_ref, b2_ref, k2_ref, v2_ref, pl_ref, phi_ref, psi_ref, theta_ref, yloc_ref,
     y_ref, bonus_ref, mab_ref, tinv_ref, mak_ref, mrb_ref, mrk_ref, av_ref, pr_ref) = refs[-20:]
    refs = refs[:-20]
    if has_v:
        (xs_ref, xsn_ref, w_ref, lw_ref, a_ref, gate_ref, vf_ref, vg_ref,
         kk_ref, ka_ref, rk_ref, lng_ref, lnb_ref, o_ref) = refs
    else:
        (xs_ref, xsn_ref, w_ref, lw_ref, a_ref, gate_ref,
         kk_ref, ka_ref, rk_ref, lng_ref, lnb_ref, o_ref, vout_ref) = refs
    W = lw_ref.shape[-1]
    P2 = 2 * L
    npair = W // P2
    N = RWKV_HEAD
    step = pl.program_id(2)

    @pl.when(step == 0)
    def _():
        s_ref[...] = jnp.zeros_like(s_ref)
        for n in range(3):
            pr_ref[0, n] = jnp.dot(xs_ref[n], w_ref[n], preferred_element_type=F32)

    cur = step % 2
    r, k, v = (pr_ref[cur, n] for n in range(3))
    pending = [(n, c0) for n in range(3) for c0 in range(0, W, RWKV_PROJ_PIECE)]

    def project_piece():
        if pending:
            n, c0 = pending.pop(0)
            cols = slice(c0, c0 + RWKV_PROJ_PIECE)
            pr_ref[1 - cur, n, :, cols] = jnp.dot(xsn_ref[n], w_ref[n, :, cols], preferred_element_type=F32)

    ri = lax.broadcasted_iota(jnp.int32, (P2, P2), 0)
    ci = lax.broadcasted_iota(jnp.int32, (P2, P2), 1)
    strict = ri > ci
    incl = ri >= ci
    eye = (ri == ci).astype(F32)
    levels = []
    s = 1
    while s < L:
        levels.append(((ri // s) % 2 == 1) & ((ci // s) == (ri // s) - 1))
        s *= 2
    head0 = lax.broadcasted_iota(jnp.int32, (L, P2), 1) < N
    sr = lax.broadcasted_iota(jnp.int32, (256, 256), 0) // N
    sc = lax.broadcasted_iota(jnp.int32, (256, 256), 1) // N
    seg = (sr == sc).astype(BF16)

    nc = tc // L
    tr = lax.broadcasted_iota(jnp.int32, (tc, tc), 0)
    tcol = lax.broadcasted_iota(jnp.int32, (tc, tc), 1)
    tri = jnp.where(tr >= tcol, jnp.where(tr // L == tcol // L, 1.0, 0.0), 0.0).astype(BF16)
    del tr, tcol

    lw = lw_ref[...]
    a = a_ref[...].astype(F32)
    if has_v:
        v = v + (vf_ref[...] - v) * vg_ref[...].astype(F32)
    else:
        vout_ref[...] = v
    kk = k * kk_ref[...]
    kk = kk / jnp.maximum(jnp.sqrt(_seg_sum(kk * kk, seg)), 1e-12)
    kmod = k * (1.0 + (a - 1.0) * ka_ref[...])
    c = _dot_exact_lhs(tri, lw)
    enc = jnp.exp(-c)
    bonus_ref[...] = _seg_sum(r * kmod * rk_ref[...], seg) * v
    operands = (-kk * jnp.exp(c - lw), r * jnp.exp(c), kk * a * enc, kmod * enc, v)
    for z, z_ref in zip(operands, (a2_ref, r2_ref, b2_ref, k2_ref, v2_ref)):
        for ch in range(nc):
            for p in range(npair):
                zz = z[ch * L:(ch + 1) * L, P2 * p:P2 * (p + 1)]
                z_ref[ch, p, :L] = jnp.where(head0, zz, 0.0).astype(BF16)
                z_ref[ch, p, L:] = jnp.where(head0, 0.0, zz).astype(BF16)
    for ch in range(nc):
        pl_ref[ch] = jnp.broadcast_to(jnp.exp(c[(ch + 1) * L - 1:(ch + 1) * L, :]), (8, W))
    del r, k, v, lw, a, kk, kmod, c, enc, operands

    chains = [(ch, p) for ch in range(nc) for p in range(npair)]
    for g0 in range(0, len(chains), RWKV_CHAINS):
        group = chains[g0:g0 + RWKV_CHAINS]
        for c in group:
            g = _dot_nt(jnp.concatenate([a2_ref[c], r2_ref[c]], axis=0),
                        jnp.concatenate([b2_ref[c], k2_ref[c]], axis=0))
            m_ab = jnp.where(strict, g[:P2, :P2], 0.0)
            mab_ref[c] = m_ab
            tinv_ref[c] = eye + jnp.where(levels[0], m_ab, 0.0)
            mak_ref[c] = jnp.where(strict, g[:P2, P2:], 0.0).astype(BF16)
            mrb_ref[c] = jnp.where(incl, g[P2:, :P2], 0.0).astype(BF16)
            mrk_ref[c] = jnp.where(incl, g[P2:, P2:], 0.0).astype(BF16)
        for lv in levels[1:]:
            steps = [_dot(jnp.where(lv, mab_ref[c], 0.0), tinv_ref[c]).astype(BF16) for c in group]
            for c, step in zip(group, steps):
                t_cur = tinv_ref[c]
                tinv_ref[c] = t_cur + _dot(t_cur, step)
        mvs = [_dot(mak_ref[c], v2_ref[c]).astype(BF16) for c in group]
        for c, mv in zip(group, mvs):
            av_ref[c] = _dot(tinv_ref[c], jnp.concatenate([a2_ref[c], mv], axis=1)).astype(BF16)
        ths = [_dot(mrb_ref[c], av_ref[c]) for c in group]
        yls = [_dot(mrk_ref[c], v2_ref[c]) for c in group]
        for c, th, yl in zip(group, ths, yls):
            theta_ref[c] = (r2_ref[c].astype(F32) + th[:, :P2]).astype(BF16)
            yloc_ref[c] = th[:, P2:] + yl
        for c in group:
            av = av_ref[c]
            pp = _dot_tn(jnp.concatenate([av[:, P2:], av[:, :P2]], axis=1), b2_ref[c])
            p_last = pl_ref[c[0]][0:1, P2 * c[1]:P2 * (c[1] + 1)]
            phi_ref[c] = ((eye + pp[P2:]) * p_last).astype(BF16)
            psi_ref[c] = (pp[:P2] + _dot_tn(v2_ref[c], k2_ref[c])) * p_last

    for ch in range(nc):
        for p in range(npair):
            s0 = s_ref[p]
            yo = _dot_nt(theta_ref[ch, p], s0) + yloc_ref[ch, p]
            y_ref[ch * L:(ch + 1) * L, P2 * p:P2 * (p + 1)] = yo[:L] + yo[L:]
            s_ref[p] = _dot(s0, phi_ref[ch, p]) + psi_ref[ch, p]
        project_piece()

    y = y_ref[...]
    mean = _seg_sum(y, seg) * (1.0 / N)
    project_piece()
    yc = y - mean
    var = _seg_sum(yc * yc, seg) * (1.0 / N)
    project_piece()
    yn = yc * lax.rsqrt(var + RWKV_GN_EPS) * lng_ref[...] + lnb_ref[...]
    o_ref[...] = ((yn + bonus_ref[...]) * gate_ref[...].astype(F32)).astype(o_ref.dtype)
    while pending:
        project_piece()


def _rwkv_scan(xs, w_rkv, lw, a, gate, v_first, vgate, k_k, k_a, r_k, lnx_g, lnx_b, bsz, seq, tc=256, wb=512):
    _, t, d = xs.shape
    L = RWKV_CHUNK
    tc, wb = min(tc, seq), min(wb, d)
    assert seq % tc == 0 and tc % L == 0 and d % wb == 0 and wb % 256 == 0
    has_v = v_first is not None
    nc, p2, npair = tc // L, 2 * L, wb // (2 * L)
    b3 = lambda z: z.reshape(bsz, seq, d)
    blk = pl.BlockSpec((None, tc, wb), lambda b, h, c: (b, c, h))
    prow = pl.BlockSpec((1, wb), lambda b, h, c: (0, h))
    xs4 = xs.reshape(3, bsz, seq, d)
    args = [xs4, xs4, w_rkv, b3(lw), b3(a), b3(gate)]
    in_specs = [pl.BlockSpec((3, None, tc, d), lambda b, h, c: (0, b, c, 0)),
                pl.BlockSpec((3, None, tc, d), lambda b, h, c: (0, b, jnp.minimum(c + 1, seq // tc - 1), 0)),
                pl.BlockSpec((3, d, wb), lambda b, h, c: (0, 0, h)),
                blk, blk, blk]
    if has_v:
        args += [b3(v_first), b3(vgate)]
        in_specs += [blk, blk]
    args += [z.reshape(1, d) for z in (k_k, k_a, r_k, lnx_g, lnx_b)]
    in_specs += [prow] * 5
    out_shape = [jax.ShapeDtypeStruct((bsz, seq, d), BF16)]
    if not has_v:
        out_shape.append(jax.ShapeDtypeStruct((bsz, seq, d), F32))
    out = pl.pallas_call(
        functools.partial(_rwkv_scan_kernel, L=L, tc=tc, has_v=has_v),
        grid=(bsz, d // wb, seq // tc), in_specs=in_specs, out_specs=[blk] * len(out_shape),
        out_shape=out_shape,
        scratch_shapes=[pltpu.VMEM((npair, p2, p2), F32)]
        + [pltpu.VMEM((nc, npair, p2, p2), BF16)] * 5
        + [pltpu.VMEM((nc, 8, wb), F32),
           pltpu.VMEM((nc, npair, p2, p2), BF16), pltpu.VMEM((nc, npair, p2, p2), F32),
           pltpu.VMEM((nc, npair, p2, p2), BF16), pltpu.VMEM((nc, npair, p2, p2), F32),
           pltpu.VMEM((tc, wb), F32), pltpu.VMEM((tc, wb), F32),
           pltpu.VMEM((nc, npair, p2, p2), F32), pltpu.VMEM((nc, npair, p2, p2), F32),
           pltpu.VMEM((nc, npair, p2, p2), BF16), pltpu.VMEM((nc, npair, p2, p2), BF16),
           pltpu.VMEM((nc, npair, p2, p2), BF16),
           pltpu.VMEM((nc, npair, p2, 2 * p2), BF16),
           pltpu.VMEM((2, 3, tc, wb), F32)],
        compiler_params=_cparams("parallel", "parallel", "arbitrary"), name="rwkv_scan")(*args)
    return out[0].reshape(t, d), (None if has_v else out[1].reshape(t, d))


def _rwkv_layer(x, norm_g, p, vres, v_first, bsz, seq):
    (mu, w_rkv, w0, w1, w2, a0, a1, a2, g1, g2, k_k, k_a, r_k, lnx_g, lnx_b, w_o) = p
    outs = _rwkv_prep(x, norm_g, mu, w0, w1, w2, a0, a1, a2, g1, g2, vres, seq)
    xs, lw, a, gate = outs[:4]
    vgate = outs[4] if vres is not None else None
    o, v_own = _rwkv_scan(xs, w_rkv.astype(BF16), lw, a, gate, v_first if vres is not None else None, vgate,
                          k_k, k_a, r_k, lnx_g, lnx_b, bsz, seq)
    return o, w_o.astype(BF16), (v_own if vres is None else v_first)


def _mlstm_kernel(x_ref, xn_ref, mg_ref, w_ref, wg_ref, bias_ref, ng_ref, out_ref, c_ref, m_ref, p_ref, z_ref,
                  *, nh, hps):
    lc = x_ref.shape[0]
    dv = ng_ref.shape[1] // hps
    dk = dv // 2
    step = pl.program_id(2)

    def project(src_ref, slot):
        hb = _rms(src_ref[...], mg_ref[...]).astype(BF16)
        p_ref[slot] = jnp.dot(hb, w_ref[...], preferred_element_type=F32).astype(BF16)
        z_ref[slot] = jnp.dot(hb, wg_ref[...], preferred_element_type=F32)

    @pl.when(step == 0)
    def _():
        c_ref[...] = jnp.zeros_like(c_ref)
        m_ref[...] = jnp.zeros_like(m_ref)
        project(x_ref, 0)

    cur = step % 2
    proj = p_ref[cur]
    z = z_ref[cur] + bias_ref[...]
    hb_next = _rms(xn_ref[...], mg_ref[...]).astype(BF16)
    z_ref[1 - cur] = jnp.dot(hb_next, wg_ref[...], preferred_element_type=F32)
    pending = list(range(0, w_ref.shape[1], MLSTM_PROJ_PIECE))

    def project_piece():
        if pending:
            c0 = pending.pop(0)
            cols = slice(c0, c0 + MLSTM_PROJ_PIECE)
            p_ref[1 - cur, :, cols] = jnp.dot(hb_next, w_ref[:, cols], preferred_element_type=F32).astype(BF16)

    q_of = lambda s: proj[:, s * dk:(s + 1) * dk]
    k_of = lambda s: proj[:, (hps + s) * dk:(hps + s + 1) * dk]
    v_of = lambda s: proj[:, 2 * hps * dk + s * dv:2 * hps * dk + (s + 1) * dv]
    o_of = lambda s: proj[:, 2 * hps * dk + (hps + s) * dv:2 * hps * dk + (hps + s + 1) * dv]

    lane = lax.broadcasted_iota(jnp.int32, (lc, LANES), 1)
    zc = MLSTM_GATE_CAP * jnp.tanh(z / MLSTM_GATE_CAP)
    lf_all = jnp.minimum(zc, 0.0) - jnp.log1p(jnp.exp(-jnp.abs(zc)))
    rr = lax.broadcasted_iota(jnp.int32, (lc, lc), 0)
    cc = lax.broadcasted_iota(jnp.int32, (lc, lc), 1)
    causal = rr >= cc
    bcum_all = _dot_exact_lhs(causal.astype(BF16), lf_all)
    project_piece()
    comb = jnp.where(lane < nh, zc, bcum_all)
    er = lax.broadcasted_iota(jnp.int32, (8, LANES), 0)
    ec = lax.broadcasted_iota(jnp.int32, (8, LANES), 1)
    head_of_row = pl.program_id(1) * hps + er // 2
    sel = jnp.where(er < 2 * hps, jnp.where(ec == head_of_row + nh * (er % 2), 1.0, 0.0), 0.0).astype(BF16)
    hi, mid, lo = _split3(comb)
    tr = lambda p: lax.dot_general(sel, p, (((1,), (1,)), ((), ())), preferred_element_type=F32)
    rows = tr(hi) + (tr(mid) + tr(lo))
    project_piece()
    ones_blk = (lane == 0).astype(BF16)

    for s in range(hps):
        h = pl.program_id(1) * hps + s
        li_col = jnp.sum(jnp.where(lane == h, comb, 0.0), axis=-1, keepdims=True)
        bc_col = jnp.sum(jnp.where(lane == h + nh, comb, 0.0), axis=-1, keepdims=True)
        li_row, bc_row = rows[2 * s:2 * s + 1], rows[2 * s + 1:2 * s + 2]
        m_st = m_ref[s, 0:1, 0:1]
        dmat = jnp.where(causal, bc_col - bc_row + li_row, -jnp.inf)
        inter = bc_col + m_st
        m_t = jnp.maximum(inter, jnp.max(dmat, axis=-1, keepdims=True))
        q = q_of(s) * (dk ** -0.5)
        k = k_of(s)
        sc = _dot_nt(q, k) * jnp.exp(dmat - m_t)
        project_piece()
        w_inter = jnp.exp(inter - m_t)
        v_ext = jnp.concatenate([v_of(s), ones_blk], axis=1)
        c_st = c_ref[s]
        nd = _dot(sc, v_ext) + w_inter * _dot(q, c_st)
        project_piece()
        den = nd[:, dv:dv + 1]
        hc = nd[:, :dv] / jnp.maximum(jnp.abs(den), jnp.exp(-m_t))

        b_tot = bc_col[lc - 1:lc]
        log_wk = b_tot - bc_col + li_col
        m_new = jnp.maximum(b_tot + m_st, jnp.max(log_wk, axis=0, keepdims=True))
        c_ref[s] = jnp.exp(b_tot + m_st - m_new) * c_st + _dot_tn(k * jnp.exp(log_wk - m_new), v_ext)
        m_ref[s] = jnp.broadcast_to(m_new, m_ref.shape[1:])
        project_piece()

        hn = hc * lax.rsqrt(jnp.mean(hc * hc, axis=-1, keepdims=True) + RMS_EPS) * ng_ref[:, s * dv:(s + 1) * dv]
        o_gate = _sigmoid(o_of(s).astype(F32))
        out_ref[:, s * dv:(s + 1) * dv] = (hn * o_gate).astype(out_ref.dtype)
    while pending:
        project_piece()


def _mlstm_layer(x, mix_g, w_in, b_if, norm_g, w_o, bsz, seq, lc=256):
    t, d = x.shape
    nh = MLSTM_HEADS
    dk, dv = d // 2 // nh, d // nh
    nq = 2 * nh * dk + 2 * nh * dv
    assert w_in.shape[1] == nq + 2 * nh and dk % LANES == 0
    lc = min(lc, seq)
    assert seq % lc == 0
    w_gates = jnp.pad(w_in[:, nq:], ((0, 0), (0, LANES - 2 * nh))).astype(BF16)
    bias = jnp.zeros((1, LANES), F32).at[0, :nh].set(b_if[0]).at[0, nh:2 * nh].set(b_if[1])
    hps = MLSTM_HEADS_PER_STEP
    assert nh % hps == 0
    ng = nh // hps
    heads = lambda base, width, g: w_in[:, base + g * hps * width:base + (g + 1) * hps * width]
    w_groups = jnp.stack([jnp.concatenate([heads(0, dk, g), heads(nh * dk, dk, g), heads(2 * nh * dk, dv, g),
                                           heads(2 * nh * dk + nh * dv, dv, g)], axis=1)
                          for g in range(ng)]).astype(BF16)
    out = pl.pallas_call(
        functools.partial(_mlstm_kernel, nh=nh, hps=hps),
        grid=(bsz, ng, seq // lc),
        in_specs=[pl.BlockSpec((None, lc, d), lambda b, h, c: (b, c, 0)),
                  pl.BlockSpec((None, lc, d), lambda b, h, c: (b, jnp.minimum(c + 1, seq // lc - 1), 0)),
                  pl.BlockSpec((1, d), lambda b, h, c: (0, 0)),
                  pl.BlockSpec((None, d, w_groups.shape[2]), lambda b, h, c: (h, 0, 0)),
                  pl.BlockSpec((d, LANES), lambda b, h, c: (0, 0)),
                  pl.BlockSpec((1, LANES), lambda b, h, c: (0, 0)),
                  pl.BlockSpec((1, hps * dv), lambda b, h, c: (0, h))],
        out_specs=pl.BlockSpec((None, lc, hps * dv), lambda b, h, c: (b, c, h)),
        out_shape=jax.ShapeDtypeStruct((bsz, seq, nh * dv), BF16),
        scratch_shapes=[pltpu.VMEM((hps, dk, dv + LANES), F32), pltpu.VMEM((hps, 8, LANES), F32),
                        pltpu.VMEM((2, lc, w_groups.shape[2]), BF16), pltpu.VMEM((2, lc, LANES), F32)],
        compiler_params=_cparams("parallel", "parallel", "arbitrary"), name="mlstm")(
            x.reshape(bsz, seq, d), x.reshape(bsz, seq, d), mix_g.reshape(1, d), w_groups, w_gates, bias,
            norm_g.reshape(1, nh * dv))
    return out.reshape(t, nh * dv), w_o.astype(BF16)


DSA_TQ = 128
DSA_TK = 256
DSA_HEAD_GROUP = 4
DSA_TILES_PER_TRIP = 4
DSA_PROJ_GROUPS = 4
DSA_NEG = -1e30
KEY_NEG_INF = -2139095041
IDX_BIG = 2 ** 30
LOG2E = math.log2(math.e)


def _dsa_proj_kernel(x_ref, g_ref, w_ref, qg_ref, kg_ref, qn_ref, qi_ref, k_ref, v_ref, ki_ref, wi_ref, h_ref,
                     *, nh, nih):
    dh = DSA_HEAD_DIM
    gs = DSA_PROJ_GROUPS
    j = pl.program_id(2)
    nq, nqi = nh // gs, nih // gs

    @pl.when(j == 0)
    def _():
        h_ref[...] = _rms(x_ref[...], g_ref[...]).astype(BF16)

    y = jnp.dot(h_ref[...], w_ref[...], preferred_element_type=F32)
    part = lambda n: y[:, n * dh:(n + 1) * dh]

    @pl.when(j < nq)
    def _():
        qscale = dh ** -0.5 * LOG2E
        for n in range(gs):
            qn_ref[n] = (_rms(part(n), qg_ref[...]) * qscale).astype(BF16)

    @pl.when(jnp.logical_and(j >= nq, j < nq + nqi))
    def _():
        for n in range(gs):
            qi_ref[n] = part(n).astype(BF16)

    @pl.when(j == nq + nqi)
    def _():
        k_ref[...] = _rms(part(0), kg_ref[...]).astype(BF16)
        v_ref[:, :dh] = part(1).astype(BF16)
        v_ref[:, dh:] = jnp.ones((v_ref.shape[0], LANES), BF16)
        ki_ref[...] = part(2).astype(BF16)
        wi_ref[...] = part(3)


def _dsa_kernel(qn_ref, qi_ref, wi_ref, k_ref, v_ref, ki_ref, nb_ref, o_ref,
                key_ref, hi_ref, lo_ref, lo2_ref, w_ref, acc_ref, m_ref, *, nh, nih, n_sel, idx_bits):
    tq, tk, dh = DSA_TQ, DSA_TK, DSA_HEAD_DIM
    hg = DSA_HEAD_GROUP
    q0 = pl.program_id(1) * tq
    jd = (q0 + tq - 1) // tk
    nt = jd + 1
    rowi = lax.broadcasted_iota(jnp.int32, (tq, tk), 0)
    coli = lax.broadcasted_iota(jnp.int32, (tq, tk), 1)
    ktile = lambda jt: pl.ds(pl.multiple_of(jt * tk, tk), tk)
    twice = lambda z: jnp.concatenate([z] * (tk // LANES), axis=1)

    wi = wi_ref[...]
    for h in range(nih):
        w_ref[h] = jnp.broadcast_to(wi[:, h:h + 1], (tq, LANES))

    def for_each_tile(n, body, per_trip=DSA_TILES_PER_TRIP):
        def trip(i, c):
            for u in range(per_trip):
                body(per_trip * i + u)
            return c
        lax.fori_loop(0, n // per_trip, trip, 0)
        done = (n // per_trip) * per_trip
        for u in range(per_trip - 1):
            @pl.when(n - done > u)
            def _():
                body(done + u)

    def score_tile(jt):
        ki_t = ki_ref[ktile(jt), :]
        score = jnp.zeros((tq, tk), F32)
        for g0 in range(0, nih, hg):
            lg = _dot_nt(qi_ref[g0:g0 + hg].reshape(hg * tq, LANES), ki_t)
            for h in range(hg):
                score = score + jnp.maximum(lg[h * tq:(h + 1) * tq], 0.0) * twice(w_ref[g0 + h])
        score = jnp.where(jt * tk + coli <= q0 + rowi, score, -jnp.inf)
        bits = pltpu.bitcast(score, jnp.int32)
        key = bits ^ ((bits >> 31) & 0x7FFFFFFF)
        key_ref[jt] = key
        key_t = pltpu.bitcast(jnp.transpose(pltpu.bitcast(key, F32)), jnp.int32)
        hi_ref[jt] = (key_t >> 16).astype(jnp.int16)
        lo_ref[jt] = ((key_t & 0xFFFF) - 32768).astype(jnp.int16)

    for_each_tile(nt, score_tile)

    unroll = DSA_TILES_PER_TRIP
    n_trips = (nt + unroll - 1) // unroll
    floor16 = jnp.full((tk, tq), -32768, jnp.int16)

    def pad_tiles(ref):
        for u in range(unroll - 1):
            @pl.when(n_trips * unroll - nt > u)
            def _():
                ref[nt + u] = floor16

    def count16(ref, cand):
        cb = jnp.broadcast_to(cand.astype(jnp.int16), (32, tq))

        def body(i, acc):
            for u in range(unroll):
                tile = ref[unroll * i + u]
                for r0 in range(0, tk, 32):
                    acc = acc + jnp.where(tile[r0:r0 + 32] >= cb, jnp.int16(1), jnp.int16(0))
            return acc
        acc = lax.fori_loop(0, n_trips, body, jnp.zeros((32, tq), jnp.int16))
        return jnp.sum(acc.astype(F32), axis=0, keepdims=True)

    pad_tiles(hi_ref)

    def kth_largest16(ref, kth):
        def bit(b, prefix):
            cand = prefix + jnp.left_shift(jnp.int32(1), 15 - b)
            return jnp.where(count16(ref, cand) >= kth, cand, prefix)
        return lax.fori_loop(0, 16, bit, jnp.full((1, tq), -32768, jnp.int32))

    hi_k = kth_largest16(hi_ref, n_sel)
    above = jnp.where(hi_k >= 32767, 0.0, count16(hi_ref, jnp.minimum(hi_k + 1, 32767)))
    hi_kb = jnp.broadcast_to(hi_k.astype(jnp.int16), (tk, tq))

    def bucket_tile(jt, c):
        lo2_ref[jt] = jnp.where(hi_ref[jt] == hi_kb, lo_ref[jt], floor16)
        return c

    lax.fori_loop(0, nt, bucket_tile, 0)
    pad_tiles(lo2_ref)
    lo_k = kth_largest16(lo2_ref, n_sel - above)
    thr_row = hi_k * 65536 + (lo_k + 32768)
    tw = pltpu.bitcast(jnp.transpose(pltpu.bitcast(jnp.broadcast_to(thr_row, (tq, tq)), F32)), jnp.int32)
    thr = tw[:, :1]

    def count(hit):
        def body(jt, acc):
            keyt = key_ref[jt]
            for c0 in range(0, tk, LANES):
                acc = acc + hit(keyt[:, c0:c0 + LANES], jt * tk + c0)
            return acc
        acc = lax.fori_loop(0, nt, body, jnp.zeros((tq, LANES), F32))
        return jnp.sum(acc, axis=-1, keepdims=True)

    wide = lambda col: jnp.broadcast_to(col, (tq, LANES))
    lane = lax.broadcasted_iota(jnp.int32, (tq, LANES), 1)

    c_gt = count(lambda kv, base: jnp.where(kv > tw, 1.0, 0.0))
    c_ge = count(lambda kv, base: jnp.where(kv >= tw, 1.0, 0.0))
    need = n_sel - c_gt
    c_eq = c_ge - c_gt

    def tie_search():
        def index_bit(b, jcur):
            cand = jcur + jnp.left_shift(jnp.int32(1), idx_bits - 1 - b)
            cw = wide(cand)
            f = count(lambda kv, base: jnp.where(kv == tw, jnp.where(base + lane < cw, 1.0, 0.0), 0.0))
            return jnp.where(f <= need, cand, jcur)
        return lax.fori_loop(0, idx_bits, index_bit, jnp.zeros((tq, 1), jnp.int32))

    excess = jnp.max(c_eq - need) > 0.0
    jc = lax.cond(excess, tie_search, lambda: jnp.full((tq, 1), IDX_BIG, jnp.int32))
    jc = jnp.where(c_eq > need, jc, IDX_BIG)
    jc = jnp.where(thr == KEY_NEG_INF, 0, jc)

    m_ref[...] = jnp.full_like(m_ref, DSA_NEG)
    acc_ref[...] = jnp.zeros_like(acc_ref)

    def attend_tile(jt, near_idx):
        k_t = k_ref[ktile(jt), :]
        v_t = v_ref[ktile(jt), :]
        keyt = key_ref[jt]
        tie_mb = jnp.where(keyt == thr, jnp.where(jt * tk + coli < jc, 0.0, DSA_NEG), DSA_NEG)
        mb = jnp.where(keyt > thr, 0.0, tie_mb)
        for g0 in range(0, nh, hg):
            s_g = _dot_nt(qn_ref[g0:g0 + hg].reshape(hg * tq, dh), k_t)
            ps, alphas = [], []
            for h in range(hg):
                rows = slice((g0 + h) * tq, (g0 + h + 1) * tq)
                s = s_g[h * tq:(h + 1) * tq] + (mb if near_idx is None else nb_ref[near_idx, g0 + h] + mb)
                m_old = m_ref[rows, :]
                m_new = jnp.maximum(m_old, jnp.max(s, axis=-1, keepdims=True))
                m_ref[rows, :] = m_new
                ps.append(jnp.exp2(s - twice(m_new)).astype(BF16))
                alphas.append(jnp.exp2(m_old - m_new))
            pv = jnp.dot(jnp.concatenate(ps, axis=0), v_t, preferred_element_type=F32)
            rows_g = slice(g0 * tq, (g0 + hg) * tq)
            acc_ref[rows_g, :] = acc_ref[rows_g, :] * twice(jnp.concatenate(alphas, axis=0)) + pv

    on_tile_edge = q0 == jd * tk
    has_prev_near = jnp.logical_and(on_tile_edge, jd >= 1)
    n_far = jnp.where(has_prev_near, jd - 1, jd)

    for_each_tile(n_far, lambda jt: attend_tile(jt, None))

    @pl.when(has_prev_near)
    def _():
        attend_tile(jd - 1, 2)
        attend_tile(jd, 0)

    @pl.when(jnp.logical_not(has_prev_near))
    def _():
        attend_tile(jd, jnp.where(on_tile_edge, 0, 1))

    for h in range(nh):
        rows = slice(h * tq, (h + 1) * tq)
        o_ref[:, h * dh:(h + 1) * dh] = (acc_ref[rows, :dh] / acc_ref[rows, dh:]).astype(o_ref.dtype)


def _t5_bucket(rel):
    n = jnp.maximum(rel, 0)
    exact = T5_BUCKETS // 2
    nf = jnp.maximum(n, exact).astype(F32)
    large = exact + (jnp.log(nf / exact) / math.log(T5_MAX_DISTANCE / exact)
                     * (T5_BUCKETS - exact)).astype(jnp.int32)
    return jnp.where(n < exact, n, jnp.minimum(large, T5_BUCKETS - 1))


def _dsa_layer(x, mix_g, w_in, q_norm_g, k_norm_g, t5_table, w_o, bsz, seq):
    t, d = x.shape
    nh, dh, nih, di = DSA_HEADS, DSA_HEAD_DIM, IDX_HEADS, IDX_DIM
    tq, tk = DSA_TQ, DSA_TK
    assert seq % (tk * DSA_TILES_PER_TRIP) == 0 and tk == 2 * tq and dh == LANES and di <= LANES and nih <= LANES
    assert T5_MAX_DISTANCE <= tq
    n_sel = min(DSA_TOPK_MAX, seq // 4)
    o1, o2, o3, o4, o5 = nh * dh, nh * dh + dh, nh * dh + 2 * dh, nh * dh + 2 * dh + nih * di, nh * dh + 2 * dh + nih * di + di
    w_qi = jnp.pad(w_in[:, o3:o4].reshape(d, nih, di), ((0, 0), (0, 0), (0, LANES - di))).reshape(d, nih * LANES)
    w_ki = jnp.pad(w_in[:, o4:o5], ((0, 0), (0, LANES - di)))
    w_wi = jnp.pad(w_in[:, o5:o5 + nih] * (nih ** -0.5 * di ** -0.5), ((0, 0), (0, LANES - nih)))
    w = jnp.concatenate([w_in[:, :o1], w_qi, w_in[:, o1:o3], w_ki, w_wi], axis=1).astype(BF16)
    gs = DSA_PROJ_GROUPS
    nq, nqi = nh // gs, nih // gs
    assert gs == 4 and w.shape[1] == gs * dh * (nq + nqi + 1) and nh % gs == 0 and nih % gs == 0
    tm = min(1024, seq)
    assert seq % tm == 0
    tok = lambda width: pl.BlockSpec((None, tm, width), lambda b, i, j: (b, i, 0))
    qn, qi, kn, vb, kib, wis = pl.pallas_call(
        functools.partial(_dsa_proj_kernel, nh=nh, nih=nih),
        grid=(bsz, seq // tm, nq + nqi + 1),
        in_specs=[pl.BlockSpec((None, tm, d), lambda b, i, j: (b, i, 0)),
                  pl.BlockSpec((1, d), lambda b, i, j: (0, 0)),
                  pl.BlockSpec((d, gs * dh), lambda b, i, j: (0, j)),
                  pl.BlockSpec((1, dh), lambda b, i, j: (0, 0)),
                  pl.BlockSpec((1, dh), lambda b, i, j: (0, 0))],
        out_specs=[pl.BlockSpec((None, gs, tm, dh), lambda b, i, j: (b, jnp.minimum(j, nq - 1), i, 0)),
                   pl.BlockSpec((None, gs, tm, LANES), lambda b, i, j: (b, jnp.clip(j - nq, 0, nqi - 1), i, 0)),
                   tok(dh), tok(dh + LANES), tok(LANES), tok(LANES)],
        out_shape=[jax.ShapeDtypeStruct((bsz, nh, seq, dh), BF16),
                   jax.ShapeDtypeStruct((bsz, nih, seq, LANES), BF16),
                   jax.ShapeDtypeStruct((bsz, seq, dh), BF16),
                   jax.ShapeDtypeStruct((bsz, seq, dh + LANES), BF16),
                   jax.ShapeDtypeStruct((bsz, seq, LANES), BF16),
                   jax.ShapeDtypeStruct((bsz, seq, LANES), F32)],
        scratch_shapes=[pltpu.VMEM((tm, d), BF16)],
        compiler_params=_cparams("parallel", "parallel", "arbitrary"), name="dsa_proj")(
            x.reshape(bsz, seq, d), mix_g.reshape(1, d), w, q_norm_g.reshape(1, dh), k_norm_g.reshape(1, dh))

    ii = jnp.arange(tq, dtype=jnp.int32)[:, None]
    jj = jnp.arange(tk, dtype=jnp.int32)[None, :]
    buckets = jnp.stack([_t5_bucket(off + ii - jj) for off in (0, tq, 2 * tq)])
    rel_table = (t5_table - t5_table[T5_BUCKETS - 1]).astype(F32)
    near = jnp.einsum("otkb,bh->ohtk", jax.nn.one_hot(buckets, T5_BUCKETS, dtype=F32), rel_table,
                      precision=lax.Precision.HIGHEST) * LOG2E

    seqblk = lambda: pl.BlockSpec((None, seq, LANES), lambda b, i: (b, 0, 0))
    out = pl.pallas_call(
        functools.partial(_dsa_kernel, nh=nh, nih=nih, n_sel=n_sel, idx_bits=int(seq).bit_length()),
        grid=(bsz, seq // tq),
        in_specs=[pl.BlockSpec((None, nh, tq, dh), lambda b, i: (b, 0, i, 0)),
                  pl.BlockSpec((None, nih, tq, LANES), lambda b, i: (b, 0, i, 0)),
                  pl.BlockSpec((None, tq, LANES), lambda b, i: (b, i, 0)),
                  seqblk(), pl.BlockSpec((None, seq, dh + LANES), lambda b, i: (b, 0, 0)), seqblk(),
                  pl.BlockSpec((3, nh, tq, tk), lambda b, i: (0, 0, 0, 0))],
        out_specs=pl.BlockSpec((None, tq, nh * dh), lambda b, i: (b, i, 0)),
        out_shape=jax.ShapeDtypeStruct((bsz, seq, nh * dh), BF16),
        scratch_shapes=[pltpu.VMEM((seq // tk, tq, tk), jnp.int32),
                        pltpu.VMEM((seq // tk, tk, tq), jnp.int16),
                        pltpu.VMEM((seq // tk, tk, tq), jnp.int16),
                        pltpu.VMEM((seq // tk, tk, tq), jnp.int16),
                        pltpu.VMEM((nih, tq, LANES), F32),
                        pltpu.VMEM((nh * tq, dh + LANES), F32),
                        pltpu.VMEM((nh * tq, LANES), F32)],
        compiler_params=_cparams("parallel", "arbitrary"), name="dsa_attn")(
            qn, qi, wis, kn, vb, kib, near)
    return out.reshape(t, nh * dh), w_o.astype(BF16)


def kernel(x, rwkv_mu, rwkv_w_rkv, rwkv_w0, rwkv_w1, rwkv_w2, rwkv_a0, rwkv_a1, rwkv_a2, rwkv_v0, rwkv_v1,
           rwkv_v2, rwkv_g1, rwkv_g2, rwkv_k_k, rwkv_k_a, rwkv_r_k, rwkv_lnx_g, rwkv_lnx_b, rwkv_w_o,
           mlstm_w_in, mlstm_b_if, mlstm_norm_g, mlstm_w_o, dsa_w_in, dsa_q_norm_g, dsa_k_norm_g, dsa_w_o,
           t5_bias, mix_norm_g, ffn_norm_g, ffn_w_gate, ffn_w_up, ffn_w_down):
    bsz, seq, d = x.shape
    depth = mix_norm_g.shape[0]
    h = x.reshape(bsz * seq, d)
    v_first = None
    for i in range(depth):
        kind, j = i % 3, i // 3
        if kind == 0:
            vres = None if j == 0 else (rwkv_v0[j - 1], rwkv_v1[j - 1], rwkv_v2[j - 1])
            p = (rwkv_mu[j], rwkv_w_rkv[j], rwkv_w0[j], rwkv_w1[j], rwkv_w2[j], rwkv_a0[j], rwkv_a1[j],
                 rwkv_a2[j], rwkv_g1[j], rwkv_g2[j], rwkv_k_k[j], rwkv_k_a[j], rwkv_r_k[j],
                 rwkv_lnx_g[j], rwkv_lnx_b[j], rwkv_w_o[j])
            y, w_o, v_first = _rwkv_layer(h, mix_norm_g[i], p, vres, v_first, bsz, seq)
        elif kind == 1:
            y, w_o = _mlstm_layer(h, mix_norm_g[i], mlstm_w_in[j], mlstm_b_if[j], mlstm_norm_g[j],
                                  mlstm_w_o[j], bsz, seq)
        else:
            y, w_o = _dsa_layer(h, mix_norm_g[i], dsa_w_in[j], dsa_q_norm_g[j], dsa_k_norm_g[j], t5_bias,
                                dsa_w_o[j], bsz, seq)
        h = _proj_ffn(h, y, w_o, ffn_norm_g[i], ffn_w_gate[i].astype(BF16), ffn_w_up[i].astype(BF16),
                      ffn_w_down[i].astype(BF16))
    return h.reshape(bsz, seq, d)
```
